```python
import math
import jax, jax.numpy as jnp
from jax import lax
import numpy as np

D_MODEL = 1024
BATCH = 2
SEQ = 8192
DEPTH = 4

N_MIXERS = 3
HEAD_DIM = 64
PLE_DIM = 256
D_FF = 4 * D_MODEL
NORM_EPS = 1e-6
NEG_INF = -1e30
Q_BLOCK = 128

REL_BUCKETS = 32
REL_MAX_DIST = 128
REL_HEADS = D_MODEL // HEAD_DIM

DA_HEADS = D_MODEL // (2 * HEAD_DIM)
DA_IN = 3 * D_MODEL

NSA_HEADS = D_MODEL // HEAD_DIM
NSA_KV_GROUPS = 4
NSA_GROUP_SIZE = NSA_HEADS // NSA_KV_GROUPS
NSA_CMP_LEN = 32
NSA_CMP_STRIDE = 16
NSA_CMP_HIDDEN = 256
NSA_SEL_LEN = 64
NSA_TOP_N = 16
NSA_WINDOW = 512
NSA_Q_CHUNK = 64
NSA_FORCE_SCORE = 1e4
NSA_KV_DIM = NSA_KV_GROUPS * HEAD_DIM
NSA_IN = NSA_HEADS * HEAD_DIM + 6 * NSA_KV_DIM + 3 * NSA_HEADS

FOX_HEADS = D_MODEL // HEAD_DIM
FOX_IN = 3 * D_MODEL + FOX_HEADS

N_DA = (DEPTH + 2) // 3
N_NSA = (DEPTH + 1) // 3
N_FOX = DEPTH // 3

kernel_name = 'hybrid_diff_nsa_fox_trunk'


def rmsnorm(x, g):
    xf = x.astype(jnp.float32)
    y = xf * lax.rsqrt(jnp.mean(xf * xf, axis=-1, keepdims=True) + NORM_EPS)
    return (y * g.astype(jnp.float32)).astype(x.dtype)


def t5_bucket(dist):
    n = jnp.maximum(dist, 0)
    max_exact = REL_BUCKETS // 2
    nf = jnp.maximum(n, 1).astype(jnp.float32)
    large = max_exact + (jnp.log(nf / max_exact) / math.log(REL_MAX_DIST / max_exact)
                         * (REL_BUCKETS - max_exact)).astype(jnp.int32)
    large = jnp.minimum(large, REL_BUCKETS - 1)
    return jnp.where(n < max_exact, n, large)


def masked_softmax(s, mask):
    s = jnp.where(mask, s.astype(jnp.float32), NEG_INF)
    return jnp.where(mask, jax.nn.softmax(s, axis=-1), 0.0)


def query_blocks(a, blk):
    B, S = a.shape[:2]
    return jnp.moveaxis(a.reshape(B, S // blk, blk, *a.shape[2:]), 1, 0)


def merge_blocks(o):
    o = jnp.moveaxis(o, 0, 1)
    return o.reshape(o.shape[0], -1, *o.shape[3:])


def diff_attention(h, w_in, lam, subln_g, w_out, rel_bias, lam_init):
    B, S, _ = h.shape
    q, k, v = jnp.split(h @ w_in, 3, axis=-1)
    q = q.reshape(B, S, DA_HEADS, 2, HEAD_DIM)
    k = k.reshape(B, S, DA_HEADS, 2, HEAD_DIM)
    v = v.reshape(B, S, DA_HEADS, 2 * HEAD_DIM)
    lf = lam.astype(jnp.float32)
    lam_full = jnp.exp(jnp.sum(lf[0] * lf[1])) - jnp.exp(jnp.sum(lf[2] * lf[3])) + lam_init
    scale = HEAD_DIM ** -0.5
    key_pos = jnp.arange(S)

    def block(args):
        qb, q0 = args
        t = q0 + jnp.arange(Q_BLOCK)
        dist = t[:, None] - key_pos[None, :]
        mask = dist >= 0
        bias = rel_bias[t5_bucket(dist)].astype(jnp.float32)
        bias = jnp.moveaxis(bias, -1, 0).reshape(DA_HEADS, 2, Q_BLOCK, S)
        s = jnp.einsum('bqhcd,bkhcd->bhcqk', qb, k).astype(jnp.float32) * scale + bias[None]
        pr = masked_softmax(s, mask)
        a = pr[:, :, 0] - lam_full * pr[:, :, 1]
        return jnp.einsum('bhqk,bkhe->bqhe', a.astype(v.dtype), v)

    starts = jnp.arange(S // Q_BLOCK) * Q_BLOCK
    o = merge_blocks(lax.map(block, (query_blocks(q, Q_BLOCK), starts)))
    o = rmsnorm(o, subln_g) * (1.0 - lam_init)
    return o.reshape(B, S, D_MODEL) @ w_out


def nsa_attention(h, w_in, cmp_pe, cmp_w1, cmp_w2, w_out, rel_bias):
    B, S, _ = h.shape
    G, R, hd, Qc = NSA_KV_GROUPS, NSA_GROUP_SIZE, HEAD_DIM, NSA_Q_CHUNK
    proj = h @ w_in
    nq = NSA_HEADS * hd
    q = proj[..., :nq].reshape(B, S, G, R, hd)
    kv = proj[..., nq:nq + 6 * NSA_KV_DIM].reshape(B, S, 6, G, hd)
    k_cmp, v_cmp, k_sel, v_sel, k_win, v_win = [kv[:, :, i] for i in range(6)]
    gates = jax.nn.sigmoid(proj[..., nq + 6 * NSA_KV_DIM:].reshape(B, S, G, R, 3))
    scale = hd ** -0.5

    n_cmp = (S - NSA_CMP_LEN) // NSA_CMP_STRIDE + 1
    tok = jnp.arange(n_cmp)[:, None] * NSA_CMP_STRIDE + jnp.arange(NSA_CMP_LEN)[None, :]

    def compress(a, pe, w1, w2):
        blk = a[:, tok] + pe[None, None, :, None, :]
        blk = jnp.moveaxis(blk, 3, 2).reshape(B, n_cmp, G, NSA_CMP_LEN * hd)
        return jax.nn.gelu(blk @ w1) @ w2

    kc = compress(k_cmp, cmp_pe[0], cmp_w1[0], cmp_w2[0])
    vc = compress(v_cmp, cmp_pe[1], cmp_w1[1], cmp_w2[1])
    cmp_start = jnp.arange(n_cmp) * NSA_CMP_STRIDE
    cmp_end = cmp_start + NSA_CMP_LEN - 1

    n_sel_blk = S // NSA_SEL_LEN
    n_top = min(NSA_TOP_N, n_sel_blk)
    sel_start = jnp.arange(n_sel_blk) * NSA_SEL_LEN
    overlap = ((cmp_start[:, None] < sel_start[None, :] + NSA_SEL_LEN)
               & (cmp_end[:, None] >= sel_start[None, :])).astype(jnp.float32)
    ks_blk = jnp.moveaxis(k_sel.reshape(B, n_sel_blk, NSA_SEL_LEN, G, hd), 3, 1)
    vs_blk = jnp.moveaxis(v_sel.reshape(B, n_sel_blk, NSA_SEL_LEN, G, hd), 3, 1)

    pad = ((0, 0), (NSA_WINDOW, 0), (0, 0), (0, 0))
    kw_pad = jnp.pad(k_win, pad)
    vw_pad = jnp.pad(v_win, pad)

    rel = rel_bias.reshape(REL_BUCKETS, G, R)
    b_idx = jnp.arange(B)[:, None, None, None]
    g_idx = jnp.arange(G)[None, :, None, None]
    j_blk = jnp.arange(n_sel_blk)
    K_sel = n_top * NSA_SEL_LEN

    def chunk(args):
        qc, gc, q0 = args
        t = q0 + jnp.arange(Qc)
        d_c = t[:, None] - cmp_end[None, :]
        m_c = d_c >= 0
        b_c = jnp.moveaxis(rel[t5_bucket(d_c)], (2, 3), (0, 1)).astype(jnp.float32)
        s_c = jnp.einsum('bqgrd,bngd->bgrqn', qc, kc).astype(jnp.float32) * scale + b_c
        p_c = masked_softmax(s_c, m_c)
        o_c = jnp.einsum('bgrqn,bngd->bqgrd', p_c.astype(vc.dtype), vc)
        imp = jnp.einsum('bgrqn,nj->bgqj', p_c, overlap)
        cur = t // NSA_SEL_LEN
        forced = (j_blk[None, :] == 0) | (j_blk[None, :] == cur[:, None]) | (j_blk[None, :] == cur[:, None] - 1)
        valid = sel_start[None, :] <= t[:, None]
        imp = jnp.where(valid, jnp.where(forced, NSA_FORCE_SCORE, imp), -1.0)
        _, sel = lax.top_k(imp, n_top)
        k_g = ks_blk[b_idx, g_idx, sel].reshape(B, G, Qc, K_sel, hd)
        v_g = vs_blk[b_idx, g_idx, sel].reshape(B, G, Qc, K_sel, hd)
        pos = (sel[..., None] * NSA_SEL_LEN + jnp.arange(NSA_SEL_LEN)).reshape(B, G, Qc, K_sel)
        d_s = t[None, None, :, None] - pos
        m_s = (d_s >= 0)[:, :, None]
        b_s = jnp.moveaxis(rel[t5_bucket(d_s), g_idx], -1, 2).astype(jnp.float32)
        s_s = jnp.einsum('bqgrd,bgqkd->bgrqk', qc, k_g).astype(jnp.float32) * scale + b_s
        p_s = masked_softmax(s_s, m_s)
        o_s = jnp.einsum('bgrqk,bgqkd->bqgrd', p_s.astype(v_g.dtype), v_g)
        kw = lax.dynamic_slice_in_dim(kw_pad, q0, NSA_WINDOW + Qc, axis=1)
        vw = lax.dynamic_slice_in_dim(vw_pad, q0, NSA_WINDOW + Qc, axis=1)
        w_pos = q0 - NSA_WINDOW + jnp.arange(NSA_WINDOW + Qc)
        d_w = t[:, None] - w_pos[None, :]
        m_w = (d_w >= 0) & (d_w < NSA_WINDOW) & (w_pos[None, :] >= 0)
        b_w = jnp.moveaxis(rel[t5_bucket(d_w)], (2, 3), (0, 1)).astype(jnp.float32)
        s_w = jnp.einsum('bqgrd,bkgd->bgrqk', qc, kw).astype(jnp.float32) * scale + b_w
        p_w = masked_softmax(s_w, m_w)
        o_w = jnp.einsum('bgrqk,bkgd->bqgrd', p_w.astype(vw.dtype), vw)
        return gc[..., 0:1] * o_c + gc[..., 1:2] * o_s + gc[..., 2:3] * o_w

    starts = jnp.arange(S // Qc) * Qc
    o = merge_blocks(lax.map(chunk, (query_blocks(q, Qc), query_blocks(gates, Qc), starts)))
    return o.reshape(B, S, D_MODEL) @ w_out


def forgetting_attention(h, w_in, b_f, w_out):
    B, S, _ = h.shape
    proj = h @ w_in
    q = proj[..., :D_MODEL].reshape(B, S, FOX_HEADS, HEAD_DIM)
    k = proj[..., D_MODEL:2 * D_MODEL].reshape(B, S, FOX_HEADS, HEAD_DIM)
    v = proj[..., 2 * D_MODEL:3 * D_MODEL].reshape(B, S, FOX_HEADS, HEAD_DIM)
    log_f = jax.nn.log_sigmoid((proj[..., 3 * D_MODEL:] + b_f).astype(jnp.float32))
    c = jnp.cumsum(log_f, axis=1)
    c_k = jnp.moveaxis(c, 1, 2)
    scale = HEAD_DIM ** -0.5
    key_pos = jnp.arange(S)

    def block(args):
        qb, cq, q0 = args
        t = q0 + jnp.arange(Q_BLOCK)
        mask = t[:, None] >= key_pos[None, :]
        decay = jnp.moveaxis(cq, 1, 2)[..., None] - c_k[:, :, None, :]
        s = jnp.einsum('bqhd,bkhd->bhqk', qb, k).astype(jnp.float32) * scale + decay
        pr = masked_softmax(s, mask)
        return jnp.einsum('bhqk,bkhd->bqhd', pr.astype(v.dtype), v)

    starts = jnp.arange(S // Q_BLOCK) * Q_BLOCK
    o = merge_blocks(lax.map(block, (query_blocks(q, Q_BLOCK), query_blocks(c, Q_BLOCK), starts)))
    return o.reshape(B, S, D_MODEL) @ w_out


def sqrelu_mlp(h, w1, w2):
    return jnp.square(jax.nn.relu(h @ w1)) @ w2


def setup_inputs(seed: int = 0) -> dict:
    key = jax.random.key(seed)
    ks = list(jax.random.split(key, 24))

    def nrm(i, shape, scale):
        return jax.random.normal(ks[i], shape, jnp.float32) * scale

    D = D_MODEL
    return {
        'x': nrm(0, (BATCH, SEQ, D), 1.0),
        'p': nrm(1, (DEPTH, BATCH, SEQ, PLE_DIM), 1.0),
        'rel_bias': nrm(2, (REL_BUCKETS, REL_HEADS), 0.5),
        'norm_g': 1.0 + nrm(3, (DEPTH, 4, D), 0.02),
        'mlp_w1': nrm(4, (DEPTH, D, D_FF), D ** -0.5),
        'mlp_w2': nrm(5, (DEPTH, D_FF, D), D_FF ** -0.5),
        'ple_w': nrm(6, (DEPTH, PLE_DIM, D), PLE_DIM ** -0.5),
        'ple_gate_w': nrm(7, (DEPTH, D, D), D ** -0.5),
        'da_w_in': nrm(8, (N_DA, D, DA_IN), D ** -0.5),
        'da_lambda': nrm(9, (N_DA, 4, HEAD_DIM), 0.1),
        'da_subln': 1.0 + nrm(10, (N_DA, 2 * HEAD_DIM), 0.02),
        'da_w_out': nrm(11, (N_DA, D, D), D ** -0.5),
        'nsa_w_in': nrm(12, (N_NSA, D, NSA_IN), D ** -0.5),
        'nsa_cmp_pe': nrm(13, (N_NSA, 2, NSA_CMP_LEN, HEAD_DIM), 0.1),
        'nsa_cmp_w1': nrm(14, (N_NSA, 2, NSA_CMP_LEN * HEAD_DIM, NSA_CMP_HIDDEN), (NSA_CMP_LEN * HEAD_DIM) ** -0.5),
        'nsa_cmp_w2': nrm(15, (N_NSA, 2, NSA_CMP_HIDDEN, HEAD_DIM), NSA_CMP_HIDDEN ** -0.5),
        'nsa_w_out': nrm(16, (N_NSA, D, D), D ** -0.5),
        'fox_w_in': nrm(17, (N_FOX, D, FOX_IN), D ** -0.5),
        'fox_b_f': 3.0 + nrm(18, (N_FOX, FOX_HEADS), 1.0),
        'fox_w_out': nrm(19, (N_FOX, D, D), D ** -0.5),
    }


def reference(x, p, rel_bias, norm_g, mlp_w1, mlp_w2, ple_w, ple_gate_w,
              da_w_in, da_lambda, da_subln, da_w_out,
              nsa_w_in, nsa_cmp_pe, nsa_cmp_w1, nsa_cmp_w2, nsa_w_out,
              fox_w_in, fox_b_f, fox_w_out):
    ia, ib, ic = 0, 0, 0
    for i in range(DEPTH):
        g = norm_g[i]
        h = rmsnorm(x, g[0])
        kind = i % N_MIXERS
        if kind == 0:
            lam_init = 0.8 - 0.6 * math.exp(-0.3 * i)
            y = diff_attention(h, da_w_in[ia], da_lambda[ia], da_subln[ia], da_w_out[ia], rel_bias, lam_init)
            ia += 1
        elif kind == 1:
            y = nsa_attention(h, nsa_w_in[ib], nsa_cmp_pe[ib], nsa_cmp_w1[ib], nsa_cmp_w2[ib], nsa_w_out[ib], rel_bias)
            ib += 1
        else:
            y = forgetting_attention(h, fox_w_in[ic], fox_b_f[ic], fox_w_out[ic])
            ic += 1
        x = x + rmsnorm(y, g[1])
        y = sqrelu_mlp(rmsnorm(x, g[2]), mlp_w1[i], mlp_w2[i])
        x = x + rmsnorm(y, g[3])
        x = x + jax.nn.sigmoid(x @ ple_gate_w[i]) * (p[i] @ ple_w[i])
    return x
```

```python
import functools
import math

import jax
import jax.numpy as jnp
from jax import lax
from jax.experimental import pallas as pl
from jax.experimental.pallas import tpu as pltpu

BF = jnp.bfloat16
F32 = jnp.float32

D_MODEL = 1024
HEAD_DIM = 64
LANES = 128
NORM_EPS = 1e-6
MASK_VALUE = -1e30
REL_BUCKETS = 32
REL_MAX_DIST = 128
REL_TABLE_LEN = 512
N_HEADS = D_MODEL // HEAD_DIM
DA_HEADS = N_HEADS // 2
NSA_GROUPS = 4
NSA_REP = N_HEADS // NSA_GROUPS
NSA_CMP_LEN = 32
NSA_CMP_STRIDE = 16
NSA_SEL_LEN = 64
NSA_TOP_N = 16
NSA_WINDOW = 512
NSA_FORCE_SCORE = 1e4
N_MIXERS = 3

ATT_TQ = 256
ATT_TK = 256
VMEM_LIMIT = 56 * 1024 * 1024


def _cparams(*sem):
    return pltpu.CompilerParams(dimension_semantics=sem, vmem_limit_bytes=VMEM_LIMIT)


def _rms(x, g):
    return x * lax.rsqrt(jnp.mean(x * x, axis=-1, keepdims=True) + NORM_EPS) * g


def _dot(a, b):
    return jnp.dot(a, b, preferred_element_type=F32)


def _dot_nt(a, b):
    return lax.dot_general(a, b, (((1,), (1,)), ((), ())), preferred_element_type=F32)


def _dot_tn(a, b):
    return lax.dot_general(a, b, (((0,), (0,)), ((), ())), preferred_element_type=F32)


def _inproj_body(*refs, n_f32, n_gate, fox, tm, col_chunk):
    it = iter(refs)
    x_ref, g_ref, wn_ref, bn_ref, wt_ref = (next(it) for _ in range(5))
    wf32_ref = next(it) if n_f32 else None
    wg_ref = next(it) if n_gate else None
    if fox:
        wf_ref, bf_ref, pm_ref = next(it), next(it), next(it)
    on_ref, ot_ref = next(it), next(it)
    of32_ref = next(it) if n_f32 else None
    og_ref = next(it) if n_gate else None
    carry_ref = next(it) if fox else None

    h = _rms(x_ref[...], g_ref[...]).astype(BF)
    n_nat = on_ref.shape[1]
    k_off = n_nat // 2
    if fox:
        @pl.when(pl.program_id(1) == 0)
        def _():
            carry_ref[...] = jnp.zeros_like(carry_ref)

        lf = _dot(h, wf_ref[...]) + bf_ref[...]
        lane = lax.broadcasted_iota(jnp.int32, lf.shape, 1)
        row = lax.broadcasted_iota(jnp.int32, lf.shape, 0)
        ls = jnp.minimum(lf, 0.0) - jnp.log1p(jnp.exp(-jnp.abs(lf)))
        c = jnp.where(lane < 3 * N_HEADS, ls, 0.0)
        k = 1
        while k < tm:
            c = c + jnp.where(row >= k, pltpu.roll(c, k, axis=0), 0.0)
            k *= 2
        c = c + carry_ref[...]
        carry_ref[...] = c[tm - 1:tm, :]
        hi = c.astype(BF).astype(F32)
        r1 = c - hi
        mid = r1.astype(BF).astype(F32)
        lo = (r1 - mid).astype(BF).astype(F32)
        c3 = jnp.where(lane < N_HEADS, hi, jnp.where(lane < 2 * N_HEADS, mid, lo)).astype(BF)
    for c0 in range(0, n_nat, col_chunk):
        y = _dot(h, wn_ref[:, c0:c0 + col_chunk]) + bn_ref[:, c0:c0 + col_chunk]
        if fox and c0 >= k_off:
            y = y + _dot(c3, pm_ref[:, c0 - k_off:c0 - k_off + col_chunk])
        on_ref[:, c0:c0 + col_chunk] = y.astype(on_ref.dtype)
    ot_ref[0] = _dot_nt(wt_ref[...], h).astype(ot_ref.dtype)
    if n_f32:
        of32_ref[...] = _dot(h, wf32_ref[...])
    if n_gate:
        og_ref[0] = jax.nn.sigmoid(_dot_nt(wg_ref[...], h))


def _inproj(x2d, g, wn, bn, wt, *, B, S, tm=512, wf32=None, wg=None, fox_extra=None):
    M, D = x2d.shape
    ns = S // tm
    n_nat, n_t = wn.shape[1], wt.shape[0]
    full = lambda a: pl.BlockSpec(a.shape, lambda b, s: (0,) * a.ndim)
    ins = [x2d, g, wn, bn, wt]
    in_specs = [pl.BlockSpec((tm, D), lambda b, s: (b * ns + s, 0)), full(g), full(wn), full(bn), full(wt)]
    out_shape = [jax.ShapeDtypeStruct((M, n_nat), BF), jax.ShapeDtypeStruct((B, n_t, S), BF)]
    out_specs = [pl.BlockSpec((tm, n_nat), lambda b, s: (b * ns + s, 0)),
                 pl.BlockSpec((1, n_t, tm), lambda b, s: (b, 0, s))]
    scratch = []
    if wf32 is not None:
        ins.append(wf32)
        in_specs.append(full(wf32))
        out_shape.append(jax.ShapeDtypeStruct((M, wf32.shape[1]), F32))
        out_specs.append(pl.BlockSpec((tm, wf32.shape[1]), lambda b, s: (b * ns + s, 0)))
    if wg is not None:
        ins.append(wg)
        in_specs.append(full(wg))
        out_shape.append(jax.ShapeDtypeStruct((B, wg.shape[0], S), F32))
        out_specs.append(pl.BlockSpec((1, wg.shape[0], tm), lambda b, s: (b, 0, s)))
    if fox_extra is not None:
        for a in fox_extra:
            ins.append(a)
            in_specs.append(full(a))
        scratch.append(pltpu.VMEM((1, LANES), F32))
    body = functools.partial(_inproj_body, n_f32=wf32 is not None, n_gate=wg is not None,
                             fox=fox_extra is not None, tm=tm, col_chunk=1024 if n_nat % 1024 == 0 else 512)
    return pl.pallas_call(
        body, grid=(B, ns), in_specs=in_specs, out_specs=out_specs, out_shape=out_shape,
        scratch_shapes=scratch, compiler_params=_cparams("arbitrary", "arbitrary"), name="inproj")(*ins)


def _attn_body(*refs, nq_stack, vdim, mode, use_sel, final, lam_init, q_mask):
    it = iter(refs)
    q_ref, k_ref, vt_ref, bias_ref = (next(it) for _ in range(4))
    selb_ref = next(it) if use_sel else None
    if final == "da":
        lam_ref, subg_ref = next(it), next(it)
    o_ref = next(it)
    qm_ref, m_ref, l_ref, acc_ref = (next(it) for _ in range(4))
    tq, tk = ATT_TQ, ATT_TK
    lw = nq_stack * tq
    i = pl.program_id(2)

    qt = q_ref[...]
    lane = lax.broadcasted_iota(jnp.int32, (tq, LANES), 1)
    for r in range(nq_stack):
        blk = qt[:, (r * LANES if q_mask != "da" else 0):(r * LANES if q_mask != "da" else 0) + LANES]
        if q_mask == "da":
            keep = (lane < HEAD_DIM) if r == 0 else (lane >= HEAD_DIM)
        elif q_mask == "group_parity":
            keep = (lane // HEAD_DIM) == (pl.program_id(1) % 2)
        else:
            keep = None
        if keep is not None:
            blk = jnp.where(keep, blk, jnp.zeros_like(blk))
        qm_ref[r * tq:(r + 1) * tq, :] = blk
    m_ref[...] = jnp.full_like(m_ref, MASK_VALUE)
    l_ref[...] = jnp.zeros_like(l_ref)
    acc_ref[...] = jnp.zeros_like(acc_ref)

    def step(kt, near):
        k0 = pl.multiple_of(kt * tk, tk)
        s = _dot_nt(k_ref[pl.ds(k0, tk), :], qm_ref[...])
        if near is not None:
            s = s + bias_ref[0, near]
        if use_sel:
            sb8 = selb_ref[0, 0, pl.ds(pl.multiple_of((kt // 2) * 8, 8), 8), :]
            sb4 = jnp.where(kt % 2 == 0, sb8[0:4, :], sb8[4:8, :])
            sbt = jnp.concatenate([sb4] * nq_stack, axis=1)
            s = jnp.concatenate(
                [s[j * NSA_SEL_LEN:(j + 1) * NSA_SEL_LEN, :] + sbt[j:j + 1, :] for j in range(tk // NSA_SEL_LEN)],
                axis=0)
        m_old = m_ref[...]
        m_new = jnp.maximum(m_old, jnp.max(s, axis=0, keepdims=True))
        alpha = jnp.exp(m_old - m_new)
        p = jnp.exp(s - m_new)
        l_ref[...] = alpha * l_ref[...] + jnp.sum(p, axis=0, keepdims=True)
        acc_ref[...] = alpha * acc_ref[...] + _dot(vt_ref[0, :, pl.ds(k0, tk)], p.astype(BF))
        m_ref[...] = m_new

    if mode == "causal":
        def far(kt, c):
            step(kt, None)
            return c
        lax.fori_loop(0, jnp.maximum(i - 1, 0), far, 0)
        near_tiles = [(i - 1, 0), (i, 1)]
    else:
        near_tiles = [(i - 2, 0), (i - 1, 1), (i, 2)]
    for kt, idx in near_tiles:
        @pl.when(kt >= 0)
        def _(kt=kt, idx=idx):
            step(kt, idx)

    acc = acc_ref[...]
    l = l_ref[...]
    if final == "da":
        lamv = lam_ref[...]
        lam = (jnp.exp(jnp.sum(lamv[0:1] * lamv[1:2], axis=1, keepdims=True))
               - jnp.exp(jnp.sum(lamv[2:3] * lamv[3:4], axis=1, keepdims=True)) + lam_init)
        o = acc[:, :tq] / l[:, :tq] - lam * (acc[:, tq:] / l[:, tq:])
        o = o * lax.rsqrt(jnp.mean(o * o, axis=0, keepdims=True) + NORM_EPS) * subg_ref[...] * (1.0 - lam_init)
        o_ref[0] = o.astype(o_ref.dtype)
    else:
        for r in range(nq_stack):
            o_ref[0, r * vdim:(r + 1) * vdim, :] = (acc[:, r * tq:(r + 1) * tq] / l[:, r * tq:(r + 1) * tq]).astype(o_ref.dtype)


def _attention(q_arr, k_arr, vt, bias, *, B, S, n_prog, q_spec, k_blk, v_blk, bias_blk, nq_stack, vdim, mode,
               out_rows, out_dtype, q_mask, selb=None, final="plain", lam=None, subg=None, lam_init=0.0):
    tq, tk = ATT_TQ, ATT_TK
    nq = S // tq
    lw = nq_stack * tq
    q_w, q_blk = q_spec
    ins = [q_arr, k_arr, vt, bias]
    in_specs = [
        pl.BlockSpec((tq, q_w), lambda b, h, i: (b * nq + i, q_blk(h))),
        pl.BlockSpec((S, LANES), lambda b, h, i: (b, k_blk(h))),
        pl.BlockSpec((1, vdim, S), lambda b, h, i: (b, v_blk(h), 0)),
        pl.BlockSpec((1,) + bias.shape[1:], lambda b, h, i: (bias_blk(h), 0, 0, 0)),
    ]
    if selb is not None:
        ins.append(selb)
        in_specs.append(pl.BlockSpec((1, 1, selb.shape[2], tq), lambda b, h, i: (b, h, 0, i)))
    if final == "da":
        ins += [lam, subg]
        in_specs += [pl.BlockSpec(lam.shape, lambda b, h, i: (0, 0)), pl.BlockSpec(subg.shape, lambda b, h, i: (0, 0))]
    body = functools.partial(_attn_body, nq_stack=nq_stack, vdim=vdim, mode=mode, use_sel=selb is not None,
                             final=final, lam_init=lam_init, q_mask=q_mask)
    return pl.pallas_call(
        body, grid=(B, n_prog, nq), in_specs=in_specs,
        out_specs=pl.BlockSpec((1, out_rows, tq), lambda b, h, i: (b, h, i)),
        out_shape=jax.ShapeDtypeStruct((B, out_rows * n_prog, S), out_dtype),
        scratch_shapes=[pltpu.VMEM((lw, LANES), BF), pltpu.VMEM((1, lw), F32), pltpu.VMEM((1, lw), F32),
                        pltpu.VMEM((vdim, lw), F32)],
        compiler_params=_cparams("arbitrary", "arbitrary", "arbitrary"), name="attn_" + mode + "_" + final)(*ins)


def _compress_body(x_ref, pe_ref, w1_ref, w2_ref, o_ref):
    xb = (x_ref[0] + pe_ref[0]).astype(BF)
    hcur = _dot(xb, w1_ref[0])
    hcur = 0.5 * hcur * (1.0 + jnp.tanh(math.sqrt(2.0 / math.pi) * (hcur + 0.044715 * (hcur * hcur * hcur))))
    o_ref[0] = _dot(hcur.astype(BF), w2_ref[0])


def _compress(xblk, pe, w1, w2, *, tr=512):
    _, R, K = xblk.shape
    return pl.pallas_call(
        _compress_body, grid=(2, R // tr),
        in_specs=[pl.BlockSpec((1, tr, K), lambda a, r: (a, r, 0)),
                  pl.BlockSpec((1, 1, K), lambda a, r: (a, 0, 0)),
                  pl.BlockSpec((1,) + w1.shape[1:], lambda a, r: (a, 0, 0)),
                  pl.BlockSpec((1,) + w2.shape[1:], lambda a, r: (a, 0, 0))],
        out_specs=pl.BlockSpec((1, tr, HEAD_DIM), lambda a, r: (a, r, 0)),
        out_shape=jax.ShapeDtypeStruct((2, R, HEAD_DIM), F32),
        compiler_params=_cparams("arbitrary", "arbitrary"), name="nsa_compress")(xblk, pe, w1, w2)


def _cmp_body(q_ref, kc_ref, vct_ref, bc_ref, ov_ref, o_ref, selb_ref, s_ref):
    tq = ATT_TQ
    lw = NSA_REP * tq
    i = pl.program_id(2)
    n_cmp = kc_ref.shape[2]
    n_sel = selb_ref.shape[2]
    band = bc_ref.shape[1]
    qt = q_ref[...]
    lane = lax.broadcasted_iota(jnp.int32, (tq, LANES), 1)
    keep = (lane // HEAD_DIM) == (pl.program_id(1) % 2)
    qm = jnp.concatenate(
        [jnp.where(keep, qt[:, r * LANES:(r + 1) * LANES], jnp.zeros((tq, LANES), BF)) for r in range(NSA_REP)], axis=0)
    s = _dot_nt(kc_ref[0, 0], qm)
    row = lax.broadcasted_iota(jnp.int32, s.shape, 0)
    s_ref[...] = jnp.where(row < (band // 2) * (i + 1), s, MASK_VALUE)

    @pl.when(i == 0)
    def _():
        s_ref[0:band // 2, :] = s_ref[0:band // 2, :] + bc_ref[0, band // 2:, :]

    @pl.when(i > 0)
    def _():
        r0 = pl.multiple_of((band // 2) * (i - 1), band // 2)
        s_ref[pl.ds(r0, band), :] = s_ref[pl.ds(r0, band), :] + bc_ref[0]

    s = s_ref[...]
    m = jnp.max(s, axis=0, keepdims=True)
    e = jnp.exp(s - m)
    l = jnp.sum(e, axis=0, keepdims=True)
    inv = jnp.where(m > 0.5 * MASK_VALUE, 1.0 / l, 0.0)
    p = e * inv
    oc = _dot(vct_ref[0, 0], p.astype(BF))
    for r in range(NSA_REP):
        o_ref[0, r * HEAD_DIM:(r + 1) * HEAD_DIM, :] = oc[:, r * tq:(r + 1) * tq]
    ps = p[:, 0:tq]
    for r in range(1, NSA_REP):
        ps = ps + p[:, r * tq:(r + 1) * tq]
    p_hi = ps.astype(BF)
    r1 = ps - p_hi.astype(F32)
    p_mid = r1.astype(BF)
    p_lo = (r1 - p_mid.astype(F32)).astype(BF)
    ov = ov_ref[...]
    imp = _dot(ov, p_hi) + _dot(ov, p_mid) + _dot(ov, p_lo)
    j = lax.broadcasted_iota(jnp.int32, (n_sel, tq), 0)
    t = i * tq + lax.broadcasted_iota(jnp.int32, (n_sel, tq), 1)
    cur = t // NSA_SEL_LEN
    forced = (j == 0) | (j == cur) | (j == cur - 1)
    w = jnp.where(j <= cur, jnp.where(forced, NSA_FORCE_SCORE, imp), -1.0)
    jf = j.astype(F32)
    sel = jnp.zeros((n_sel, tq), F32)
    for _ in range(min(NSA_TOP_N, n_sel)):
        mx = jnp.max(w, axis=0, keepdims=True)
        idx = jnp.min(jnp.where(w == mx, jf, float(n_sel)), axis=0, keepdims=True)
        pick = jf == idx
        sel = jnp.where(pick, 1.0, sel)
        w = jnp.where(pick, -2.0, w)
    selb_ref[0, 0] = jnp.where(sel > 0.5, 0.0, MASK_VALUE)


def _cmp_attention(qn, kc, vct, bc, ov, *, B, S, q_coloff):
    tq = ATT_TQ
    nq = S // tq
    n_cmp = kc.shape[2]
    n_sel = S // NSA_SEL_LEN
    return pl.pallas_call(
        _cmp_body, grid=(B, NSA_GROUPS, nq),
        in_specs=[pl.BlockSpec((tq, NSA_REP * LANES), lambda b, g, i: (b * nq + i, q_coloff + g // 2)),
                  pl.BlockSpec((1, 1, n_cmp, LANES), lambda b, g, i: (b, g, 0, 0)),
                  pl.BlockSpec((1, 1, HEAD_DIM, n_cmp), lambda b, g, i: (b, g, 0, 0)),
                  pl.BlockSpec((1,) + bc.shape[1:], lambda b, g, i: (g, 0, 0)),
                  pl.BlockSpec(ov.shape, lambda b, g, i: (0, 0))],
        out_specs=[pl.BlockSpec((1, NSA_REP * HEAD_DIM, tq), lambda b, g, i: (b, g, i)),
                   pl.BlockSpec((1, 1, n_sel, tq), lambda b, g, i: (b, g, 0, i))],
        out_shape=[jax.ShapeDtypeStruct((B, D_MODEL, S), F32), jax.ShapeDtypeStruct((B, NSA_GROUPS, n_sel, S), F32)],
        scratch_shapes=[pltpu.VMEM((n_cmp, NSA_REP * tq), F32)],
        compiler_params=_cparams("arbitrary", "arbitrary", "arbitrary"), name="nsa_cmp_topk")(qn, kc, vct, bc, ov)


def _outproj_body(*refs, n_o):
    it = iter(refs)
    o_refs = [next(it) for _ in range(n_o)]
    gt_ref = next(it) if n_o > 1 else None
    w_ref, x_ref, g_ref, out_ref = (next(it) for _ in range(4))
    if n_o == 1:
        ot = o_refs[0][0]
    else:
        gts = gt_ref[0]
        parts = []
        for hd in range(N_HEADS):
            rows = slice(hd * HEAD_DIM, (hd + 1) * HEAD_DIM)
            acc = gts[hd:hd + 1, :] * o_refs[0][0, rows, :]
            for b in range(1, n_o):
                acc = acc + gts[b * N_HEADS + hd:b * N_HEADS + hd + 1, :] * o_refs[b][0, rows, :]
            parts.append(acc.astype(BF))
        ot = jnp.concatenate(parts, axis=0)
    y = _dot_tn(ot, w_ref[...])
    out_ref[...] = x_ref[...] + _rms(y, g_ref[...])


def _outproj(o_list, gt, w, x2d, g, *, B, S, tm=512):
    M, D = x2d.shape
    ns = S // tm
    ins = list(o_list)
    in_specs = [pl.BlockSpec((1, D, tm), lambda b, s: (b, 0, s)) for _ in o_list]
    if gt is not None:
        ins.append(gt)
        in_specs.append(pl.BlockSpec((1, gt.shape[1], tm), lambda b, s: (b, 0, s)))
    ins += [w, x2d, g]
    in_specs += [pl.BlockSpec(w.shape, lambda b, s: (0, 0)),
                 pl.BlockSpec((tm, D), lambda b, s: (b * ns + s, 0)),
                 pl.BlockSpec(g.shape, lambda b, s: (0, 0))]
    return pl.pallas_call(
        functools.partial(_outproj_body, n_o=len(o_list)), grid=(B, ns), in_specs=in_specs,
        out_specs=pl.BlockSpec((tm, D), lambda b, s: (b * ns + s, 0)),
        out_shape=jax.ShapeDtypeStruct((M, D), F32),
        compiler_params=_cparams("arbitrary", "arbitrary"), name="outproj")(*ins)


def _mlp_body(x_ref, g2_ref, w1_ref, w2_ref, g3_ref, wg_ref, p_ref, wp_ref, out_ref, h_ref, acc_ref):
    f = pl.program_id(1)

    @pl.when(f == 0)
    def _():
        h_ref[...] = _rms(x_ref[...], g2_ref[...]).astype(BF)
        acc_ref[...] = jnp.zeros_like(acc_ref)

    a = jnp.maximum(_dot(h_ref[...], w1_ref[...]), 0.0)
    acc_ref[...] += _dot((a * a).astype(BF), w2_ref[...])

    @pl.when(f == pl.num_programs(1) - 1)
    def _():
        x2 = x_ref[...] + _rms(acc_ref[...], g3_ref[...])
        gate = jax.nn.sigmoid(_dot(x2.astype(BF), wg_ref[...]))
        out_ref[...] = x2 + gate * _dot(p_ref[...].astype(BF), wp_ref[...])


def _mlp_ple(x2d, g2, w1, w2, g3, wg, p2d, wp, *, tm=512, tf=1024):
    M, D = x2d.shape
    FF = w1.shape[1]
    PD = p2d.shape[1]
    const = lambda a: pl.BlockSpec(a.shape, lambda m, f: (0, 0))
    return pl.pallas_call(
        _mlp_body, grid=(M // tm, FF // tf),
        in_specs=[pl.BlockSpec((tm, D), lambda m, f: (m, 0)), const(g2),
                  pl.BlockSpec((D, tf), lambda m, f: (0, f)), pl.BlockSpec((tf, D), lambda m, f: (f, 0)),
                  const(g3), const(wg), pl.BlockSpec((tm, PD), lambda m, f: (m, 0)), const(wp)],
        out_specs=pl.BlockSpec((tm, D), lambda m, f: (m, 0)),
        out_shape=jax.ShapeDtypeStruct((M, D), F32),
        scratch_shapes=[pltpu.VMEM((tm, D), BF), pltpu.VMEM((tm, D), F32)],
        compiler_params=_cparams("arbitrary", "arbitrary"), name="mlp_ple")(x2d, g2, w1, w2, g3, wg, p2d, wp)


def _t5_bucket(dist):
    n = jnp.maximum(dist, 0)
    max_exact = REL_BUCKETS // 2
    nf = jnp.maximum(n, 1).astype(F32)
    large = max_exact + (jnp.log(nf / max_exact) / math.log(REL_MAX_DIST / max_exact)
                         * (REL_BUCKETS - max_exact)).astype(jnp.int32)
    large = jnp.minimum(large, REL_BUCKETS - 1)
    return jnp.where(n < max_exact, n, large)


def _bias_by_distance(rel_bias):
    tb = rel_bias[_t5_bucket(jnp.arange(REL_TABLE_LEN))].astype(F32)
    return (tb - rel_bias[REL_BUCKETS - 1].astype(F32)[None, :]).T


def _toeplitz(fn, rows, cols, off, row_stride=1):
    n = row_stride * rows + cols
    d = jnp.arange(n)
    d = jnp.where(d < cols, d, d - n)
    v = fn(d + off)
    flat = jnp.tile(v, (1,) * (v.ndim - 1) + (rows,))[..., :rows * (n - row_stride)]
    return flat.reshape(v.shape[:-1] + (rows, n - row_stride))[..., :cols]


def _bias_fn(tbs, lo_mask=True, hi_limit=None):
    def fn(d):
        val = tbs[:, jnp.clip(d, 0, REL_TABLE_LEN - 1)]
        bad = (d < 0) if lo_mask else jnp.zeros(d.shape, bool)
        if hi_limit is not None:
            bad = bad | (d >= hi_limit)
        return jnp.where(bad[None, :], MASK_VALUE, val)
    return fn


def _stack_heads(t, n_grp, n_stack):
    h, nt, r, c = t.shape
    return t.reshape(n_grp, n_stack, nt, r, c).transpose(0, 2, 3, 1, 4).reshape(n_grp, nt, r, n_stack * c)


def _attn_tables(tbs):
    tq, tk = ATT_TQ, ATT_TK
    causal = jnp.stack([_toeplitz(_bias_fn(tbs), tk, tq, tk), _toeplitz(_bias_fn(tbs), tk, tq, 0)], axis=1)
    window = jnp.stack([_toeplitz(_bias_fn(tbs, hi_limit=NSA_WINDOW), tk, tq, 2 * tk),
                        _toeplitz(_bias_fn(tbs), tk, tq, tk), _toeplitz(_bias_fn(tbs), tk, tq, 0)], axis=1)
    band = 2 * (tq // NSA_CMP_STRIDE)
    cmp_band = _toeplitz(_bias_fn(tbs), band, tq, tq - (NSA_CMP_LEN - 1), row_stride=NSA_CMP_STRIDE)
    return causal, window, cmp_band


def _nsa_q_perm():
    cols = []
    for p in range(NSA_GROUPS // 2):
        for r in range(NSA_REP):
            for g in (2 * p, 2 * p + 1):
                h = g * NSA_REP + r
                cols.extend(range(h * HEAD_DIM, (h + 1) * HEAD_DIM))
    return jnp.asarray(cols, jnp.int32)


def _row(v):
    return v.reshape(1, -1).astype(F32)


def kernel(x, p, rel_bias, norm_g, mlp_w1, mlp_w2, ple_w, ple_gate_w, da_w_in, da_lambda, da_subln, da_w_out,
           nsa_w_in, nsa_cmp_pe, nsa_cmp_w1, nsa_cmp_w2, nsa_w_out, fox_w_in, fox_b_f, fox_w_out):
    B, S, D = x.shape
    depth = p.shape[0]
    scale = HEAD_DIM ** -0.5
    tq, tk = ATT_TQ, ATT_TK
    tbs = _bias_by_distance(rel_bias)
    causal_t, window_t, cmp_band_t = _attn_tables(tbs)
    da_bias = _stack_heads(causal_t, DA_HEADS, 2)
    x2d = x.reshape(B * S, D)
    ia = ib = ic = 0
    for i in range(depth):
        g = norm_g[i]
        kind = i % N_MIXERS
        if kind == 0:
            lam_init = 0.8 - 0.6 * math.exp(-0.3 * i)
            w_in = da_w_in[ia]
            wn = jnp.concatenate([w_in[:, :D] * scale, w_in[:, D:2 * D]], axis=1).astype(BF)
            wt = w_in[:, 2 * D:].T.astype(BF)
            qk, vt = _inproj(x2d, _row(g[0]), wn, jnp.zeros((1, 2 * D), F32), wt, B=B, S=S)
            ot = _attention(qk, qk, vt, da_bias, B=B, S=S, n_prog=DA_HEADS, q_spec=(LANES, lambda h: h),
                            k_blk=lambda h: DA_HEADS + h, v_blk=lambda h: h, bias_blk=lambda h: h, nq_stack=2,
                            vdim=2 * HEAD_DIM, mode="causal", out_rows=2 * HEAD_DIM, out_dtype=BF, q_mask="da",
                            final="da", lam=da_lambda[ia].astype(F32), subg=da_subln[ia].reshape(-1, 1).astype(F32),
                            lam_init=lam_init)
            x2d = _outproj([ot], None, da_w_out[ia].astype(BF), x2d, _row(g[1]), B=B, S=S)
            ia += 1
        elif kind == 1:
            w_in = nsa_w_in[ib]
            kvd = NSA_GROUPS * HEAD_DIM
            wq = w_in[:, :D][:, _nsa_q_perm()] * scale
            kv = [w_in[:, D + a * kvd:D + (a + 1) * kvd] for a in range(6)]
            wn = jnp.concatenate([wq, kv[2], kv[4]], axis=1).astype(BF)
            wt = jnp.concatenate([kv[3], kv[5]], axis=1).T.astype(BF)
            wf32 = jnp.concatenate([kv[0], kv[1]], axis=1).astype(BF)
            wgate = w_in[:, D + 6 * kvd:].reshape(D, N_HEADS, 3).transpose(2, 1, 0).reshape(3 * N_HEADS, D).astype(BF)
            qn, vt, cmp_in, gates_t = _inproj(x2d, _row(g[0]), wn, jnp.zeros((1, wn.shape[1]), F32), wt, B=B, S=S,
                                              wf32=wf32, wg=wgate)
            n_chunk = S // NSA_CMP_STRIDE
            cm = cmp_in.reshape(B, n_chunk, NSA_CMP_STRIDE, 2, NSA_GROUPS, HEAD_DIM).transpose(3, 0, 1, 4, 2, 5)
            cm = cm.reshape(2, B, n_chunk, NSA_GROUPS, NSA_CMP_STRIDE * HEAD_DIM)
            nxt = jnp.concatenate([cm[:, :, 1:], jnp.zeros_like(cm[:, :, :1])], axis=2)
            xblk = jnp.concatenate([cm, nxt], axis=-1).reshape(2, B * n_chunk * NSA_GROUPS, NSA_CMP_LEN * HEAD_DIM)
            pe = nsa_cmp_pe[ib].reshape(2, 1, NSA_CMP_LEN * HEAD_DIM).astype(F32)
            kvc = _compress(xblk, pe, nsa_cmp_w1[ib].astype(BF), nsa_cmp_w2[ib].astype(BF))
            kvc = kvc.reshape(2, B, n_chunk, NSA_GROUPS, HEAD_DIM).transpose(0, 1, 3, 2, 4)
            kc = jnp.concatenate([kvc[0], kvc[0]], axis=-1).astype(BF)
            vct = kvc[1].transpose(0, 1, 3, 2).astype(BF)
            jj = jnp.arange(S // NSA_SEL_LEN)[:, None]
            nn = jnp.arange(n_chunk)[None, :]
            n_cmp_blocks = (S - NSA_CMP_LEN) // NSA_CMP_STRIDE + 1
            ov = ((nn * NSA_CMP_STRIDE < (jj + 1) * NSA_SEL_LEN) & (nn * NSA_CMP_STRIDE + NSA_CMP_LEN - 1 >= jj * NSA_SEL_LEN)
                  & (nn < n_cmp_blocks)).astype(BF)
            oc_t, selb = _cmp_attention(qn, kc, vct, _stack_heads(cmp_band_t[:, None], NSA_GROUPS, NSA_REP)[:, 0],
                                        ov, B=B, S=S, q_coloff=0)
            nsa_kw = dict(B=B, S=S, n_prog=NSA_GROUPS, q_spec=(NSA_REP * LANES, lambda h: h // 2),
                          bias_blk=lambda h: h, nq_stack=NSA_REP, vdim=HEAD_DIM, out_rows=NSA_REP * HEAD_DIM,
                          out_dtype=F32, q_mask="group_parity")
            kblk0 = D // LANES
            os_t = _attention(qn, qn, vt, _stack_heads(causal_t, NSA_GROUPS, NSA_REP), mode="causal",
                              k_blk=lambda h: kblk0 + h // 2, v_blk=lambda h: h, selb=selb, **nsa_kw)
            ow_t = _attention(qn, qn, vt, _stack_heads(window_t, NSA_GROUPS, NSA_REP), mode="window",
                              k_blk=lambda h: kblk0 + 2 + h // 2, v_blk=lambda h: NSA_GROUPS + h, **nsa_kw)
            x2d = _outproj([oc_t, os_t, ow_t], gates_t, nsa_w_out[ib].astype(BF), x2d, _row(g[1]), B=B, S=S)
            ib += 1
        else:
            w_in = fox_w_in[ic]
            z64 = jnp.zeros((D, N_HEADS, HEAD_DIM), F32)
            wq = jnp.concatenate([(w_in[:, :D] * scale).reshape(D, N_HEADS, HEAD_DIM), z64], axis=2).reshape(D, 2 * D)
            wk = jnp.concatenate([w_in[:, D:2 * D].reshape(D, N_HEADS, HEAD_DIM), z64], axis=2).reshape(D, 2 * D)
            wn = jnp.concatenate([wq, wk], axis=1).astype(BF)
            aug = jnp.zeros((N_HEADS, LANES), F32).at[:, HEAD_DIM:HEAD_DIM + 3].set(1.0).reshape(1, 2 * D)
            bn = jnp.concatenate([aug, jnp.zeros((1, 2 * D), F32)], axis=1)
            wf = w_in[:, 3 * D:]
            wf3 = jnp.concatenate([wf, wf, wf, jnp.zeros((D, LANES - 3 * N_HEADS), F32)], axis=1).astype(BF)
            bf3 = jnp.concatenate([fox_b_f[ic]] * 3 + [jnp.zeros((LANES - 3 * N_HEADS,), F32)]).reshape(1, LANES).astype(F32)
            hh = jnp.arange(N_HEADS)
            pm = jnp.zeros((LANES, 2 * D), F32)
            for a in range(3):
                pm = pm.at[a * N_HEADS + hh, hh * LANES + HEAD_DIM + a].set(-1.0)
            qk, vt = _inproj(x2d, _row(g[0]), wn, bn, w_in[:, 2 * D:3 * D].T.astype(BF), B=B, S=S,
                             fox_extra=[wf3, bf3, pm.astype(BF)])
            mask_t = _toeplitz(lambda d: jnp.where(d < 0, MASK_VALUE, 0.0).astype(F32)[None, :], tk, tq, 0)[:, None]
            mask_t = jnp.concatenate([jnp.zeros_like(mask_t), mask_t], axis=1)
            ot = _attention(qk, qk, vt, mask_t, B=B, S=S, n_prog=N_HEADS, q_spec=(LANES, lambda h: h),
                            k_blk=lambda h: N_HEADS + h, v_blk=lambda h: h, bias_blk=lambda h: 0, nq_stack=1,
                            vdim=HEAD_DIM, mode="causal", out_rows=HEAD_DIM, out_dtype=BF, q_mask="none")
            x2d = _outproj([ot], None, fox_w_out[ic].astype(BF), x2d, _row(g[1]), B=B, S=S)
            ic += 1
        x2d = _mlp_ple(x2d, _row(g[2]), mlp_w1[i].astype(BF), mlp_w2[i].astype(BF), _row(g[3]),
                       ple_gate_w[i].astype(BF), p[i].reshape(B * S, -1), ple_w[i].astype(BF))
    return x2d.reshape(B, S, D)
```

```python
import functools
import math

import jax
import jax.numpy as jnp
from jax import lax
from jax.experimental import pallas as pl
from jax.experimental.pallas import tpu as pltpu

BF = jnp.bfloat16
F32 = jnp.float32

D_MODEL = 1024
HEAD_DIM = 64
LANES = 128
NORM_EPS = 1e-6
MASK_VALUE = -1e30
REL_BUCKETS = 32
REL_MAX_DIST = 128
REL_TABLE_LEN = 512
N_HEADS = D_MODEL // HEAD_DIM
DA_HEADS = N_HEADS // 2
NSA_GROUPS = 4
NSA_REP = N_HEADS // NSA_GROUPS
NSA_CMP_LEN = 32
NSA_CMP_STRIDE = 16
NSA_SEL_LEN = 64
NSA_TOP_N = 16
NSA_WINDOW = 512
NSA_FORCE_SCORE = 1e4
N_MIXERS = 3

ATT_TQ = 256
DA_TQ = 512
FOX_TQ = 1024
ATT_TK = 256
VMEM_LIMIT = 56 * 1024 * 1024
ONES_ROWS = 16
LOG2E = math.log2(math.e)


def _cparams(*sem):
    return pltpu.CompilerParams(dimension_semantics=sem, vmem_limit_bytes=VMEM_LIMIT)


def _rms(x, g):
    return x * lax.rsqrt(jnp.mean(x * x, axis=-1, keepdims=True) + NORM_EPS) * g


def _dot(a, b):
    return jnp.dot(a, b, preferred_element_type=F32)


def _dot_nt(a, b):
    return lax.dot_general(a, b, (((1,), (1,)), ((), ())), preferred_element_type=F32)


def _dot_tn(a, b):
    return lax.dot_general(a, b, (((0,), (0,)), ((), ())), preferred_element_type=F32)


def _inproj_body(*refs, n_f32, n_gate, fox, tm, col_chunk):
    it = iter(refs)
    x_ref, g_ref, wn_ref, bn_ref, wt_ref = (next(it) for _ in range(5))
    wf32_ref = next(it) if n_f32 else None
    wg_ref = next(it) if n_gate else None
    if fox:
        wf_ref, bf_ref, pm_ref = next(it), next(it), next(it)
    on_ref, ot_ref = next(it), next(it)
    of32_ref = next(it) if n_f32 else None
    og_ref = next(it) if n_gate else None
    carry_ref = next(it) if fox else None

    h = _rms(x_ref[...], g_ref[...]).astype(BF)
    n_nat = on_ref.shape[1]
    k_off = n_nat // 2
    if fox:
        @pl.when(pl.program_id(1) == 0)
        def _():
            carry_ref[...] = jnp.zeros_like(carry_ref)

        lf = _dot(h, wf_ref[...]) + bf_ref[...]
        lane = lax.broadcasted_iota(jnp.int32, lf.shape, 1)
        row = lax.broadcasted_iota(jnp.int32, lf.shape, 0)
        ls = jnp.minimum(lf, 0.0) - jnp.log1p(jnp.exp(-jnp.abs(lf)))
        c = jnp.where(lane < 3 * N_HEADS, ls, 0.0)
        k = 1
        while k < tm:
            c = c + jnp.where(row >= k, pltpu.roll(c, k, axis=0), 0.0)
            k *= 2
        c = c + carry_ref[...]
        carry_ref[...] = c[tm - 1:tm, :]
        c = c * LOG2E
        hi = c.astype(BF).astype(F32)
        r1 = c - hi
        mid = r1.astype(BF).astype(F32)
        lo = (r1 - mid).astype(BF).astype(F32)
        c3 = jnp.where(lane < N_HEADS, hi, jnp.where(lane < 2 * N_HEADS, mid, lo)).astype(BF)
    for c0 in range(0, n_nat, col_chunk):
        y = _dot(h, wn_ref[:, c0:c0 + col_chunk]) + bn_ref[:, c0:c0 + col_chunk]
        if fox and c0 >= k_off:
            y = y + _dot(c3, pm_ref[:, c0 - k_off:c0 - k_off + col_chunk])
        on_ref[:, c0:c0 + col_chunk] = y.astype(on_ref.dtype)
    ot_ref[0] = _dot_nt(wt_ref[...], h).astype(ot_ref.dtype)
    if n_f32:
        of32_ref[...] = _dot(h, wf32_ref[...])
    if n_gate:
        og_ref[0] = jax.nn.sigmoid(_dot_nt(wg_ref[...], h))


def _inproj(x2d, g, wn, bn, wt, *, B, S, tm=512, wf32=None, wg=None, fox_extra=None):
    M, D = x2d.shape
    ns = S // tm
    n_nat, n_t = wn.shape[1], wt.shape[0]
    full = lambda a: pl.BlockSpec(a.shape, lambda b, s: (0,) * a.ndim)
    ins = [x2d, g, wn, bn, wt]
    in_specs = [pl.BlockSpec((tm, D), lambda b, s: (b * ns + s, 0)), full(g), full(wn), full(bn), full(wt)]
    out_shape = [jax.ShapeDtypeStruct((M, n_nat), BF), jax.ShapeDtypeStruct((B, n_t, S), BF)]
    out_specs = [pl.BlockSpec((tm, n_nat), lambda b, s: (b * ns + s, 0)),
                 pl.BlockSpec((1, n_t, tm), lambda b, s: (b, 0, s))]
    scratch = []
    if wf32 is not None:
        ins.append(wf32)
        in_specs.append(full(wf32))
        out_shape.append(jax.ShapeDtypeStruct((M, wf32.shape[1]), F32))
        out_specs.append(pl.BlockSpec((tm, wf32.shape[1]), lambda b, s: (b * ns + s, 0)))
    if wg is not None:
        ins.append(wg)
        in_specs.append(full(wg))
        out_shape.append(jax.ShapeDtypeStruct((B, wg.shape[0], S), F32))
        out_specs.append(pl.BlockSpec((1, wg.shape[0], tm), lambda b, s: (b, 0, s)))
    if fox_extra is not None:
        for a in fox_extra:
            ins.append(a)
            in_specs.append(full(a))
        scratch.append(pltpu.VMEM((1, LANES), F32))
    body = functools.partial(_inproj_body, n_f32=wf32 is not None, n_gate=wg is not None,
                             fox=fox_extra is not None, tm=tm, col_chunk=1024 if n_nat % 1024 == 0 else 512)
    return pl.pallas_call(
        body, grid=(B, ns), in_specs=in_specs, out_specs=out_specs, out_shape=out_shape,
        scratch_shapes=scratch, compiler_params=_cparams("arbitrary", "arbitrary"), name="inproj")(*ins)


def _attn_body(*refs, tq, nq_stack, vdim, mode, use_sel, final, lam_init, q_mask, near):
    it = iter(refs)
    q_ref, k_ref, vt_ref, bias_ref = (next(it) for _ in range(4))
    selb_ref = next(it) if use_sel else None
    if final == "da":
        lam_ref, subg_ref = next(it), next(it)
    o_ref = next(it)
    qm_ref, m_ref, acc_ref, s_scr, p_scr, alpha_scr = (next(it) for _ in range(6))
    tk = ATT_TK
    lw = nq_stack * tq
    i = pl.program_id(2)

    qt = q_ref[...]
    lane = lax.broadcasted_iota(jnp.int32, (tq, LANES), 1)
    for r in range(nq_stack):
        blk = qt[:, 0:LANES] if q_mask == "da" else qt[:, r * LANES:(r + 1) * LANES]
        if q_mask == "da":
            keep = (lane < HEAD_DIM) if r == 0 else (lane >= HEAD_DIM)
        elif q_mask == "group_parity":
            keep = (lane // HEAD_DIM) == (pl.program_id(1) % 2)
        else:
            keep = None
        if keep is not None:
            blk = jnp.where(keep, blk, jnp.zeros_like(blk))
        qm_ref[r * tq:(r + 1) * tq, :] = blk
    m_ref[...] = jnp.full_like(m_ref, MASK_VALUE)
    acc_ref[...] = jnp.zeros_like(acc_ref)
    ones_rows = jnp.ones((ONES_ROWS, tk), BF)

    def stage_qk(kt, slot):
        k0 = pl.multiple_of(kt * tk, tk)
        s_scr[slot] = _dot_nt(k_ref[pl.ds(k0, tk), :], qm_ref[...])

    def stage_softmax(kt, slot, bias_idx=None, full_mask=None, wipe=None):
        s = s_scr[slot]
        if bias_idx is not None:
            s = s + bias_ref[0, bias_idx]
        if full_mask is not None:
            s = s + jnp.where(full_mask, MASK_VALUE, 0.0)
        if use_sel:
            sb8 = selb_ref[0, 0, pl.ds(pl.multiple_of((kt // 2) * 8, 8), 8), :]
            sb4 = jnp.where(kt % 2 == 0, sb8[0:4, :], sb8[4:8, :])
            sbt = jnp.concatenate([sb4] * nq_stack, axis=1)
            s = jnp.concatenate(
                [s[j * NSA_SEL_LEN:(j + 1) * NSA_SEL_LEN, :] + sbt[j:j + 1, :] for j in range(tk // NSA_SEL_LEN)],
                axis=0)
        m_old = m_ref[...]
        m_new = jnp.maximum(m_old, jnp.max(s, axis=0, keepdims=True))
        alpha = jnp.exp2(m_old - m_new)
        if wipe is not None:
            alpha = jnp.where(wipe, 0.0, alpha)
        alpha_scr[slot] = alpha
        p_scr[slot] = jnp.exp2(s - m_new).astype(BF)
        m_ref[...] = m_new

    def stage_pv(kt, slot):
        k0 = pl.multiple_of(kt * tk, tk)
        lhs = jnp.concatenate([vt_ref[0, :, pl.ds(k0, tk)], ones_rows], axis=0)
        acc_ref[...] = alpha_scr[slot] * acc_ref[...] + _dot(lhs, p_scr[slot])

    base = i * (tq // tk)
    near_kt = [jnp.maximum(base + a, 0) for a, _ in near]
    near_fm = [(base + a < 0) if a < 0 else None for a, _ in near]
    if mode == "causal":
        n_far = jnp.maximum(base + near[0][0], 0)
        delta = n_far % 2
        trips = (n_far + delta) // 2
        kt_of = lambda j: jnp.maximum(j - delta, 0)
        first_kt = kt_of(0)
    else:
        delta, trips = 0, 0
        kt_of = lambda j: near_kt[0]
        first_kt = near_kt[0]
    p_scr[1] = jnp.zeros((tk, lw), BF)
    alpha_scr[1] = jnp.ones((1, lw), F32)
    stage_qk(first_kt, 0)
    if mode == "causal":
        def body(u, c):
            j = 2 * u
            stage_qk(kt_of(j + 1), 1)
            stage_softmax(kt_of(j), 0)
            stage_pv(kt_of(j - 1), 1)
            stage_qk(kt_of(j + 2), 0)
            stage_softmax(kt_of(j + 1), 1, wipe=(u == 0) & (delta == 1))
            stage_pv(kt_of(j), 0)
            return c
        lax.fori_loop(0, trips, body, 0)
    for idx, (_, bidx) in enumerate(near):
        if idx + 1 < len(near):
            stage_qk(near_kt[idx + 1], (idx + 1) % 2)
        stage_softmax(near_kt[idx], idx % 2, bias_idx=bidx, full_mask=near_fm[idx])
        stage_pv(near_kt[idx - 1] if idx > 0 else kt_of(2 * trips - 1), (idx - 1) % 2)
    stage_pv(near_kt[-1], (len(near) - 1) % 2)

    acc = acc_ref[0:vdim, :]
    l = acc_ref[vdim:vdim + 1, :]
    if final == "da":
        lamv = lam_ref[...]
        lam = (jnp.exp(jnp.sum(lamv[0:1] * lamv[1:2], axis=1, keepdims=True))
               - jnp.exp(jnp.sum(lamv[2:3] * lamv[3:4], axis=1, keepdims=True)) + lam_init)
        o = acc[:, :tq] / l[:, :tq] - lam * (acc[:, tq:] / l[:, tq:])
        o = o * lax.rsqrt(jnp.mean(o * o, axis=0, keepdims=True) + NORM_EPS) * subg_ref[...] * (1.0 - lam_init)
        o_ref[0] = o.astype(o_ref.dtype)
    else:
        for r in range(nq_stack):
            o_ref[0, r * vdim:(r + 1) * vdim, :] = (acc[:, r * tq:(r + 1) * tq] / l[:, r * tq:(r + 1) * tq]).astype(o_ref.dtype)


def _attention(q_arr, k_arr, vt, bias, *, B, S, n_prog, q_spec, k_blk, v_blk, bias_blk, nq_stack, vdim, mode,
               out_rows, out_dtype, q_mask, tq, near, selb=None, final="plain", lam=None, subg=None, lam_init=0.0):
    tk = ATT_TK
    nq = S // tq
    lw = nq_stack * tq
    q_w, q_blk = q_spec
    ins = [q_arr, k_arr, vt, bias]
    in_specs = [
        pl.BlockSpec((tq, q_w), lambda b, h, i: (b * nq + i, q_blk(h))),
        pl.BlockSpec((S, LANES), lambda b, h, i: (b, k_blk(h))),
        pl.BlockSpec((1, vdim, S), lambda b, h, i: (b, v_blk(h), 0)),
        pl.BlockSpec((1,) + bias.shape[1:], lambda b, h, i: (bias_blk(h), 0, 0, 0)),
    ]
    if selb is not None:
        ins.append(selb)
        in_specs.append(pl.BlockSpec((1, 1, selb.shape[2], tq), lambda b, h, i: (b, h, 0, i)))
    if final == "da":
        ins += [lam, subg]
        in_specs += [pl.BlockSpec(lam.shape, lambda b, h, i: (0, 0)), pl.BlockSpec(subg.shape, lambda b, h, i: (0, 0))]
    body = functools.partial(_attn_body, tq=tq, nq_stack=nq_stack, vdim=vdim, mode=mode, use_sel=selb is not None,
                             final=final, lam_init=lam_init, q_mask=q_mask, near=near)
    return pl.pallas_call(
        body, grid=(B, n_prog, nq), in_specs=in_specs,
        out_specs=pl.BlockSpec((1, out_rows, tq), lambda b, h, i: (b, h, i)),
        out_shape=jax.ShapeDtypeStruct((B, out_rows * n_prog, S), out_dtype),
        scratch_shapes=[pltpu.VMEM((lw, LANES), BF), pltpu.VMEM((1, lw), F32), pltpu.VMEM((vdim + ONES_ROWS, lw), F32),
                        pltpu.VMEM((2, tk, lw), F32), pltpu.VMEM((2, tk, lw), BF), pltpu.VMEM((2, 1, lw), F32)],
        compiler_params=_cparams("arbitrary", "arbitrary", "arbitrary"), name="attn_" + mode + "_" + final)(*ins)


def _compress_body(x_ref, pe_ref, w1_ref, w2_ref, o_ref):
    xb = (x_ref[0] + pe_ref[0]).astype(BF)
    hcur = _dot(xb, w1_ref[0])
    hcur = 0.5 * hcur * (1.0 + jnp.tanh(math.sqrt(2.0 / math.pi) * (hcur + 0.044715 * (hcur * hcur * hcur))))
    o_ref[0] = _dot(hcur.astype(BF), w2_ref[0])


def _compress(xblk, pe, w1, w2, *, tr=512):
    _, R, K = xblk.shape
    return pl.pallas_call(
        _compress_body, grid=(2, R // tr),
        in_specs=[pl.BlockSpec((1, tr, K), lambda a, r: (a, r, 0)),
                  pl.BlockSpec((1, 1, K), lambda a, r: (a, 0, 0)),
                  pl.BlockSpec((1,) + w1.shape[1:], lambda a, r: (a, 0, 0)),
                  pl.BlockSpec((1,) + w2.shape[1:], lambda a, r: (a, 0, 0))],
        out_specs=pl.BlockSpec((1, tr, HEAD_DIM), lambda a, r: (a, r, 0)),
        out_shape=jax.ShapeDtypeStruct((2, R, HEAD_DIM), F32),
        compiler_params=_cparams("arbitrary", "arbitrary"), name="nsa_compress")(xblk, pe, w1, w2)


def _cmp_body(q_ref, kc_ref, vct_ref, bc_ref, ov_ref, o_ref, selb_ref, s_ref):
    tq = ATT_TQ
    lw = NSA_REP * tq
    i = pl.program_id(2)
    n_cmp = kc_ref.shape[2]
    n_sel = selb_ref.shape[2]
    band = bc_ref.shape[1]
    qt = q_ref[...]
    lane = lax.broadcasted_iota(jnp.int32, (tq, LANES), 1)
    keep = (lane // HEAD_DIM) == (pl.program_id(1) % 2)
    qm = jnp.concatenate(
        [jnp.where(keep, qt[:, r * LANES:(r + 1) * LANES], jnp.zeros((tq, LANES), BF)) for r in range(NSA_REP)], axis=0)
    s = _dot_nt(kc_ref[0, 0], qm)
    row = lax.broadcasted_iota(jnp.int32, s.shape, 0)
    s_ref[...] = jnp.where(row < (band // 2) * (i + 1), s, MASK_VALUE)

    @pl.when(i == 0)
    def _():
        s_ref[0:band // 2, :] = s_ref[0:band // 2, :] + bc_ref[0, band // 2:, :]

    @pl.when(i > 0)
    def _():
        r0 = pl.multiple_of((band // 2) * (i - 1), band // 2)
        s_ref[pl.ds(r0, band), :] = s_ref[pl.ds(r0, band), :] + bc_ref[0]

    s = s_ref[...]
    m = jnp.max(s, axis=0, keepdims=True)
    e = jnp.exp2(s - m)
    l = jnp.sum(e, axis=0, keepdims=True)
    inv = jnp.where(m > 0.5 * MASK_VALUE, 1.0 / l, 0.0)
    p = e * inv
    oc = _dot(vct_ref[0, 0], p.astype(BF))
    for r in range(NSA_REP):
        o_ref[0, r * HEAD_DIM:(r + 1) * HEAD_DIM, :] = oc[:, r * tq:(r + 1) * tq]
    ps = p[:, 0:tq]
    for r in range(1, NSA_REP):
        ps = ps + p[:, r * tq:(r + 1) * tq]
    p_hi = ps.astype(BF)
    r1 = ps - p_hi.astype(F32)
    p_mid = r1.astype(BF)
    p_lo = (r1 - p_mid.astype(F32)).astype(BF)
    ov = ov_ref[...]
    imp = _dot(ov, p_hi) + _dot(ov, p_mid) + _dot(ov, p_lo)
    j = lax.broadcasted_iota(jnp.int32, (n_sel, tq), 0)
    t = i * tq + lax.broadcasted_iota(jnp.int32, (n_sel, tq), 1)
    cur = t // NSA_SEL_LEN
    forced = (j == 0) | (j == cur) | (j == cur - 1)
    w = jnp.where(j <= cur, jnp.where(forced, NSA_FORCE_SCORE, imp), -1.0)
    jf = j.astype(F32)
    sel = jnp.zeros((n_sel, tq), F32)
    for _ in range(min(NSA_TOP_N, n_sel)):
        mx = jnp.max(w, axis=0, keepdims=True)
        idx = jnp.min(jnp.where(w == mx, jf, float(n_sel)), axis=0, keepdims=True)
        pick = jf == idx
        sel = jnp.where(pick, 1.0, sel)
        w = jnp.where(pick, -2.0, w)
    selb_ref[0, 0] = jnp.where(sel > 0.5, 0.0, MASK_VALUE)


def _cmp_attention(qn, kc, vct, bc, ov, *, B, S, q_coloff):
    tq = ATT_TQ
    nq = S // tq
    n_cmp = kc.shape[2]
    n_sel = S // NSA_SEL_LEN
    return pl.pallas_call(
        _cmp_body, grid=(B, NSA_GROUPS, nq),
        in_specs=[pl.BlockSpec((tq, NSA_REP * LANES), lambda b, g, i: (b * nq + i, q_coloff + g // 2)),
                  pl.BlockSpec((1, 1, n_cmp, LANES), lambda b, g, i: (b, g, 0, 0)),
                  pl.BlockSpec((1, 1, HEAD_DIM, n_cmp), lambda b, g, i: (b, g, 0, 0)),
                  pl.BlockSpec((1,) + bc.shape[1:], lambda b, g, i: (g, 0, 0)),
                  pl.BlockSpec(ov.shape, lambda b, g, i: (0, 0))],
        out_specs=[pl.BlockSpec((1, NSA_REP * HEAD_DIM, tq), lambda b, g, i: (b, g, i)),
                   pl.BlockSpec((1, 1, n_sel, tq), lambda b, g, i: (b, g, 0, i))],
        out_shape=[jax.ShapeDtypeStruct((B, D_MODEL, S), F32), jax.ShapeDtypeStruct((B, NSA_GROUPS, n_sel, S), F32)],
        scratch_shapes=[pltpu.VMEM((n_cmp, NSA_REP * tq), F32)],
        compiler_params=_cparams("arbitrary", "arbitrary", "arbitrary"), name="nsa_cmp_topk")(qn, kc, vct, bc, ov)


def _outproj_body(*refs, n_o):
    it = iter(refs)
    o_refs = [next(it) for _ in range(n_o)]
    gt_ref = next(it) if n_o > 1 else None
    w_ref, x_ref, g_ref, out_ref = (next(it) for _ in range(4))
    if n_o == 1:
        ot = o_refs[0][0]
    else:
        gts = gt_ref[0]
        parts = []
        for hd in range(N_HEADS):
            rows = slice(hd * HEAD_DIM, (hd + 1) * HEAD_DIM)
            acc = gts[hd:hd + 1, :] * o_refs[0][0, rows, :]
            for b in range(1, n_o):
                acc = acc + gts[b * N_HEADS + hd:b * N_HEADS + hd + 1, :] * o_refs[b][0, rows, :]
            parts.append(acc.astype(BF))
        ot = jnp.concatenate(parts, axis=0)
    y = _dot_tn(ot, w_ref[...])
    out_ref[...] = x_ref[...] + _rms(y, g_ref[...])


def _outproj(o_list, gt, w, x2d, g, *, B, S, tm=512):
    M, D = x2d.shape
    ns = S // tm
    ins = list(o_list)
    in_specs = [pl.BlockSpec((1, D, tm), lambda b, s: (b, 0, s)) for _ in o_list]
    if gt is not None:
        ins.append(gt)
        in_specs.append(pl.BlockSpec((1, gt.shape[1], tm), lambda b, s: (b, 0, s)))
    ins += [w, x2d, g]
    in_specs += [pl.BlockSpec(w.shape, lambda b, s: (0, 0)),
                 pl.BlockSpec((tm, D), lambda b, s: (b * ns + s, 0)),
                 pl.BlockSpec(g.shape, lambda b, s: (0, 0))]
    return pl.pallas_call(
        functools.partial(_outproj_body, n_o=len(o_list)), grid=(B, ns), in_specs=in_specs,
        out_specs=pl.BlockSpec((tm, D), lambda b, s: (b * ns + s, 0)),
        out_shape=jax.ShapeDtypeStruct((M, D), F32),
        compiler_params=_cparams("arbitrary", "arbitrary"), name="outproj")(*ins)


def _mlp_body(x_ref, g2_ref, w1_ref, w2_ref, g3_ref, wg_ref, p_ref, wp_ref, out_ref, h_ref, acc_ref):
    f = pl.program_id(1)

    @pl.when(f == 0)
    def _():
        h_ref[...] = _rms(x_ref[...], g2_ref[...]).astype(BF)
        acc_ref[...] = jnp.zeros_like(acc_ref)

    a = jnp.maximum(_dot(h_ref[...], w1_ref[...]), 0.0)
    acc_ref[...] += _dot((a * a).astype(BF), w2_ref[...])

    @pl.when(f == pl.num_programs(1) - 1)
    def _():
        x2 = x_ref[...] + _rms(acc_ref[...], g3_ref[...])
        gate = jax.nn.sigmoid(_dot(x2.astype(BF), wg_ref[...]))
        out_ref[...] = x2 + gate * _dot(p_ref[...].astype(BF), wp_ref[...])


def _mlp_ple(x2d, g2, w1, w2, g3, wg, p2d, wp, *, tm=512, tf=1024):
    M, D = x2d.shape
    FF = w1.shape[1]
    PD = p2d.shape[1]
    const = lambda a: pl.BlockSpec(a.shape, lambda m, f: (0, 0))
    return pl.pallas_call(
        _mlp_body, grid=(M // tm, FF // tf),
        in_specs=[pl.BlockSpec((tm, D), lambda m, f: (m, 0)), const(g2),
                  pl.BlockSpec((D, tf), lambda m, f: (0, f)), pl.BlockSpec((tf, D), lambda m, f: (f, 0)),
                  const(g3), const(wg), pl.BlockSpec((tm, PD), lambda m, f: (m, 0)), const(wp)],
        out_specs=pl.BlockSpec((tm, D), lambda m, f: (m, 0)),
        out_shape=jax.ShapeDtypeStruct((M, D), F32),
        scratch_shapes=[pltpu.VMEM((tm, D), BF), pltpu.VMEM((tm, D), F32)],
        compiler_params=_cparams("arbitrary", "arbitrary"), name="mlp_ple")(x2d, g2, w1, w2, g3, wg, p2d, wp)


def _t5_bucket(dist):
    n = jnp.maximum(dist, 0)
    max_exact = REL_BUCKETS // 2
    nf = jnp.maximum(n, 1).astype(F32)
    large = max_exact + (jnp.log(nf / max_exact) / math.log(REL_MAX_DIST / max_exact)
                         * (REL_BUCKETS - max_exact)).astype(jnp.int32)
    large = jnp.minimum(large, REL_BUCKETS - 1)
    return jnp.where(n < max_exact, n, large)


def _bias_by_distance(rel_bias):
    tb = rel_bias[_t5_bucket(jnp.arange(REL_TABLE_LEN))].astype(F32)
    return ((tb - rel_bias[REL_BUCKETS - 1].astype(F32)[None, :]) * LOG2E).T


def _toeplitz(fn, rows, cols, off, row_stride=1):
    n = row_stride * rows + cols
    d = jnp.arange(n)
    d = jnp.where(d < cols, d, d - n)
    v = fn(d + off)
    flat = jnp.tile(v, (1,) * (v.ndim - 1) + (rows,))[..., :rows * (n - row_stride)]
    return flat.reshape(v.shape[:-1] + (rows, n - row_stride))[..., :cols]


def _bias_fn(tbs, hi_limit=None):
    def fn(d):
        val = jnp.zeros((1,) + d.shape, F32) if tbs is None else tbs[:, jnp.clip(d, 0, REL_TABLE_LEN - 1)]
        bad = d < 0
        if hi_limit is not None:
            bad = bad | (d >= hi_limit)
        return jnp.where(bad[None, :], MASK_VALUE, val)
    return fn


def _near_tables(tbs, tq, offsets, hi_limit=None):
    return jnp.stack([_toeplitz(_bias_fn(tbs, hi_limit), ATT_TK, tq, -a * ATT_TK) for a in offsets], axis=1)


def _stack_heads(t, n_grp, n_stack):
    h, nt, r, c = t.shape
    return t.reshape(n_grp, n_stack, nt, r, c).transpose(0, 2, 3, 1, 4).reshape(n_grp, nt, r, n_stack * c)


def _cmp_band_table(tbs):
    tq = ATT_TQ
    band = 2 * (tq // NSA_CMP_STRIDE)
    return _toeplitz(_bias_fn(tbs), band, tq, tq - (NSA_CMP_LEN - 1), row_stride=NSA_CMP_STRIDE)


def _nsa_q_perm():
    cols = []
    for p in range(NSA_GROUPS // 2):
        for r in range(NSA_REP):
            for g in (2 * p, 2 * p + 1):
                h = g * NSA_REP + r
                cols.extend(range(h * HEAD_DIM, (h + 1) * HEAD_DIM))
    return jnp.asarray(cols, jnp.int32)


def _row(v):
    return v.reshape(1, -1).astype(F32)


def kernel(x, p, rel_bias, norm_g, mlp_w1, mlp_w2, ple_w, ple_gate_w, da_w_in, da_lambda, da_subln, da_w_out,
           nsa_w_in, nsa_cmp_pe, nsa_cmp_w1, nsa_cmp_w2, nsa_w_out, fox_w_in, fox_b_f, fox_w_out):
    B, S, D = x.shape
    depth = p.shape[0]
    scale = HEAD_DIM ** -0.5 * LOG2E
    tbs = _bias_by_distance(rel_bias)
    da_near = (-1, 0, 1)
    da_bias = _stack_heads(_near_tables(tbs, DA_TQ, da_near), DA_HEADS, 2)
    x2d = x.reshape(B * S, D)
    ia = ib = ic = 0
    for i in range(depth):
        g = norm_g[i]
        kind = i % N_MIXERS
        if kind == 0:
            lam_init = 0.8 - 0.6 * math.exp(-0.3 * i)
            w_in = da_w_in[ia]
            wn = jnp.concatenate([w_in[:, :D] * scale, w_in[:, D:2 * D]], axis=1).astype(BF)
            wt = w_in[:, 2 * D:].T.astype(BF)
            qk, vt = _inproj(x2d, _row(g[0]), wn, jnp.zeros((1, 2 * D), F32), wt, B=B, S=S)
            ot = _attention(qk, qk, vt, da_bias, B=B, S=S, n_prog=DA_HEADS, q_spec=(LANES, lambda h: h),
                            k_blk=lambda h: DA_HEADS + h, v_blk=lambda h: h, bias_blk=lambda h: h, nq_stack=2,
                            vdim=2 * HEAD_DIM, mode="causal", out_rows=2 * HEAD_DIM, out_dtype=BF, q_mask="da",
                            tq=DA_TQ, near=[(a, n) for n, a in enumerate(da_near)],
                            final="da", lam=da_lambda[ia].astype(F32), subg=da_subln[ia].reshape(-1, 1).astype(F32),
                            lam_init=lam_init)
            x2d = _outproj([ot], None, da_w_out[ia].astype(BF), x2d, _row(g[1]), B=B, S=S)
            ia += 1
        elif kind == 1:
            w_in = nsa_w_in[ib]
            kvd = NSA_GROUPS * HEAD_DIM
            wq = w_in[:, :D][:, _nsa_q_perm()] * scale
            kv = [w_in[:, D + a * kvd:D + (a + 1) * kvd] for a in range(6)]
            wn = jnp.concatenate([wq, kv[2], kv[4]], axis=1).astype(BF)
            wt = jnp.concatenate([kv[3], kv[5]], axis=1).T.astype(BF)
            wf32 = jnp.concatenate([kv[0], kv[1]], axis=1).astype(BF)
            wgate = w_in[:, D + 6 * kvd:].reshape(D, N_HEADS, 3).transpose(2, 1, 0).reshape(3 * N_HEADS, D).astype(BF)
            qn, vt, cmp_in, gates_t = _inproj(x2d, _row(g[0]), wn, jnp.zeros((1, wn.shape[1]), F32), wt, B=B, S=S,
                                              wf32=wf32, wg=wgate)
            n_chunk = S // NSA_CMP_STRIDE
            cm = cmp_in.reshape(B, n_chunk, NSA_CMP_STRIDE, 2, NSA_GROUPS, HEAD_DIM).transpose(3, 0, 1, 4, 2, 5)
            cm = cm.reshape(2, B, n_chunk, NSA_GROUPS, NSA_CMP_STRIDE * HEAD_DIM)
            nxt = jnp.concatenate([cm[:, :, 1:], jnp.zeros_like(cm[:, :, :1])], axis=2)
            xblk = jnp.concatenate([cm, nxt], axis=-1).reshape(2, B * n_chunk * NSA_GROUPS, NSA_CMP_LEN * HEAD_DIM)
            pe = nsa_cmp_pe[ib].reshape(2, 1, NSA_CMP_LEN * HEAD_DIM).astype(F32)
            kvc = _compress(xblk, pe, nsa_cmp_w1[ib].astype(BF), nsa_cmp_w2[ib].astype(BF))
            kvc = kvc.reshape(2, B, n_chunk, NSA_GROUPS, HEAD_DIM).transpose(0, 1, 3, 2, 4)
            kc = jnp.concatenate([kvc[0], kvc[0]], axis=-1).astype(BF)
            vct = kvc[1].transpose(0, 1, 3, 2).astype(BF)
            jj = jnp.arange(S // NSA_SEL_LEN)[:, None]
            nn = jnp.arange(n_chunk)[None, :]
            n_cmp_blocks = (S - NSA_CMP_LEN) // NSA_CMP_STRIDE + 1
            ov = ((nn * NSA_CMP_STRIDE < (jj + 1) * NSA_SEL_LEN) & (nn * NSA_CMP_STRIDE + NSA_CMP_LEN - 1 >= jj * NSA_SEL_LEN)
                  & (nn < n_cmp_blocks)).astype(BF)
            cmp_band = _stack_heads(_cmp_band_table(tbs)[:, None], NSA_GROUPS, NSA_REP)[:, 0]
            oc_t, selb = _cmp_attention(qn, kc, vct, cmp_band, ov, B=B, S=S, q_coloff=0)
            nsa_kw = dict(B=B, S=S, n_prog=NSA_GROUPS, q_spec=(NSA_REP * LANES, lambda h: h // 2),
                          bias_blk=lambda h: h, nq_stack=NSA_REP, vdim=HEAD_DIM, out_rows=NSA_REP * HEAD_DIM,
                          out_dtype=F32, q_mask="group_parity", tq=ATT_TQ)
            kblk0 = D // LANES
            sel_bias = _stack_heads(_near_tables(tbs, ATT_TQ, (-1, 0)), NSA_GROUPS, NSA_REP)
            win_bias = _stack_heads(_near_tables(tbs, ATT_TQ, (-2, -1, 0), hi_limit=NSA_WINDOW), NSA_GROUPS, NSA_REP)
            os_t = _attention(qn, qn, vt, sel_bias, mode="causal", near=[(-1, 0), (0, 1)],
                              k_blk=lambda h: kblk0 + h // 2, v_blk=lambda h: h, selb=selb, **nsa_kw)
            ow_t = _attention(qn, qn, vt, win_bias, mode="window", near=[(-2, 0), (-1, 1), (0, 2)],
                              k_blk=lambda h: kblk0 + 2 + h // 2, v_blk=lambda h: NSA_GROUPS + h, **nsa_kw)
            x2d = _outproj([oc_t, os_t, ow_t], gates_t, nsa_w_out[ib].astype(BF), x2d, _row(g[1]), B=B, S=S)
            ib += 1
        else:
            w_in = fox_w_in[ic]
            z64 = jnp.zeros((D, N_HEADS, HEAD_DIM), F32)
            wq = jnp.concatenate([(w_in[:, :D] * scale).reshape(D, N_HEADS, HEAD_DIM), z64], axis=2).reshape(D, 2 * D)
            wk = jnp.concatenate([w_in[:, D:2 * D].reshape(D, N_HEADS, HEAD_DIM), z64], axis=2).reshape(D, 2 * D)
            wn = jnp.concatenate([wq, wk], axis=1).astype(BF)
            aug = jnp.zeros((N_HEADS, LANES), F32).at[:, HEAD_DIM:HEAD_DIM + 3].set(1.0).reshape(1, 2 * D)
            bn = jnp.concatenate([aug, jnp.zeros((1, 2 * D), F32)], axis=1)
            wf = w_in[:, 3 * D:]
            wf3 = jnp.concatenate([wf, wf, wf, jnp.zeros((D, LANES - 3 * N_HEADS), F32)], axis=1).astype(BF)
            bf3 = jnp.concatenate([fox_b_f[ic]] * 3 + [jnp.zeros((LANES - 3 * N_HEADS,), F32)]).reshape(1, LANES).astype(F32)
            hh = jnp.arange(N_HEADS)
            pm = jnp.zeros((LANES, 2 * D), F32)
            for a in range(3):
                pm = pm.at[a * N_HEADS + hh, hh * LANES + HEAD_DIM + a].set(-1.0)
            qk, vt = _inproj(x2d, _row(g[0]), wn, bn, w_in[:, 2 * D:3 * D].T.astype(BF), B=B, S=S,
                             fox_extra=[wf3, bf3, pm.astype(BF)])
            fox_near = tuple(range(FOX_TQ // ATT_TK))
            mask_t = _near_tables(None, FOX_TQ, fox_near)
            ot = _attention(qk, qk, vt, mask_t, B=B, S=S, n_prog=N_HEADS, q_spec=(LANES, lambda h: h),
                            k_blk=lambda h: N_HEADS + h, v_blk=lambda h: h, bias_blk=lambda h: 0, nq_stack=1,
                            vdim=HEAD_DIM, mode="causal", out_rows=HEAD_DIM, out_dtype=BF, q_mask="none",
                            tq=FOX_TQ, near=[(a, a) for a in fox_near])
            x2d = _outproj([ot], None, fox_w_out[ic].astype(BF), x2d, _row(g[1]), B=B, S=S)
            ic += 1
        x2d = _mlp_ple(x2d, _row(g[2]), mlp_w1[i].astype(BF), mlp_w2[i].astype(BF), _row(g[3]),
                       ple_gate_w[i].astype(BF), p[i].reshape(B * S, -1), ple_w[i].astype(BF))
    return x2d.reshape(B, S, D)
```

```python
import functools
import math

import jax
import jax.numpy as jnp
from jax import lax
from jax.experimental import pallas as pl
from jax.experimental.pallas import tpu as pltpu

BF = jnp.bfloat16
F32 = jnp.float32

D_MODEL = 1024
HEAD_DIM = 64
LANES = 128
NORM_EPS = 1e-6
MASK_VALUE = -1e30
REL_BUCKETS = 32
REL_MAX_DIST = 128
REL_TABLE_LEN = 512
N_HEADS = D_MODEL // HEAD_DIM
DA_HEADS = N_HEADS // 2
NSA_GROUPS = 4
NSA_REP = N_HEADS // NSA_GROUPS
NSA_CMP_LEN = 32
NSA_CMP_STRIDE = 16
NSA_SEL_LEN = 64
NSA_TOP_N = 16
NSA_WINDOW = 512
NSA_FORCE_SCORE = 1e4
N_MIXERS = 3

ATT_TQ = 256
DA_TQ = 512
FOX_TQ = 1024
ATT_TK = 256
VMEM_LIMIT = 56 * 1024 * 1024
ONES_ROWS = 16
LOG2E = math.log2(math.e)


def _cparams(*sem):
    return pltpu.CompilerParams(dimension_semantics=sem, vmem_limit_bytes=VMEM_LIMIT)


def _rms(x, g):
    return x * lax.rsqrt(jnp.mean(x * x, axis=-1, keepdims=True) + NORM_EPS) * g


def _dot(a, b):
    return jnp.dot(a, b, preferred_element_type=F32)


def _dot_nt(a, b):
    return lax.dot_general(a, b, (((1,), (1,)), ((), ())), preferred_element_type=F32)


def _dot_tn(a, b):
    return lax.dot_general(a, b, (((0,), (0,)), ((), ())), preferred_element_type=F32)


def _inproj_body(*refs, n_f32, n_gate, fox, tm, col_chunk):
    it = iter(refs)
    x_ref, g_ref, wn_ref, bn_ref, wt_ref = (next(it) for _ in range(5))
    wf32_ref = next(it) if n_f32 else None
    wg_ref = next(it) if n_gate else None
    if fox:
        wf_ref, bf_ref, pm_ref = next(it), next(it), next(it)
    on_ref, ot_ref = next(it), next(it)
    of32_ref = next(it) if n_f32 else None
    og_ref = next(it) if n_gate else None
    carry_ref = next(it) if fox else None

    h = _rms(x_ref[...], g_ref[...]).astype(BF)
    n_nat = on_ref.shape[1]
    k_off = n_nat // 2
    if fox:
        @pl.when(pl.program_id(1) == 0)
        def _():
            carry_ref[...] = jnp.zeros_like(carry_ref)

        lf = _dot(h, wf_ref[...]) + bf_ref[...]
        lane = lax.broadcasted_iota(jnp.int32, lf.shape, 1)
        row = lax.broadcasted_iota(jnp.int32, lf.shape, 0)
        ls = jnp.minimum(lf, 0.0) - jnp.log1p(jnp.exp(-jnp.abs(lf)))
        c = jnp.where(lane < 3 * N_HEADS, ls, 0.0)
        k = 1
        while k < tm:
            c = c + jnp.where(row >= k, pltpu.roll(c, k, axis=0), 0.0)
            k *= 2
        c = c + carry_ref[...]
        carry_ref[...] = c[tm - 1:tm, :]
        c = c * LOG2E
        hi = c.astype(BF).astype(F32)
        r1 = c - hi
        mid = r1.astype(BF).astype(F32)
        lo = (r1 - mid).astype(BF).astype(F32)
        c3 = jnp.where(lane < N_HEADS, hi, jnp.where(lane < 2 * N_HEADS, mid, lo)).astype(BF)
    for c0 in range(0, n_nat, col_chunk):
        y = _dot(h, wn_ref[:, c0:c0 + col_chunk]) + bn_ref[:, c0:c0 + col_chunk]
        if fox and c0 >= k_off:
            y = y + _dot(c3, pm_ref[:, c0 - k_off:c0 - k_off + col_chunk])
        on_ref[:, c0:c0 + col_chunk] = y.astype(on_ref.dtype)
    ot_ref[0] = _dot_nt(wt_ref[...], h).astype(ot_ref.dtype)
    if n_f32:
        of32_ref[...] = _dot(h, wf32_ref[...])
    if n_gate:
        og_ref[0] = jax.nn.sigmoid(_dot_nt(wg_ref[...], h))


def _inproj(x2d, g, wn, bn, wt, *, B, S, tm=512, wf32=None, wg=None, fox_extra=None):
    M, D = x2d.shape
    ns = S // tm
    n_nat, n_t = wn.shape[1], wt.shape[0]
    full = lambda a: pl.BlockSpec(a.shape, lambda b, s: (0,) * a.ndim)
    ins = [x2d, g, wn, bn, wt]
    in_specs = [pl.BlockSpec((tm, D), lambda b, s: (b * ns + s, 0)), full(g), full(wn), full(bn), full(wt)]
    out_shape = [jax.ShapeDtypeStruct((M, n_nat), BF), jax.ShapeDtypeStruct((B, n_t, S), BF)]
    out_specs = [pl.BlockSpec((tm, n_nat), lambda b, s: (b * ns + s, 0)),
                 pl.BlockSpec((1, n_t, tm), lambda b, s: (b, 0, s))]
    scratch = []
    if wf32 is not None:
        ins.append(wf32)
        in_specs.append(full(wf32))
        out_shape.append(jax.ShapeDtypeStruct((M, wf32.shape[1]), F32))
        out_specs.append(pl.BlockSpec((tm, wf32.shape[1]), lambda b, s: (b * ns + s, 0)))
    if wg is not None:
        ins.append(wg)
        in_specs.append(full(wg))
        out_shape.append(jax.ShapeDtypeStruct((B, wg.shape[0], S), F32))
        out_specs.append(pl.BlockSpec((1, wg.shape[0], tm), lambda b, s: (b, 0, s)))
    if fox_extra is not None:
        for a in fox_extra:
            ins.append(a)
            in_specs.append(full(a))
        scratch.append(pltpu.VMEM((1, LANES), F32))
    body = functools.partial(_inproj_body, n_f32=wf32 is not None, n_gate=wg is not None,
                             fox=fox_extra is not None, tm=tm, col_chunk=1024 if n_nat % 1024 == 0 else 512)
    return pl.pallas_call(
        body, grid=(B, ns), in_specs=in_specs, out_specs=out_specs, out_shape=out_shape,
        scratch_shapes=scratch, compiler_params=_cparams("arbitrary", "arbitrary"), name="inproj")(*ins)


def _attn_body(*refs, tq, nq_stack, vdim, mode, use_sel, final, lam_init, q_mask, near):
    it = iter(refs)
    q_ref, k_ref, vt_ref, bias_ref = (next(it) for _ in range(4))
    selb_ref = next(it) if use_sel else None
    if final == "da":
        lam_ref, subg_ref = next(it), next(it)
    o_ref = next(it)
    qm_ref, m_ref, acc_ref, s_scr, p_scr, alpha_scr, mt_scr = (next(it) for _ in range(7))
    tk = ATT_TK
    lw = nq_stack * tq
    i = pl.program_id(2)

    qt = q_ref[...]
    lane = lax.broadcasted_iota(jnp.int32, (tq, LANES), 1)
    for r in range(nq_stack):
        blk = qt[:, 0:LANES] if q_mask == "da" else qt[:, r * LANES:(r + 1) * LANES]
        if q_mask == "da":
            keep = (lane < HEAD_DIM) if r == 0 else (lane >= HEAD_DIM)
        elif q_mask == "group_parity":
            keep = (lane // HEAD_DIM) == (pl.program_id(1) % 2)
        else:
            keep = None
        if keep is not None:
            blk = jnp.where(keep, blk, jnp.zeros_like(blk))
        qm_ref[r * tq:(r + 1) * tq, :] = blk
    m_ref[...] = jnp.full_like(m_ref, MASK_VALUE)
    acc_ref[...] = jnp.zeros_like(acc_ref)
    ones_rows = jnp.ones((ONES_ROWS, tk), BF)

    def stage_qk(kt, r, t):
        k0 = pl.multiple_of(kt * tk, tk)
        s = _dot_nt(k_ref[pl.ds(k0, tk), :], qm_ref[...])
        if use_sel:
            sb8 = selb_ref[0, 0, pl.ds(pl.multiple_of((kt // 2) * 8, 8), 8), :]
            sb4 = jnp.where(kt % 2 == 0, sb8[0:4, :], sb8[4:8, :])
            sbt = jnp.concatenate([sb4] * nq_stack, axis=1)
            s = jnp.concatenate(
                [s[j * NSA_SEL_LEN:(j + 1) * NSA_SEL_LEN, :] + sbt[j:j + 1, :] for j in range(tk // NSA_SEL_LEN)],
                axis=0)
        s_scr[r, t] = s
        mt_scr[r, t] = jnp.max(s, axis=0, keepdims=True)

    def stage_softmax(r, t, bias_idx=None, full_mask=None, wipe=None):
        s = s_scr[r, t]
        if bias_idx is None and full_mask is None:
            mt = mt_scr[r, t]
        else:
            if bias_idx is not None:
                s = s + bias_ref[0, bias_idx]
            if full_mask is not None:
                s = s + jnp.where(full_mask, MASK_VALUE, 0.0)
            mt = jnp.max(s, axis=0, keepdims=True)
        m_old = m_ref[...]
        m_new = jnp.maximum(m_old, mt)
        alpha = jnp.exp2(m_old - m_new)
        if wipe is not None:
            alpha = jnp.where(wipe, 0.0, alpha)
        alpha_scr[r, t] = alpha
        p_scr[r, t] = jnp.exp2(s - m_new).astype(BF)
        m_ref[...] = m_new

    def stage_pv(kt, r, t):
        k0 = pl.multiple_of(kt * tk, tk)
        lhs = jnp.concatenate([vt_ref[0, :, pl.ds(k0, tk)], ones_rows], axis=0)
        acc_ref[...] = alpha_scr[r, t] * acc_ref[...] + _dot(lhs, p_scr[r, t])

    def trip(r, pv_kts, sm_tiles, qk_kts):
        if not isinstance(r, int):
            for val in (0, 1):
                @pl.when(r == val)
                def _(val=val):
                    trip(val, pv_kts, sm_tiles, qk_kts)
            return
        for t, kt in enumerate(qk_kts):
            stage_qk(kt, 1 - r, t)
        for t, kw in enumerate(sm_tiles):
            stage_softmax(r, t, **kw)
        for t, kt in enumerate(pv_kts):
            stage_pv(kt, r, t)

    base = i * (tq // tk)
    near_kt = [jnp.maximum(base + a, 0) for a, _ in near]
    near_sm = [dict(bias_idx=bidx, full_mask=(base + a < 0) if a < 0 else None) for a, bidx in near]
    near_pairs = [list(range(n, min(n + 2, len(near)))) for n in range(0, len(near), 2)]
    if mode == "causal":
        n_far = jnp.maximum(base + near[0][0], 0)
        delta = n_far % 2
        trips = (n_far + delta) // 2
        kt_of = lambda j: jnp.maximum(j - delta, 0)
        first = [jnp.where(trips > 0, kt_of(n), near_kt[n]) for n in near_pairs[0]]
    else:
        trips = 0
        kt_of = lambda j: near_kt[0]
        first = [near_kt[n] for n in near_pairs[0]]
    for t, kt in enumerate(first):
        stage_qk(kt, 0, t)
    if mode == "causal":
        def body(u, c):
            j = 2 * u
            trip(u % 2, [kt_of(j), kt_of(j + 1)], [{}, dict(wipe=(u == 0) & (delta == 1))],
                 [kt_of(j + 2), kt_of(j + 3)])
            return c
        lax.fori_loop(0, trips, body, 0)
    for e, pair in enumerate(near_pairs):
        next_kts = [near_kt[n] for n in near_pairs[e + 1]] if e + 1 < len(near_pairs) else []
        trip((trips + e) % 2, [near_kt[n] for n in pair], [near_sm[n] for n in pair], next_kts)

    acc = acc_ref[0:vdim, :]
    l = acc_ref[vdim:vdim + 1, :]
    if final == "da":
        lamv = lam_ref[...]
        lam = (jnp.exp(jnp.sum(lamv[0:1] * lamv[1:2], axis=1, keepdims=True))
               - jnp.exp(jnp.sum(lamv[2:3] * lamv[3:4], axis=1, keepdims=True)) + lam_init)
        o = acc[:, :tq] / l[:, :tq] - lam * (acc[:, tq:] / l[:, tq:])
        o = o * lax.rsqrt(jnp.mean(o * o, axis=0, keepdims=True) + NORM_EPS) * subg_ref[...] * (1.0 - lam_init)
        o_ref[0] = o.astype(o_ref.dtype)
    else:
        for r in range(nq_stack):
            o_ref[0, r * vdim:(r + 1) * vdim, :] = (acc[:, r * tq:(r + 1) * tq] / l[:, r * tq:(r + 1) * tq]).astype(o_ref.dtype)


def _attention(q_arr, k_arr, vt, bias, *, B, S, n_prog, q_spec, k_blk, v_blk, bias_blk, nq_stack, vdim, mode,
               out_rows, out_dtype, q_mask, tq, near, selb=None, final="plain", lam=None, subg=None, lam_init=0.0):
    tk = ATT_TK
    nq = S // tq
    lw = nq_stack * tq
    q_w, q_blk = q_spec
    ins = [q_arr, k_arr, vt, bias]
    in_specs = [
        pl.BlockSpec((tq, q_w), lambda b, h, i: (b * nq + i, q_blk(h))),
        pl.BlockSpec((S, LANES), lambda b, h, i: (b, k_blk(h))),
        pl.BlockSpec((1, vdim, S), lambda b, h, i: (b, v_blk(h), 0)),
        pl.BlockSpec((1,) + bias.shape[1:], lambda b, h, i: (bias_blk(h), 0, 0, 0)),
    ]
    if selb is not None:
        ins.append(selb)
        in_specs.append(pl.BlockSpec((1, 1, selb.shape[2], tq), lambda b, h, i: (b, h, 0, i)))
    if final == "da":
        ins += [lam, subg]
        in_specs += [pl.BlockSpec(lam.shape, lambda b, h, i: (0, 0)), pl.BlockSpec(subg.shape, lambda b, h, i: (0, 0))]
    body = functools.partial(_attn_body, tq=tq, nq_stack=nq_stack, vdim=vdim, mode=mode, use_sel=selb is not None,
                             final=final, lam_init=lam_init, q_mask=q_mask, near=near)
    return pl.pallas_call(
        body, grid=(B, n_prog, nq), in_specs=in_specs,
        out_specs=pl.BlockSpec((1, out_rows, tq), lambda b, h, i: (b, h, i)),
        out_shape=jax.ShapeDtypeStruct((B, out_rows * n_prog, S), out_dtype),
        scratch_shapes=[pltpu.VMEM((lw, LANES), BF), pltpu.VMEM((1, lw), F32), pltpu.VMEM((vdim + ONES_ROWS, lw), F32),
                        pltpu.VMEM((2, 2, tk, lw), F32), pltpu.VMEM((2, 2, tk, lw), BF), pltpu.VMEM((2, 2, 1, lw), F32),
                        pltpu.VMEM((2, 2, 1, lw), F32)],
        compiler_params=_cparams("arbitrary", "arbitrary", "arbitrary"), name="attn_" + mode + "_" + final)(*ins)


def _compress_body(x_ref, pe_ref, w1_ref, w2_ref, o_ref):
    xb = (x_ref[0] + pe_ref[0]).astype(BF)
    hcur = _dot(xb, w1_ref[0])
    hcur = 0.5 * hcur * (1.0 + jnp.tanh(math.sqrt(2.0 / math.pi) * (hcur + 0.044715 * (hcur * hcur * hcur))))
    o_ref[0] = _dot(hcur.astype(BF), w2_ref[0])


def _compress(xblk, pe, w1, w2, *, tr=512):
    _, R, K = xblk.shape
    return pl.pallas_call(
        _compress_body, grid=(2, R // tr),
        in_specs=[pl.BlockSpec((1, tr, K), lambda a, r: (a, r, 0)),
                  pl.BlockSpec((1, 1, K), lambda a, r: (a, 0, 0)),
                  pl.BlockSpec((1,) + w1.shape[1:], lambda a, r: (a, 0, 0)),
                  pl.BlockSpec((1,) + w2.shape[1:], lambda a, r: (a, 0, 0))],
        out_specs=pl.BlockSpec((1, tr, HEAD_DIM), lambda a, r: (a, r, 0)),
        out_shape=jax.ShapeDtypeStruct((2, R, HEAD_DIM), F32),
        compiler_params=_cparams("arbitrary", "arbitrary"), name="nsa_compress")(xblk, pe, w1, w2)


def _cmp_body(q_ref, kc_ref, vct_ref, bc_ref, ov_ref, o_ref, selb_ref, s_ref):
    tq = ATT_TQ
    lw = NSA_REP * tq
    i = pl.program_id(2)
    n_cmp = kc_ref.shape[2]
    n_sel = selb_ref.shape[2]
    band = bc_ref.shape[1]
    qt = q_ref[...]
    lane = lax.broadcasted_iota(jnp.int32, (tq, LANES), 1)
    keep = (lane // HEAD_DIM) == (pl.program_id(1) % 2)
    qm = jnp.concatenate(
        [jnp.where(keep, qt[:, r * LANES:(r + 1) * LANES], jnp.zeros((tq, LANES), BF)) for r in range(NSA_REP)], axis=0)
    s = _dot_nt(kc_ref[0, 0], qm)
    row = lax.broadcasted_iota(jnp.int32, s.shape, 0)
    s_ref[...] = jnp.where(row < (band // 2) * (i + 1), s, MASK_VALUE)

    @pl.when(i == 0)
    def _():
        s_ref[0:band // 2, :] = s_ref[0:band // 2, :] + bc_ref[0, band // 2:, :]

    @pl.when(i > 0)
    def _():
        r0 = pl.multiple_of((band // 2) * (i - 1), band // 2)
        s_ref[pl.ds(r0, band), :] = s_ref[pl.ds(r0, band), :] + bc_ref[0]

    s = s_ref[...]
    m = jnp.max(s, axis=0, keepdims=True)
    e = jnp.exp2(s - m)
    l = jnp.sum(e, axis=0, keepdims=True)
    inv = jnp.where(m > 0.5 * MASK_VALUE, 1.0 / l, 0.0)
    p = e * inv
    oc = _dot(vct_ref[0, 0], p.astype(BF))
    for r in range(NSA_REP):
        o_ref[0, r * HEAD_DIM:(r + 1) * HEAD_DIM, :] = oc[:, r * tq:(r + 1) * tq]
    ps = p[:, 0:tq]
    for r in range(1, NSA_REP):
        ps = ps + p[:, r * tq:(r + 1) * tq]
    p_hi = ps.astype(BF)
    r1 = ps - p_hi.astype(F32)
    p_mid = r1.astype(BF)
    p_lo = (r1 - p_mid.astype(F32)).astype(BF)
    ov = ov_ref[...]
    imp = _dot(ov, p_hi) + _dot(ov, p_mid) + _dot(ov, p_lo)
    j = lax.broadcasted_iota(jnp.int32, (n_sel, tq), 0)
    t = i * tq + lax.broadcasted_iota(jnp.int32, (n_sel, tq), 1)
    cur = t // NSA_SEL_LEN
    forced = (j == 0) | (j == cur) | (j == cur - 1)
    w = jnp.where(j <= cur, jnp.where(forced, NSA_FORCE_SCORE, imp), -1.0)
    jf = j.astype(F32)
    sel = jnp.zeros((n_sel, tq), F32)
    for _ in range(min(NSA_TOP_N, n_sel)):
        mx = jnp.max(w, axis=0, keepdims=True)
        idx = jnp.min(jnp.where(w == mx, jf, float(n_sel)), axis=0, keepdims=True)
        pick = jf == idx
        sel = jnp.where(pick, 1.0, sel)
        w = jnp.where(pick, -2.0, w)
    selb_ref[0, 0] = jnp.where(sel > 0.5, 0.0, MASK_VALUE)


def _cmp_attention(qn, kc, vct, bc, ov, *, B, S, q_coloff):
    tq = ATT_TQ
    nq = S // tq
    n_cmp = kc.shape[2]
    n_sel = S // NSA_SEL_LEN
    return pl.pallas_call(
        _cmp_body, grid=(B, NSA_GROUPS, nq),
        in_specs=[pl.BlockSpec((tq, NSA_REP * LANES), lambda b, g, i: (b * nq + i, q_coloff + g // 2)),
                  pl.BlockSpec((1, 1, n_cmp, LANES), lambda b, g, i: (b, g, 0, 0)),
                  pl.BlockSpec((1, 1, HEAD_DIM, n_cmp), lambda b, g, i: (b, g, 0, 0)),
                  pl.BlockSpec((1,) + bc.shape[1:], lambda b, g, i: (g, 0, 0)),
                  pl.BlockSpec(ov.shape, lambda b, g, i: (0, 0))],
        out_specs=[pl.BlockSpec((1, NSA_REP * HEAD_DIM, tq), lambda b, g, i: (b, g, i)),
                   pl.BlockSpec((1, 1, n_sel, tq), lambda b, g, i: (b, g, 0, i))],
        out_shape=[jax.ShapeDtypeStruct((B, D_MODEL, S), F32), jax.ShapeDtypeStruct((B, NSA_GROUPS, n_sel, S), F32)],
        scratch_shapes=[pltpu.VMEM((n_cmp, NSA_REP * tq), F32)],
        compiler_params=_cparams("arbitrary", "arbitrary", "arbitrary"), name="nsa_cmp_topk")(qn, kc, vct, bc, ov)


def _outproj_body(*refs, n_o):
    it = iter(refs)
    o_refs = [next(it) for _ in range(n_o)]
    gt_ref = next(it) if n_o > 1 else None
    w_ref, x_ref, g_ref, out_ref = (next(it) for _ in range(4))
    if n_o == 1:
        ot = o_refs[0][0]
    else:
        gts = gt_ref[0]
        parts = []
        for hd in range(N_HEADS):
            rows = slice(hd * HEAD_DIM, (hd + 1) * HEAD_DIM)
            acc = gts[hd:hd + 1, :] * o_refs[0][0, rows, :]
            for b in range(1, n_o):
                acc = acc + gts[b * N_HEADS + hd:b * N_HEADS + hd + 1, :] * o_refs[b][0, rows, :]
            parts.append(acc.astype(BF))
        ot = jnp.concatenate(parts, axis=0)
    y = _dot_tn(ot, w_ref[...])
    out_ref[...] = x_ref[...] + _rms(y, g_ref[...])


def _outproj(o_list, gt, w, x2d, g, *, B, S, tm=512):
    M, D = x2d.shape
    ns = S // tm
    ins = list(o_list)
    in_specs = [pl.BlockSpec((1, D, tm), lambda b, s: (b, 0, s)) for _ in o_list]
    if gt is not None:
        ins.append(gt)
        in_specs.append(pl.BlockSpec((1, gt.shape[1], tm), lambda b, s: (b, 0, s)))
    ins += [w, x2d, g]
    in_specs += [pl.BlockSpec(w.shape, lambda b, s: (0, 0)),
                 pl.BlockSpec((tm, D), lambda b, s: (b * ns + s, 0)),
                 pl.BlockSpec(g.shape, lambda b, s: (0, 0))]
    return pl.pallas_call(
        functools.partial(_outproj_body, n_o=len(o_list)), grid=(B, ns), in_specs=in_specs,
        out_specs=pl.BlockSpec((tm, D), lambda b, s: (b * ns + s, 0)),
        out_shape=jax.ShapeDtypeStruct((M, D), F32),
        compiler_params=_cparams("arbitrary", "arbitrary"), name="outproj")(*ins)


def _mlp_body(x_ref, g2_ref, w1_ref, w2_ref, g3_ref, wg_ref, p_ref, wp_ref, out_ref, h_ref, acc_ref):
    f = pl.program_id(1)

    @pl.when(f == 0)
    def _():
        h_ref[...] = _rms(x_ref[...], g2_ref[...]).astype(BF)
        acc_ref[...] = jnp.zeros_like(acc_ref)

    a = jnp.maximum(_dot(h_ref[...], w1_ref[...]), 0.0)
    acc_ref[...] += _dot((a * a).astype(BF), w2_ref[...])

    @pl.when(f == pl.num_programs(1) - 1)
    def _():
        x2 = x_ref[...] + _rms(acc_ref[...], g3_ref[...])
        gate = jax.nn.sigmoid(_dot(x2.astype(BF), wg_ref[...]))
        out_ref[...] = x2 + gate * _dot(p_ref[...].astype(BF), wp_ref[...])


def _mlp_ple(x2d, g2, w1, w2, g3, wg, p2d, wp, *, tm=1024, tf=512):
    M, D = x2d.shape
    FF = w1.shape[1]
    PD = p2d.shape[1]
    const = lambda a: pl.BlockSpec(a.shape, lambda m, f: (0, 0))
    return pl.pallas_call(
        _mlp_body, grid=(M // tm, FF // tf),
        in_specs=[pl.BlockSpec((tm, D), lambda m, f: (m, 0)), const(g2),
                  pl.BlockSpec((D, tf), lambda m, f: (0, f)), pl.BlockSpec((tf, D), lambda m, f: (f, 0)),
                  const(g3), const(wg), pl.BlockSpec((tm, PD), lambda m, f: (m, 0)), const(wp)],
        out_specs=pl.BlockSpec((tm, D), lambda m, f: (m, 0)),
        out_shape=jax.ShapeDtypeStruct((M, D), F32),
        scratch_shapes=[pltpu.VMEM((tm, D), BF), pltpu.VMEM((tm, D), F32)],
        compiler_params=_cparams("arbitrary", "arbitrary"), name="mlp_ple")(x2d, g2, w1, w2, g3, wg, p2d, wp)


def _t5_bucket(dist):
    n = jnp.maximum(dist, 0)
    max_exact = REL_BUCKETS // 2
    nf = jnp.maximum(n, 1).astype(F32)
    large = max_exact + (jnp.log(nf / max_exact) / math.log(REL_MAX_DIST / max_exact)
                         * (REL_BUCKETS - max_exact)).astype(jnp.int32)
    large = jnp.minimum(large, REL_BUCKETS - 1)
    return jnp.where(n < max_exact, n, large)


def _bias_by_distance(rel_bias):
    tb = rel_bias[_t5_bucket(jnp.arange(REL_TABLE_LEN))].astype(F32)
    return ((tb - rel_bias[REL_BUCKETS - 1].astype(F32)[None, :]) * LOG2E).T


def _toeplitz(fn, rows, cols, off, row_stride=1):
    n = row_stride * rows + cols
    d = jnp.arange(n)
    d = jnp.where(d < cols, d, d - n)
    v = fn(d + off)
    flat = jnp.tile(v, (1,) * (v.ndim - 1) + (rows,))[..., :rows * (n - row_stride)]
    return flat.reshape(v.shape[:-1] + (rows, n - row_stride))[..., :cols]


def _bias_fn(tbs, hi_limit=None):
    def fn(d):
        val = jnp.zeros((1,) + d.shape, F32) if tbs is None else tbs[:, jnp.clip(d, 0, REL_TABLE_LEN - 1)]
        bad = d < 0
        if hi_limit is not None:
            bad = bad | (d >= hi_limit)
        return jnp.where(bad[None, :], MASK_VALUE, val)
    return fn


def _near_tables(tbs, tq, offsets, hi_limit=None):
    return jnp.stack([_toeplitz(_bias_fn(tbs, hi_limit), ATT_TK, tq, -a * ATT_TK) for a in offsets], axis=1)


def _stack_heads(t, n_grp, n_stack):
    h, nt, r, c = t.shape
    return t.reshape(n_grp, n_stack, nt, r, c).transpose(0, 2, 3, 1, 4).reshape(n_grp, nt, r, n_stack * c)


def _cmp_band_table(tbs):
    tq = ATT_TQ
    band = 2 * (tq // NSA_CMP_STRIDE)
    return _toeplitz(_bias_fn(tbs), band, tq, tq - (NSA_CMP_LEN - 1), row_stride=NSA_CMP_STRIDE)


def _nsa_q_perm():
    cols = []
    for p in range(NSA_GROUPS // 2):
        for r in range(NSA_REP):
            for g in (2 * p, 2 * p + 1):
                h = g * NSA_REP + r
                cols.extend(range(h * HEAD_DIM, (h + 1) * HEAD_DIM))
    return jnp.asarray(cols, jnp.int32)


def _row(v):
    return v.reshape(1, -1).astype(F32)


def kernel(x, p, rel_bias, norm_g, mlp_w1, mlp_w2, ple_w, ple_gate_w, da_w_in, da_lambda, da_subln, da_w_out,
           nsa_w_in, nsa_cmp_pe, nsa_cmp_w1, nsa_cmp_w2, nsa_w_out, fox_w_in, fox_b_f, fox_w_out):
    B, S, D = x.shape
    depth = p.shape[0]
    scale = HEAD_DIM ** -0.5 * LOG2E
    tbs = _bias_by_distance(rel_bias)
    da_near = (-1, 0, 1)
    da_bias = _stack_heads(_near_tables(tbs, DA_TQ, da_near), DA_HEADS, 2)
    x2d = x.reshape(B * S, D)
    ia = ib = ic = 0
    for i in range(depth):
        g = norm_g[i]
        kind = i % N_MIXERS
        if kind == 0:
            lam_init = 0.8 - 0.6 * math.exp(-0.3 * i)
            w_in = da_w_in[ia]
            wn = jnp.concatenate([w_in[:, :D] * scale, w_in[:, D:2 * D]], axis=1).astype(BF)
            wt = w_in[:, 2 * D:].T.astype(BF)
            qk, vt = _inproj(x2d, _row(g[0]), wn, jnp.zeros((1, 2 * D), F32), wt, B=B, S=S)
            ot = _attention(qk, qk, vt, da_bias, B=B, S=S, n_prog=DA_HEADS, q_spec=(LANES, lambda h: h),
                            k_blk=lambda h: DA_HEADS + h, v_blk=lambda h: h, bias_blk=lambda h: h, nq_stack=2,
                            vdim=2 * HEAD_DIM, mode="causal", out_rows=2 * HEAD_DIM, out_dtype=BF, q_mask="da",
                            tq=DA_TQ, near=[(a, n) for n, a in enumerate(da_near)],
                            final="da", lam=da_lambda[ia].astype(F32), subg=da_subln[ia].reshape(-1, 1).astype(F32),
                            lam_init=lam_init)
            x2d = _outproj([ot], None, da_w_out[ia].astype(BF), x2d, _row(g[1]), B=B, S=S)
            ia += 1
        elif kind == 1:
            w_in = nsa_w_in[ib]
            kvd = NSA_GROUPS * HEAD_DIM
            wq = w_in[:, :D][:, _nsa_q_perm()] * scale
            kv = [w_in[:, D + a * kvd:D + (a + 1) * kvd] for a in range(6)]
            wn = jnp.concatenate([wq, kv[2], kv[4]], axis=1).astype(BF)
            wt = jnp.concatenate([kv[3], kv[5]], axis=1).T.astype(BF)
            wf32 = jnp.concatenate([kv[0], kv[1]], axis=1).astype(BF)
            wgate = w_in[:, D + 6 * kvd:].reshape(D, N_HEADS, 3).transpose(2, 1, 0).reshape(3 * N_HEADS, D).astype(BF)
            qn, vt, cmp_in, gates_t = _inproj(x2d, _row(g[0]), wn, jnp.zeros((1, wn.shape[1]), F32), wt, B=B, S=S,
                                              wf32=wf32, wg=wgate)
            n_chunk = S // NSA_CMP_STRIDE
            cm = cmp_in.reshape(B, n_chunk, NSA_CMP_STRIDE, 2, NSA_GROUPS, HEAD_DIM).transpose(3, 0, 1, 4, 2, 5)
            cm = cm.reshape(2, B, n_chunk, NSA_GROUPS, NSA_CMP_STRIDE * HEAD_DIM)
            nxt = jnp.concatenate([cm[:, :, 1:], jnp.zeros_like(cm[:, :, :1])], axis=2)
            xblk = jnp.concatenate([cm, nxt], axis=-1).reshape(2, B * n_chunk * NSA_GROUPS, NSA_CMP_LEN * HEAD_DIM)
            pe = nsa_cmp_pe[ib].reshape(2, 1, NSA_CMP_LEN * HEAD_DIM).astype(F32)
            kvc = _compress(xblk, pe, nsa_cmp_w1[ib].astype(BF), nsa_cmp_w2[ib].astype(BF))
            kvc = kvc.reshape(2, B, n_chunk, NSA_GROUPS, HEAD_DIM).transpose(0, 1, 3, 2, 4)
            kc = jnp.concatenate([kvc[0], kvc[0]], axis=-1).astype(BF)
            vct = kvc[1].transpose(0, 1, 3, 2).astype(BF)
            jj = jnp.arange(S // NSA_SEL_LEN)[:, None]
            nn = jnp.arange(n_chunk)[None, :]
            n_cmp_blocks = (S - NSA_CMP_LEN) // NSA_CMP_STRIDE + 1
            ov = ((nn * NSA_CMP_STRIDE < (jj + 1) * NSA_SEL_LEN) & (nn * NSA_CMP_STRIDE + NSA_CMP_LEN - 1 >= jj * NSA_SEL_LEN)
                  & (nn < n_cmp_blocks)).astype(BF)
            cmp_band = _stack_heads(_cmp_band_table(tbs)[:, None], NSA_GROUPS, NSA_REP)[:, 0]
            oc_t, selb = _cmp_attention(qn, kc, vct, cmp_band, ov, B=B, S=S, q_coloff=0)
            nsa_kw = dict(B=B, S=S, n_prog=NSA_GROUPS, q_spec=(NSA_REP * LANES, lambda h: h // 2),
                          bias_blk=lambda h: h, nq_stack=NSA_REP, vdim=HEAD_DIM, out_rows=NSA_REP * HEAD_DIM,
                          out_dtype=F32, q_mask="group_parity", tq=ATT_TQ)
            kblk0 = D // LANES
            sel_bias = _stack_heads(_near_tables(tbs, ATT_TQ, (-1, 0)), NSA_GROUPS, NSA_REP)
            win_bias = _stack_heads(_near_tables(tbs, ATT_TQ, (-2, -1, 0), hi_limit=NSA_WINDOW), NSA_GROUPS, NSA_REP)
            os_t = _attention(qn, qn, vt, sel_bias, mode="causal", near=[(-1, 0), (0, 1)],
                              k_blk=lambda h: kblk0 + h // 2, v_blk=lambda h: h, selb=selb, **nsa_kw)
            ow_t = _attention(qn, qn, vt, win_bias, mode="window", near=[(-2, 0), (-1, 1), (0, 2)],
                              k_blk=lambda h: kblk0 + 2 + h // 2, v_blk=lambda h: NSA_GROUPS + h, **nsa_kw)
            x2d = _outproj([oc_t, os_t, ow_t], gates_t, nsa_w_out[ib].astype(BF), x2d, _row(g[1]), B=B, S=S)
            ib += 1
        else:
            w_in = fox_w_in[ic]
            z64 = jnp.zeros((D, N_HEADS, HEAD_DIM), F32)
            wq = jnp.concatenate([(w_in[:, :D] * scale).reshape(D, N_HEADS, HEAD_DIM), z64], axis=2).reshape(D, 2 * D)
            wk = jnp.concatenate([w_in[:, D:2 * D].reshape(D, N_HEADS, HEAD_DIM), z64], axis=2).reshape(D, 2 * D)
            wn = jnp.concatenate([wq, wk], axis=1).astype(BF)
            aug = jnp.zeros((N_HEADS, LANES), F32).at[:, HEAD_DIM:HEAD_DIM + 3].set(1.0).reshape(1, 2 * D)
            bn = jnp.concatenate([aug, jnp.zeros((1, 2 * D), F32)], axis=1)
            wf = w_in[:, 3 * D:]
            wf3 = jnp.concatenate([wf, wf, wf, jnp.zeros((D, LANES - 3 * N_HEADS), F32)], axis=1).astype(BF)
            bf3 = jnp.concatenate([fox_b_f[ic]] * 3 + [jnp.zeros((LANES - 3 * N_HEADS,), F32)]).reshape(1, LANES).astype(F32)
            hh = jnp.arange(N_HEADS)
            pm = jnp.zeros((LANES, 2 * D), F32)
            for a in range(3):
                pm = pm.at[a * N_HEADS + hh, hh * LANES + HEAD_DIM + a].set(-1.0)
            qk, vt = _inproj(x2d, _row(g[0]), wn, bn, w_in[:, 2 * D:3 * D].T.astype(BF), B=B, S=S,
                             fox_extra=[wf3, bf3, pm.astype(BF)])
            fox_near = tuple(range(FOX_TQ // ATT_TK))
            mask_t = _near_tables(None, FOX_TQ, fox_near)
            ot = _attention(qk, qk, vt, mask_t, B=B, S=S, n_prog=N_HEADS, q_spec=(LANES, lambda h: h),
                            k_blk=lambda h: N_HEADS + h, v_blk=lambda h: h, bias_blk=lambda h: 0, nq_stack=1,
                            vdim=HEAD_DIM, mode="causal", out_rows=HEAD_DIM, out_dtype=BF, q_mask="none",
                            tq=FOX_TQ, near=[(a, a) for a in fox_near])
            x2d = _outproj([ot], None, fox_w_out[ic].astype(BF), x2d, _row(g[1]), B=B, S=S)
            ic += 1
        x2d = _mlp_ple(x2d, _row(g[2]), mlp_w1[i].astype(BF), mlp_w2[i].astype(BF), _row(g[3]),
                       ple_gate_w[i].astype(BF), p[i].reshape(B * S, -1), ple_w[i].astype(BF))
    return x2d.reshape(B, S, D)
```

```python
import functools
import math

import jax
import jax.numpy as jnp
from jax import lax
from jax.experimental import pallas as pl
from jax.experimental.pallas import tpu as pltpu

BF = jnp.bfloat16
F32 = jnp.float32

D_MODEL = 1024
HEAD_DIM = 64
LANES = 128
NORM_EPS = 1e-6
MASK_VALUE = -1e30
REL_BUCKETS = 32
REL_MAX_DIST = 128
REL_TABLE_LEN = 512
N_HEADS = D_MODEL // HEAD_DIM
DA_HEADS = N_HEADS // 2
NSA_GROUPS = 4
NSA_REP = N_HEADS // NSA_GROUPS
NSA_CMP_LEN = 32
NSA_CMP_STRIDE = 16
NSA_SEL_LEN = 64
NSA_TOP_N = 16
NSA_WINDOW = 512
NSA_FORCE_SCORE = 1e4
N_MIXERS = 3

ATT_TQ = 256
DA_TQ = 512
FOX_TQ = 1024
ATT_TK = 256
VMEM_LIMIT = 56 * 1024 * 1024
ONES_ROWS = 16
LOG2E = math.log2(math.e)


def _cparams(*sem):
    return pltpu.CompilerParams(dimension_semantics=sem, vmem_limit_bytes=VMEM_LIMIT)


def _rms(x, g):
    return x * lax.rsqrt(jnp.mean(x * x, axis=-1, keepdims=True) + NORM_EPS) * g


def _dot(a, b):
    return jnp.dot(a, b, preferred_element_type=F32)


def _dot_nt(a, b):
    return lax.dot_general(a, b, (((1,), (1,)), ((), ())), preferred_element_type=F32)


def _dot_tn(a, b):
    return lax.dot_general(a, b, (((0,), (0,)), ((), ())), preferred_element_type=F32)


def _inproj_body(*refs, n_f32, n_gate, fox, tm, col_chunk):
    it = iter(refs)
    x_ref, g_ref, wn_ref, bn_ref, wt_ref = (next(it) for _ in range(5))
    wf32_ref = next(it) if n_f32 else None
    wg_ref = next(it) if n_gate else None
    if fox:
        wf_ref, bf_ref, pm_ref = next(it), next(it), next(it)
    on_ref, ot_ref = next(it), next(it)
    of32_ref = next(it) if n_f32 else None
    og_ref = next(it) if n_gate else None
    carry_ref = next(it) if fox else None

    h = _rms(x_ref[...], g_ref[...]).astype(BF)
    n_nat = on_ref.shape[1]
    k_off = n_nat // 2
    if fox:
        @pl.when(pl.program_id(1) == 0)
        def _():
            carry_ref[...] = jnp.zeros_like(carry_ref)

        lf = _dot(h, wf_ref[...]) + bf_ref[...]
        lane = lax.broadcasted_iota(jnp.int32, lf.shape, 1)
        row = lax.broadcasted_iota(jnp.int32, lf.shape, 0)
        ls = jnp.minimum(lf, 0.0) - jnp.log1p(jnp.exp(-jnp.abs(lf)))
        c = jnp.where(lane < 3 * N_HEADS, ls, 0.0)
        k = 1
        while k < tm:
            c = c + jnp.where(row >= k, pltpu.roll(c, k, axis=0), 0.0)
            k *= 2
        c = c + carry_ref[...]
        carry_ref[...] = c[tm - 1:tm, :]
        c = c * LOG2E
        hi = c.astype(BF).astype(F32)
        r1 = c - hi
        mid = r1.astype(BF).astype(F32)
        lo = (r1 - mid).astype(BF).astype(F32)
        c3 = jnp.where(lane < N_HEADS, hi, jnp.where(lane < 2 * N_HEADS, mid, lo)).astype(BF)
    if fox:
        lane_b = lax.broadcasted_iota(jnp.int32, (tm, LANES), 1)
        d_c = wn_ref.shape[1] // 2
        for part in range(2):
            yc = _dot(h, wn_ref[:, part * d_c:(part + 1) * d_c])
            for oc in range(0, k_off, col_chunk):
                pieces = []
                for hh in range(oc // LANES, (oc + col_chunk) // LANES):
                    blk = yc[:, (hh // 2) * LANES:(hh // 2 + 1) * LANES]
                    keep = (lane_b < HEAD_DIM) if hh % 2 == 0 else (lane_b >= HEAD_DIM)
                    pieces.append(jnp.where(keep, blk, 0.0))
                y = jnp.concatenate(pieces, axis=1) + bn_ref[:, part * k_off + oc:part * k_off + oc + col_chunk]
                if part == 1:
                    y = y + _dot(c3, pm_ref[:, oc:oc + col_chunk])
                on_ref[:, part * k_off + oc:part * k_off + oc + col_chunk] = y.astype(on_ref.dtype)
    else:
        for c0 in range(0, n_nat, col_chunk):
            y = _dot(h, wn_ref[:, c0:c0 + col_chunk]) + bn_ref[:, c0:c0 + col_chunk]
            on_ref[:, c0:c0 + col_chunk] = y.astype(on_ref.dtype)
    ot_ref[0] = _dot_nt(wt_ref[...], h).astype(ot_ref.dtype)
    if n_f32:
        of32_ref[...] = _dot(h, wf32_ref[...])
    if n_gate:
        og_ref[0] = jax.nn.sigmoid(_dot_nt(wg_ref[...], h))


def _inproj(x2d, g, wn, bn, wt, *, B, S, tm=512, wf32=None, wg=None, fox_extra=None):
    M, D = x2d.shape
    ns = S // tm
    n_nat, n_t = bn.shape[1], wt.shape[0]
    full = lambda a: pl.BlockSpec(a.shape, lambda b, s: (0,) * a.ndim)
    ins = [x2d, g, wn, bn, wt]
    in_specs = [pl.BlockSpec((tm, D), lambda b, s: (b * ns + s, 0)), full(g), full(wn), full(bn), full(wt)]
    out_shape = [jax.ShapeDtypeStruct((M, n_nat), BF), jax.ShapeDtypeStruct((B, n_t, S), BF)]
    out_specs = [pl.BlockSpec((tm, n_nat), lambda b, s: (b * ns + s, 0)),
                 pl.BlockSpec((1, n_t, tm), lambda b, s: (b, 0, s))]
    scratch = []
    if wf32 is not None:
        ins.append(wf32)
        in_specs.append(full(wf32))
        out_shape.append(jax.ShapeDtypeStruct((M, wf32.shape[1]), F32))
        out_specs.append(pl.BlockSpec((tm, wf32.shape[1]), lambda b, s: (b * ns + s, 0)))
    if wg is not None:
        ins.append(wg)
        in_specs.append(full(wg))
        out_shape.append(jax.ShapeDtypeStruct((B, wg.shape[0], S), F32))
        out_specs.append(pl.BlockSpec((1, wg.shape[0], tm), lambda b, s: (b, 0, s)))
    if fox_extra is not None:
        for a in fox_extra:
            ins.append(a)
            in_specs.append(full(a))
        scratch.append(pltpu.VMEM((1, LANES), F32))
    body = functools.partial(_inproj_body, n_f32=wf32 is not None, n_gate=wg is not None,
                             fox=fox_extra is not None, tm=tm, col_chunk=1024 if n_nat % 1024 == 0 else 512)
    return pl.pallas_call(
        body, grid=(B, ns), in_specs=in_specs, out_specs=out_specs, out_shape=out_shape,
        scratch_shapes=scratch, compiler_params=_cparams("arbitrary", "arbitrary"), name="inproj")(*ins)


def _attn_body(*refs, tq, nq_stack, vdim, mode, use_sel, final, lam_init, q_mask, near):
    it = iter(refs)
    q_ref, k_ref, vt_ref, bias_ref = (next(it) for _ in range(4))
    selb_ref = next(it) if use_sel else None
    if final == "da":
        lam_ref, subg_ref = next(it), next(it)
    o_ref = next(it)
    qm_ref, m_ref, acc_ref, s_scr, mt_scr = (next(it) for _ in range(5))
    tk = ATT_TK
    lw = nq_stack * tq
    i = pl.program_id(2)

    qt = q_ref[...]
    lane = lax.broadcasted_iota(jnp.int32, (tq, LANES), 1)
    for r in range(nq_stack):
        blk = qt[:, 0:LANES] if q_mask == "da" else qt[:, r * LANES:(r + 1) * LANES]
        if q_mask == "da":
            keep = (lane < HEAD_DIM) if r == 0 else (lane >= HEAD_DIM)
        elif q_mask == "group_parity":
            keep = (lane // HEAD_DIM) == (pl.program_id(1) % 2)
        else:
            keep = None
        if keep is not None:
            blk = jnp.where(keep, blk, jnp.zeros_like(blk))
        qm_ref[r * tq:(r + 1) * tq, :] = blk
    m_ref[...] = jnp.full_like(m_ref, MASK_VALUE)
    acc_ref[...] = jnp.zeros_like(acc_ref)

    def stage_qk(group, r):
        rows = len(group) * tk
        ks = [k_ref[pl.ds(pl.multiple_of(kt * tk, tk), tk), :] for kt, _, _ in group]
        s = _dot_nt(ks[0] if len(ks) == 1 else jnp.concatenate(ks, axis=0), qm_ref[...])
        if use_sel:
            slabs = []
            for t, (kt, _, _) in enumerate(group):
                sb8 = selb_ref[0, 0, pl.ds(pl.multiple_of((kt // 2) * 8, 8), 8), :]
                sb4 = jnp.where(kt % 2 == 0, sb8[0:4, :], sb8[4:8, :])
                sbt = jnp.concatenate([sb4] * nq_stack, axis=1)
                for j in range(tk // NSA_SEL_LEN):
                    r0 = t * tk + j * NSA_SEL_LEN
                    slabs.append(s[r0:r0 + NSA_SEL_LEN, :] + sbt[j:j + 1, :])
            s = jnp.concatenate(slabs, axis=0)
        s_scr[r, 0:rows, :] = s
        mt_scr[r] = jnp.max(s, axis=0, keepdims=True)

    def stage_softmax_pv(group, r, zero_first=None):
        rows = len(group) * tk
        if all(bidx is None and fm is None for _, bidx, fm in group):
            s = s_scr[r, 0:rows, :]
            mt = mt_scr[r]
        else:
            parts = []
            for t, (_, bidx, fm) in enumerate(group):
                st = s_scr[r, t * tk:(t + 1) * tk, :]
                if bidx is not None:
                    st = st + jnp.concatenate([bias_ref[q, bidx] for q in range(nq_stack)], axis=1)
                if fm is not None:
                    st = st + jnp.where(fm, MASK_VALUE, 0.0)
                parts.append(st)
            s = parts[0] if len(parts) == 1 else jnp.concatenate(parts, axis=0)
            mt = jnp.max(s, axis=0, keepdims=True)
        m_old = m_ref[...]
        m_new = jnp.maximum(m_old, mt)
        alpha = jnp.exp2(m_old - m_new)
        p = jnp.exp2(s - m_new).astype(BF)
        m_ref[...] = m_new
        vts = [vt_ref[0, :, pl.ds(pl.multiple_of(kt * tk, tk), tk)] for kt, _, _ in group]
        lhs = jnp.concatenate([vts[0] if len(vts) == 1 else jnp.concatenate(vts, axis=1),
                               jnp.ones((ONES_ROWS, rows), BF)], axis=0)
        if zero_first is not None:
            col = lax.broadcasted_iota(jnp.int32, lhs.shape, 1)
            lhs = jnp.where(zero_first & (col < tk), jnp.zeros_like(lhs), lhs)
        acc_ref[...] = alpha * acc_ref[...] + _dot(lhs, p)

    def trip(r, cur, nxt, zero_first=None):
        if not isinstance(r, int):
            for val in (0, 1):
                @pl.when(r == val)
                def _(val=val):
                    trip(val, cur, nxt, zero_first)
            return
        if nxt:
            stage_qk(nxt, 1 - r)
        stage_softmax_pv(cur, r, zero_first)

    base = i * (tq // tk)
    near_tiles = [(jnp.maximum(base + a, 0), bidx, (base + a < 0) if a < 0 else None) for a, bidx in near]
    near_groups = [near_tiles[n:n + 2] for n in range(0, len(near_tiles), 2)]
    if mode == "causal":
        n_far = jnp.maximum(base + near[0][0], 0)
        delta = n_far % 2
        trips = (n_far + delta) // 2
        far_group = lambda u: [(jnp.maximum(2 * u + t - delta, 0), None, None) for t in range(2)]
        first = [(jnp.where(trips > 0, far_group(0)[t][0], near_groups[0][t][0]), None, None) for t in range(2)]
        stage_qk(first, 0)

        def body(u, c):
            trip(u % 2, far_group(u), far_group(u + 1), zero_first=(u == 0) & (delta == 1))
            return c
        lax.fori_loop(0, trips, body, 0)
    else:
        trips = 0
        stage_qk(near_groups[0], 0)
    for e, grp in enumerate(near_groups):
        trip((trips + e) % 2, grp, near_groups[e + 1] if e + 1 < len(near_groups) else None)

    acc = acc_ref[0:vdim, :]
    l = acc_ref[vdim:vdim + 1, :]
    if final == "da":
        lamv = lam_ref[...]
        lam = (jnp.exp(jnp.sum(lamv[0:1] * lamv[1:2], axis=1, keepdims=True))
               - jnp.exp(jnp.sum(lamv[2:3] * lamv[3:4], axis=1, keepdims=True)) + lam_init)
        o = acc[:, :tq] / l[:, :tq] - lam * (acc[:, tq:] / l[:, tq:])
        o = o * lax.rsqrt(jnp.mean(o * o, axis=0, keepdims=True) + NORM_EPS) * subg_ref[...] * (1.0 - lam_init)
        o_ref[0] = o.astype(o_ref.dtype)
    else:
        for r in range(nq_stack):
            o_ref[0, r * vdim:(r + 1) * vdim, :] = (acc[:, r * tq:(r + 1) * tq] / l[:, r * tq:(r + 1) * tq]).astype(o_ref.dtype)


def _attention(q_arr, k_arr, vt, bias, *, B, S, n_prog, q_spec, k_blk, v_blk, bias_blk, nq_stack, vdim, mode,
               out_rows, out_dtype, q_mask, tq, near, selb=None, final="plain", lam=None, subg=None, lam_init=0.0):
    tk = ATT_TK
    nq = S // tq
    lw = nq_stack * tq
    q_w, q_blk = q_spec
    ins = [q_arr, k_arr, vt, bias]
    in_specs = [
        pl.BlockSpec((tq, q_w), lambda b, h, i: (b * nq + i, q_blk(h))),
        pl.BlockSpec((S, LANES), lambda b, h, i: (b, k_blk(h))),
        pl.BlockSpec((1, vdim, S), lambda b, h, i: (b, v_blk(h), 0)),
        pl.BlockSpec((nq_stack,) + bias.shape[1:], lambda b, h, i: (bias_blk(h), 0, 0, 0)),
    ]
    if selb is not None:
        ins.append(selb)
        in_specs.append(pl.BlockSpec((1, 1, selb.shape[2], tq), lambda b, h, i: (b, h, 0, i)))
    if final == "da":
        ins += [lam, subg]
        in_specs += [pl.BlockSpec(lam.shape, lambda b, h, i: (0, 0)), pl.BlockSpec(subg.shape, lambda b, h, i: (0, 0))]
    body = functools.partial(_attn_body, tq=tq, nq_stack=nq_stack, vdim=vdim, mode=mode, use_sel=selb is not None,
                             final=final, lam_init=lam_init, q_mask=q_mask, near=near)
    return pl.pallas_call(
        body, grid=(B, n_prog, nq), in_specs=in_specs,
        out_specs=pl.BlockSpec((1, out_rows, tq), lambda b, h, i: (b, h, i)),
        out_shape=jax.ShapeDtypeStruct((B, out_rows * n_prog, S), out_dtype),
        scratch_shapes=[pltpu.VMEM((lw, LANES), BF), pltpu.VMEM((1, lw), F32), pltpu.VMEM((vdim + ONES_ROWS, lw), F32),
                        pltpu.VMEM((2, 2 * tk, lw), F32), pltpu.VMEM((2, 1, lw), F32)],
        compiler_params=_cparams("arbitrary", "arbitrary", "arbitrary"), name="attn_" + mode + "_" + final)(*ins)


def _compress_body(x_ref, pe_ref, w1_ref, w2_ref, o_ref):
    x = x_ref[0, 0, 0]
    n, half = x.shape
    pe = pe_ref[0]
    first = _dot((x + pe[:, :half]).astype(BF), w1_ref[0, :half, :])
    second = _dot((x + pe[:, half:]).astype(BF), w1_ref[0, half:, :])
    hcur = first + pltpu.roll(second, n - 1, axis=0)
    hcur = 0.5 * hcur * (1.0 + jnp.tanh(math.sqrt(2.0 / math.pi) * (hcur + 0.044715 * (hcur * hcur * hcur))))
    o_ref[0, 0, 0] = _dot(hcur.astype(BF), w2_ref[0])


def _compress(chunks, pe, w1, w2):
    _, B, G, n, K = chunks.shape
    return pl.pallas_call(
        _compress_body, grid=(2, B, G),
        in_specs=[pl.BlockSpec((1, 1, 1, n, K), lambda a, b, g: (a, b, g, 0, 0)),
                  pl.BlockSpec((1, 1, 2 * K), lambda a, b, g: (a, 0, 0)),
                  pl.BlockSpec((1,) + w1.shape[1:], lambda a, b, g: (a, 0, 0)),
                  pl.BlockSpec((1,) + w2.shape[1:], lambda a, b, g: (a, 0, 0))],
        out_specs=pl.BlockSpec((1, 1, 1, n, HEAD_DIM), lambda a, b, g: (a, b, g, 0, 0)),
        out_shape=jax.ShapeDtypeStruct((2, B, G, n, HEAD_DIM), F32),
        compiler_params=_cparams("arbitrary", "arbitrary", "arbitrary"), name="nsa_compress")(chunks, pe, w1, w2)


def _cmp_body(q_ref, kc_ref, vct_ref, bc_ref, ov_ref, o_ref, selb_ref, s_ref):
    tq = ATT_TQ
    lw = NSA_REP * tq
    i = pl.program_id(2)
    n_cmp = kc_ref.shape[2]
    n_sel = selb_ref.shape[2]
    band = bc_ref.shape[1]
    bc = jnp.concatenate([bc_ref[r] for r in range(NSA_REP)], axis=1)
    qt = q_ref[...]
    lane = lax.broadcasted_iota(jnp.int32, (tq, LANES), 1)
    keep = (lane // HEAD_DIM) == (pl.program_id(1) % 2)
    qm = jnp.concatenate(
        [jnp.where(keep, qt[:, r * LANES:(r + 1) * LANES], jnp.zeros((tq, LANES), BF)) for r in range(NSA_REP)], axis=0)
    s = _dot_nt(kc_ref[0, 0], qm)
    row = lax.broadcasted_iota(jnp.int32, s.shape, 0)
    s_ref[...] = jnp.where(row < (band // 2) * (i + 1), s, MASK_VALUE)

    @pl.when(i == 0)
    def _():
        s_ref[0:band // 2, :] = s_ref[0:band // 2, :] + bc[band // 2:, :]

    @pl.when(i > 0)
    def _():
        r0 = pl.multiple_of((band // 2) * (i - 1), band // 2)
        s_ref[pl.ds(r0, band), :] = s_ref[pl.ds(r0, band), :] + bc

    s = s_ref[...]
    m = jnp.max(s, axis=0, keepdims=True)
    e = jnp.exp2(s - m)
    l = jnp.sum(e, axis=0, keepdims=True)
    inv = jnp.where(m > 0.5 * MASK_VALUE, 1.0 / l, 0.0)
    p = e * inv
    oc = _dot(vct_ref[0, 0], p.astype(BF))
    for r in range(NSA_REP):
        o_ref[0, r * HEAD_DIM:(r + 1) * HEAD_DIM, :] = oc[:, r * tq:(r + 1) * tq]
    ps = p[:, 0:tq]
    for r in range(1, NSA_REP):
        ps = ps + p[:, r * tq:(r + 1) * tq]
    p_hi = ps.astype(BF)
    r1 = ps - p_hi.astype(F32)
    p_mid = r1.astype(BF)
    p_lo = (r1 - p_mid.astype(F32)).astype(BF)
    ov = ov_ref[...]
    imp = _dot(ov, p_hi) + _dot(ov, p_mid) + _dot(ov, p_lo)
    j = lax.broadcasted_iota(jnp.int32, (n_sel, tq), 0)
    t = i * tq + lax.broadcasted_iota(jnp.int32, (n_sel, tq), 1)
    cur = t // NSA_SEL_LEN
    forced = (j == 0) | (j == cur) | (j == cur - 1)
    w = jnp.where(j <= cur, jnp.where(forced, NSA_FORCE_SCORE, imp), -1.0)
    jf = j.astype(F32)
    sel = jnp.zeros((n_sel, tq), F32)
    for _ in range(min(NSA_TOP_N, n_sel)):
        mx = jnp.max(w, axis=0, keepdims=True)
        idx = jnp.min(jnp.where(w == mx, jf, float(n_sel)), axis=0, keepdims=True)
        pick = jf == idx
        sel = jnp.where(pick, 1.0, sel)
        w = jnp.where(pick, -2.0, w)
    selb_ref[0, 0] = jnp.where(sel > 0.5, 0.0, MASK_VALUE)


def _cmp_attention(qn, kc, vct, bc, ov, *, B, S, q_coloff):
    tq = ATT_TQ
    nq = S // tq
    n_cmp = kc.shape[2]
    n_sel = S // NSA_SEL_LEN
    return pl.pallas_call(
        _cmp_body, grid=(B, NSA_GROUPS, nq),
        in_specs=[pl.BlockSpec((tq, NSA_REP * LANES), lambda b, g, i: (b * nq + i, q_coloff + g // 2)),
                  pl.BlockSpec((1, 1, n_cmp, LANES), lambda b, g, i: (b, g, 0, 0)),
                  pl.BlockSpec((1, 1, HEAD_DIM, n_cmp), lambda b, g, i: (b, g, 0, 0)),
                  pl.BlockSpec((NSA_REP,) + bc.shape[1:], lambda b, g, i: (g, 0, 0)),
                  pl.BlockSpec(ov.shape, lambda b, g, i: (0, 0))],
        out_specs=[pl.BlockSpec((1, NSA_REP * HEAD_DIM, tq), lambda b, g, i: (b, g, i)),
                   pl.BlockSpec((1, 1, n_sel, tq), lambda b, g, i: (b, g, 0, i))],
        out_shape=[jax.ShapeDtypeStruct((B, D_MODEL, S), F32), jax.ShapeDtypeStruct((B, NSA_GROUPS, n_sel, S), F32)],
        scratch_shapes=[pltpu.VMEM((n_cmp, NSA_REP * tq), F32)],
        compiler_params=_cparams("arbitrary", "arbitrary", "arbitrary"), name="nsa_cmp_topk")(qn, kc, vct, bc, ov)


def _outproj_body(*refs, n_o):
    it = iter(refs)
    o_refs = [next(it) for _ in range(n_o)]
    gt_ref = next(it) if n_o > 1 else None
    w_ref, x_ref, g_ref, out_ref = (next(it) for _ in range(4))
    if n_o == 1:
        ot = o_refs[0][0]
    else:
        gts = gt_ref[0]
        parts = []
        for hd in range(N_HEADS):
            rows = slice(hd * HEAD_DIM, (hd + 1) * HEAD_DIM)
            acc = gts[hd:hd + 1, :] * o_refs[0][0, rows, :]
            for b in range(1, n_o):
                acc = acc + gts[b * N_HEADS + hd:b * N_HEADS + hd + 1, :] * o_refs[b][0, rows, :]
            parts.append(acc.astype(BF))
        ot = jnp.concatenate(parts, axis=0)
    y = _dot_tn(ot, w_ref[...])
    out_ref[...] = x_ref[...] + _rms(y, g_ref[...])


def _outproj(o_list, gt, w, x2d, g, *, B, S, tm=512):
    M, D = x2d.shape
    ns = S // tm
    ins = list(o_list)
    in_specs = [pl.BlockSpec((1, D, tm), lambda b, s: (b, 0, s)) for _ in o_list]
    if gt is not None:
        ins.append(gt)
        in_specs.append(pl.BlockSpec((1, gt.shape[1], tm), lambda b, s: (b, 0, s)))
    ins += [w, x2d, g]
    in_specs += [pl.BlockSpec(w.shape, lambda b, s: (0, 0)),
                 pl.BlockSpec((tm, D), lambda b, s: (b * ns + s, 0)),
                 pl.BlockSpec(g.shape, lambda b, s: (0, 0))]
    return pl.pallas_call(
        functools.partial(_outproj_body, n_o=len(o_list)), grid=(B, ns), in_specs=in_specs,
        out_specs=pl.BlockSpec((tm, D), lambda b, s: (b * ns + s, 0)),
        out_shape=jax.ShapeDtypeStruct((M, D), F32),
        compiler_params=_cparams("arbitrary", "arbitrary"), name="outproj")(*ins)


def _mlp_body(x_ref, g2_ref, w1_ref, w2_ref, g3_ref, wg_ref, p_ref, wp_ref, out_ref, h_ref, acc_ref):
    f = pl.program_id(1)

    @pl.when(f == 0)
    def _():
        h_ref[...] = _rms(x_ref[...], g2_ref[...]).astype(BF)
        acc_ref[...] = jnp.zeros_like(acc_ref)

    a = jnp.maximum(_dot(h_ref[...], w1_ref[...]), 0.0)
    acc_ref[...] += _dot((a * a).astype(BF), w2_ref[...])

    @pl.when(f == pl.num_programs(1) - 1)
    def _():
        x2 = x_ref[...] + _rms(acc_ref[...], g3_ref[...])
        gate = jax.nn.sigmoid(_dot(x2.astype(BF), wg_ref[...]))
        out_ref[...] = x2 + gate * _dot(p_ref[...].astype(BF), wp_ref[...])


def _mlp_ple(x2d, g2, w1, w2, g3, wg, p2d, wp, *, tm=1024, tf=512):
    M, D = x2d.shape
    FF = w1.shape[1]
    PD = p2d.shape[1]
    const = lambda a: pl.BlockSpec(a.shape, lambda m, f: (0, 0))
    return pl.pallas_call(
        _mlp_body, grid=(M // tm, FF // tf),
        in_specs=[pl.BlockSpec((tm, D), lambda m, f: (m, 0)), const(g2),
                  pl.BlockSpec((D, tf), lambda m, f: (0, f)), pl.BlockSpec((tf, D), lambda m, f: (f, 0)),
                  const(g3), const(wg), pl.BlockSpec((tm, PD), lambda m, f: (m, 0)), const(wp)],
        out_specs=pl.BlockSpec((tm, D), lambda m, f: (m, 0)),
        out_shape=jax.ShapeDtypeStruct((M, D), F32),
        scratch_shapes=[pltpu.VMEM((tm, D), BF), pltpu.VMEM((tm, D), F32)],
        compiler_params=_cparams("arbitrary", "arbitrary"), name="mlp_ple")(x2d, g2, w1, w2, g3, wg, p2d, wp)


def _t5_bucket(dist):
    n = jnp.maximum(dist, 0)
    max_exact = REL_BUCKETS // 2
    nf = jnp.maximum(n, 1).astype(F32)
    large = max_exact + (jnp.log(nf / max_exact) / math.log(REL_MAX_DIST / max_exact)
                         * (REL_BUCKETS - max_exact)).astype(jnp.int32)
    large = jnp.minimum(large, REL_BUCKETS - 1)
    return jnp.where(n < max_exact, n, large)


def _bias_by_distance(rel_bias):
    tb = rel_bias[_t5_bucket(jnp.arange(REL_TABLE_LEN))].astype(F32)
    return ((tb - rel_bias[REL_BUCKETS - 1].astype(F32)[None, :]) * LOG2E).T


def _toeplitz(fn, rows, cols, off, row_stride=1):
    n = row_stride * rows + cols
    d = jnp.arange(n)
    d = jnp.where(d < cols, d, d - n)
    v = fn(d + off)
    flat = jnp.tile(v, (1,) * (v.ndim - 1) + (rows,))[..., :rows * (n - row_stride)]
    return flat.reshape(v.shape[:-1] + (rows, n - row_stride))[..., :cols]


def _bias_fn(tbs, hi_limit=None):
    def fn(d):
        val = jnp.zeros((1,) + d.shape, F32) if tbs is None else tbs[:, jnp.clip(d, 0, REL_TABLE_LEN - 1)]
        bad = d < 0
        if hi_limit is not None:
            bad = bad | (d >= hi_limit)
        return jnp.where(bad[None, :], MASK_VALUE, val)
    return fn


def _near_tables(tbs, tq, offsets, hi_limit=None):
    return jnp.stack([_toeplitz(_bias_fn(tbs, hi_limit), ATT_TK, tq, -a * ATT_TK) for a in offsets], axis=1)


def _cmp_band_table(tbs):
    tq = ATT_TQ
    band = 2 * (tq // NSA_CMP_STRIDE)
    return _toeplitz(_bias_fn(tbs), band, tq, tq - (NSA_CMP_LEN - 1), row_stride=NSA_CMP_STRIDE)


def _nsa_q_perm():
    cols = []
    for p in range(NSA_GROUPS // 2):
        for r in range(NSA_REP):
            for g in (2 * p, 2 * p + 1):
                h = g * NSA_REP + r
                cols.extend(range(h * HEAD_DIM, (h + 1) * HEAD_DIM))
    return jnp.asarray(cols, jnp.int32)


def _row(v):
    return v.reshape(1, -1).astype(F32)


def kernel(x, p, rel_bias, norm_g, mlp_w1, mlp_w2, ple_w, ple_gate_w, da_w_in, da_lambda, da_subln, da_w_out,
           nsa_w_in, nsa_cmp_pe, nsa_cmp_w1, nsa_cmp_w2, nsa_w_out, fox_w_in, fox_b_f, fox_w_out):
    B, S, D = x.shape
    depth = p.shape[0]
    scale = HEAD_DIM ** -0.5 * LOG2E
    tbs = _bias_by_distance(rel_bias)
    da_near = tuple(range(-1, DA_TQ // ATT_TK))
    da_bias = _near_tables(tbs, DA_TQ, da_near)
    x2d = x.reshape(B * S, D)
    ia = ib = ic = 0
    for i in range(depth):
        g = norm_g[i]
        kind = i % N_MIXERS
        if kind == 0:
            lam_init = 0.8 - 0.6 * math.exp(-0.3 * i)
            w_in = da_w_in[ia]
            wn = jnp.concatenate([w_in[:, :D] * scale, w_in[:, D:2 * D]], axis=1).astype(BF)
            wt = w_in[:, 2 * D:].T.astype(BF)
            qk, vt = _inproj(x2d, _row(g[0]), wn, jnp.zeros((1, 2 * D), F32), wt, B=B, S=S)
            ot = _attention(qk, qk, vt, da_bias, B=B, S=S, n_prog=DA_HEADS, q_spec=(LANES, lambda h: h),
                            k_blk=lambda h: DA_HEADS + h, v_blk=lambda h: h, bias_blk=lambda h: h, nq_stack=2,
                            vdim=2 * HEAD_DIM, mode="causal", out_rows=2 * HEAD_DIM, out_dtype=BF, q_mask="da",
                            tq=DA_TQ, near=[(da_near[0] - 1, None)] + [(a, n) for n, a in enumerate(da_near)],
                            final="da", lam=da_lambda[ia].astype(F32), subg=da_subln[ia].reshape(-1, 1).astype(F32),
                            lam_init=lam_init)
            x2d = _outproj([ot], None, da_w_out[ia].astype(BF), x2d, _row(g[1]), B=B, S=S)
            ia += 1
        elif kind == 1:
            w_in = nsa_w_in[ib]
            kvd = NSA_GROUPS * HEAD_DIM
            wq = w_in[:, :D][:, _nsa_q_perm()] * scale
            kv = [w_in[:, D + a * kvd:D + (a + 1) * kvd] for a in range(6)]
            wn = jnp.concatenate([wq, kv[2], kv[4]], axis=1).astype(BF)
            wt = jnp.concatenate([kv[3], kv[5]], axis=1).T.astype(BF)
            wf32 = jnp.concatenate([kv[0], kv[1]], axis=1).astype(BF)
            wgate = w_in[:, D + 6 * kvd:].reshape(D, N_HEADS, 3).transpose(2, 1, 0).reshape(3 * N_HEADS, D).astype(BF)
            qn, vt, cmp_in, gates_t = _inproj(x2d, _row(g[0]), wn, jnp.zeros((1, wn.shape[1]), F32), wt, B=B, S=S,
                                              wf32=wf32, wg=wgate)
            n_chunk = S // NSA_CMP_STRIDE
            cm = cmp_in.reshape(B, n_chunk, NSA_CMP_STRIDE, 2, NSA_GROUPS, HEAD_DIM).transpose(3, 0, 4, 1, 2, 5)
            cm = cm.reshape(2, B, NSA_GROUPS, n_chunk, NSA_CMP_STRIDE * HEAD_DIM)
            pe = nsa_cmp_pe[ib].reshape(2, 1, NSA_CMP_LEN * HEAD_DIM).astype(F32)
            kvc = _compress(cm, pe, nsa_cmp_w1[ib].astype(BF), nsa_cmp_w2[ib].astype(BF))
            kc = jnp.concatenate([kvc[0], kvc[0]], axis=-1).astype(BF)
            vct = kvc[1].transpose(0, 1, 3, 2).astype(BF)
            jj = jnp.arange(S // NSA_SEL_LEN)[:, None]
            nn = jnp.arange(n_chunk)[None, :]
            n_cmp_blocks = (S - NSA_CMP_LEN) // NSA_CMP_STRIDE + 1
            ov = ((nn * NSA_CMP_STRIDE < (jj + 1) * NSA_SEL_LEN) & (nn * NSA_CMP_STRIDE + NSA_CMP_LEN - 1 >= jj * NSA_SEL_LEN)
                  & (nn < n_cmp_blocks)).astype(BF)
            cmp_band = _cmp_band_table(tbs)
            oc_t, selb = _cmp_attention(qn, kc, vct, cmp_band, ov, B=B, S=S, q_coloff=0)
            nsa_kw = dict(B=B, S=S, n_prog=NSA_GROUPS, q_spec=(NSA_REP * LANES, lambda h: h // 2),
                          bias_blk=lambda h: h, nq_stack=NSA_REP, vdim=HEAD_DIM, out_rows=NSA_REP * HEAD_DIM,
                          out_dtype=F32, q_mask="group_parity", tq=ATT_TQ)
            kblk0 = D // LANES
            sel_bias = _near_tables(tbs, ATT_TQ, (-1, 0))
            win_bias = _near_tables(tbs, ATT_TQ, (-2, -1, 0), hi_limit=NSA_WINDOW)
            os_t = _attention(qn, qn, vt, sel_bias, mode="causal", near=[(-1, 0), (0, 1)],
                              k_blk=lambda h: kblk0 + h // 2, v_blk=lambda h: h, selb=selb, **nsa_kw)
            ow_t = _attention(qn, qn, vt, win_bias, mode="window", near=[(-2, 0), (-1, 1), (0, 2)],
                              k_blk=lambda h: kblk0 + 2 + h // 2, v_blk=lambda h: NSA_GROUPS + h, **nsa_kw)
            x2d = _outproj([oc_t, os_t, ow_t], gates_t, nsa_w_out[ib].astype(BF), x2d, _row(g[1]), B=B, S=S)
            ib += 1
        else:
            w_in = fox_w_in[ic]
            wn = jnp.concatenate([w_in[:, :D] * scale, w_in[:, D:2 * D]], axis=1).astype(BF)
            hh = jnp.arange(N_HEADS)
            aug0 = hh * LANES + jnp.where(hh % 2 == 0, HEAD_DIM, 0)
            sel = jnp.arange(2 * D)[None, :] - aug0[:, None]
            aug = jnp.any((sel >= 0) & (sel < 3), axis=0).astype(F32).reshape(1, 2 * D)
            bn = jnp.concatenate([aug, jnp.zeros((1, 2 * D), F32)], axis=1)
            wf = w_in[:, 3 * D:]
            wf3 = jnp.concatenate([wf, wf, wf, jnp.zeros((D, LANES - 3 * N_HEADS), F32)], axis=1).astype(BF)
            bf3 = jnp.concatenate([fox_b_f[ic]] * 3 + [jnp.zeros((LANES - 3 * N_HEADS,), F32)]).reshape(1, LANES).astype(F32)
            rows = jnp.arange(LANES)[:, None]
            tgt = jnp.where(rows < 3 * N_HEADS, aug0[rows % N_HEADS] + rows // N_HEADS, -1)
            pm = jnp.where(jnp.arange(2 * D)[None, :] == tgt, -1.0, 0.0)
            qk, vt = _inproj(x2d, _row(g[0]), wn, bn, w_in[:, 2 * D:3 * D].T.astype(BF), B=B, S=S,
                             fox_extra=[wf3, bf3, pm.astype(BF)])
            fox_near = tuple(range(FOX_TQ // ATT_TK))
            mask_t = _near_tables(None, FOX_TQ, fox_near)
            ot = _attention(qk, qk, vt, mask_t, B=B, S=S, n_prog=N_HEADS, q_spec=(LANES, lambda h: h),
                            k_blk=lambda h: N_HEADS + h, v_blk=lambda h: h, bias_blk=lambda h: 0, nq_stack=1,
                            vdim=HEAD_DIM, mode="causal", out_rows=HEAD_DIM, out_dtype=BF, q_mask="none",
                            tq=FOX_TQ, near=[(a, a) for a in fox_near])
            x2d = _outproj([ot], None, fox_w_out[ic].astype(BF), x2d, _row(g[1]), B=B, S=S)
            ic += 1
        x2d = _mlp_ple(x2d, _row(g[2]), mlp_w1[i].astype(BF), mlp_w2[i].astype(BF), _row(g[3]),
                       ple_gate_w[i].astype(BF), p[i].reshape(B * S, -1), ple_w[i].astype(BF))
    return x2d.reshape(B, S, D)
```

```python
import functools
import math

import jax
import jax.numpy as jnp
from jax import lax
from jax.experimental import pallas as pl
from jax.experimental.pallas import tpu as pltpu

BF = jnp.bfloat16
F32 = jnp.float32

D_MODEL = 1024
HEAD_DIM = 64
LANES = 128
NORM_EPS = 1e-6
MASK_VALUE = -1e30
REL_BUCKETS = 32
REL_MAX_DIST = 128
REL_TABLE_LEN = 512
N_HEADS = D_MODEL // HEAD_DIM
DA_HEADS = N_HEADS // 2
NSA_GROUPS = 4
NSA_REP = N_HEADS // NSA_GROUPS
NSA_CMP_LEN = 32
NSA_CMP_STRIDE = 16
NSA_SEL_LEN = 64
NSA_TOP_N = 16
NSA_WINDOW = 512
NSA_FORCE_SCORE = 1e4
N_MIXERS = 3

ATT_TQ = 256
DA_TQ = 512
FOX_TQ = 1024
ATT_TK = 256
VMEM_LIMIT = 56 * 1024 * 1024
ONES_ROWS = 16
LOG2E = math.log2(math.e)


def _cparams(*sem):
    return pltpu.CompilerParams(dimension_semantics=sem, vmem_limit_bytes=VMEM_LIMIT)


def _rms(x, g):
    return x * lax.rsqrt(jnp.mean(x * x, axis=-1, keepdims=True) + NORM_EPS) * g


def _dot(a, b):
    return jnp.dot(a, b, preferred_element_type=F32)


def _dot_nt(a, b):
    return lax.dot_general(a, b, (((1,), (1,)), ((), ())), preferred_element_type=F32)


def _dot_tn(a, b):
    return lax.dot_general(a, b, (((0,), (0,)), ((), ())), preferred_element_type=F32)


def _inproj_body(*refs, n_f32, n_gate, fox, tm, col_chunk):
    it = iter(refs)
    x_ref, g_ref, wn_ref, bn_ref, wt_ref = (next(it) for _ in range(5))
    wf32_ref = next(it) if n_f32 else None
    wg_ref = next(it) if n_gate else None
    if fox:
        wf_ref, bf_ref, pm_ref = next(it), next(it), next(it)
    on_ref, ot_ref = next(it), next(it)
    of32_ref = next(it) if n_f32 else None
    og_ref = next(it) if n_gate else None
    carry_ref = next(it) if fox else None

    h = _rms(x_ref[...], g_ref[...]).astype(BF)
    n_nat = on_ref.shape[1]
    k_off = n_nat // 2
    if fox:
        @pl.when(pl.program_id(1) == 0)
        def _():
            carry_ref[...] = jnp.zeros_like(carry_ref)

        lf = _dot(h, wf_ref[...]) + bf_ref[...]
        lane = lax.broadcasted_iota(jnp.int32, lf.shape, 1)
        row = lax.broadcasted_iota(jnp.int32, lf.shape, 0)
        ls = jnp.minimum(lf, 0.0) - jnp.log1p(jnp.exp(-jnp.abs(lf)))
        c = jnp.where(lane < 3 * N_HEADS, ls, 0.0)
        k = 1
        while k < tm:
            c = c + jnp.where(row >= k, pltpu.roll(c, k, axis=0), 0.0)
            k *= 2
        c = c + carry_ref[...]
        carry_ref[...] = c[tm - 1:tm, :]
        c = c * LOG2E
        hi = c.astype(BF).astype(F32)
        r1 = c - hi
        mid = r1.astype(BF).astype(F32)
        lo = (r1 - mid).astype(BF).astype(F32)
        c3 = jnp.where(lane < N_HEADS, hi, jnp.where(lane < 2 * N_HEADS, mid, lo)).astype(BF)
    if fox:
        lane_b = lax.broadcasted_iota(jnp.int32, (tm, LANES), 1)
        d_c = wn_ref.shape[1] // 2
        for part in range(2):
            yc = _dot(h, wn_ref[:, part * d_c:(part + 1) * d_c])
            for oc in range(0, k_off, col_chunk):
                pieces = []
                for hh in range(oc // LANES, (oc + col_chunk) // LANES):
                    blk = yc[:, (hh // 2) * LANES:(hh // 2 + 1) * LANES]
                    keep = (lane_b < HEAD_DIM) if hh % 2 == 0 else (lane_b >= HEAD_DIM)
                    pieces.append(jnp.where(keep, blk, 0.0))
                y = jnp.concatenate(pieces, axis=1) + bn_ref[:, part * k_off + oc:part * k_off + oc + col_chunk]
                if part == 1:
                    y = y + _dot(c3, pm_ref[:, oc:oc + col_chunk])
                on_ref[:, part * k_off + oc:part * k_off + oc + col_chunk] = y.astype(on_ref.dtype)
    else:
        for c0 in range(0, n_nat, col_chunk):
            y = _dot(h, wn_ref[:, c0:c0 + col_chunk]) + bn_ref[:, c0:c0 + col_chunk]
            on_ref[:, c0:c0 + col_chunk] = y.astype(on_ref.dtype)
    ot_ref[0] = _dot_nt(wt_ref[...], h).astype(ot_ref.dtype)
    if n_f32:
        of32_ref[...] = _dot(h, wf32_ref[...])
    if n_gate:
        og_ref[0] = jax.nn.sigmoid(_dot_nt(wg_ref[...], h))


def _inproj(x2d, g, wn, bn, wt, *, B, S, tm=512, wf32=None, wg=None, fox_extra=None):
    M, D = x2d.shape
    ns = S // tm
    n_nat, n_t = bn.shape[1], wt.shape[0]
    full = lambda a: pl.BlockSpec(a.shape, lambda b, s: (0,) * a.ndim)
    ins = [x2d, g, wn, bn, wt]
    in_specs = [pl.BlockSpec((tm, D), lambda b, s: (b * ns + s, 0)), full(g), full(wn), full(bn), full(wt)]
    out_shape = [jax.ShapeDtypeStruct((M, n_nat), BF), jax.ShapeDtypeStruct((B, n_t, S), BF)]
    out_specs = [pl.BlockSpec((tm, n_nat), lambda b, s: (b * ns + s, 0)),
                 pl.BlockSpec((1, n_t, tm), lambda b, s: (b, 0, s))]
    scratch = []
    if wf32 is not None:
        ins.append(wf32)
        in_specs.append(full(wf32))
        out_shape.append(jax.ShapeDtypeStruct((M, wf32.shape[1]), F32))
        out_specs.append(pl.BlockSpec((tm, wf32.shape[1]), lambda b, s: (b * ns + s, 0)))
    if wg is not None:
        ins.append(wg)
        in_specs.append(full(wg))
        out_shape.append(jax.ShapeDtypeStruct((B, wg.shape[0], S), F32))
        out_specs.append(pl.BlockSpec((1, wg.shape[0], tm), lambda b, s: (b, 0, s)))
    if fox_extra is not None:
        for a in fox_extra:
            ins.append(a)
            in_specs.append(full(a))
        scratch.append(pltpu.VMEM((1, LANES), F32))
    body = functools.partial(_inproj_body, n_f32=wf32 is not None, n_gate=wg is not None,
                             fox=fox_extra is not None, tm=tm, col_chunk=1024 if n_nat % 1024 == 0 else 512)
    return pl.pallas_call(
        body, grid=(B, ns), in_specs=in_specs, out_specs=out_specs, out_shape=out_shape,
        scratch_shapes=scratch, compiler_params=_cparams("arbitrary", "arbitrary"), name="inproj")(*ins)


def _attn_body(*refs, tq, nq_stack, vdim, mode, use_sel, final, lam_init, q_mask, near):
    it = iter(refs)
    q_ref, qn_ref, k_ref, vt_ref, bias_ref = (next(it) for _ in range(5))
    selb_ref, selbn_ref = (next(it), next(it)) if use_sel else (None, None)
    if final == "da":
        lam_ref, subg_ref = next(it), next(it)
    o_ref = next(it)
    qm_ref, m_ref, acc_ref, s_scr, mt_scr, r0_ref = (next(it) for _ in range(6))
    tk = ATT_TK
    lw = nq_stack * tq
    i = pl.program_id(2)
    i_next = jnp.minimum(i + 1, pl.num_programs(2) - 1)

    lane = lax.broadcasted_iota(jnp.int32, (tq, LANES), 1)
    for slot, src_ref in enumerate((q_ref, qn_ref)):
        qt = src_ref[...]
        for r in range(nq_stack):
            blk = qt[:, 0:LANES] if q_mask == "da" else qt[:, r * LANES:(r + 1) * LANES]
            if q_mask == "da":
                keep = (lane < HEAD_DIM) if r == 0 else (lane >= HEAD_DIM)
            elif q_mask == "group_parity":
                keep = (lane // HEAD_DIM) == (pl.program_id(1) % 2)
            else:
                keep = None
            if keep is not None:
                blk = jnp.where(keep, blk, jnp.zeros_like(blk))
            qm_ref[slot, r * tq:(r + 1) * tq, :] = blk
    m_ref[...] = jnp.full_like(m_ref, MASK_VALUE)
    acc_ref[...] = jnp.zeros_like(acc_ref)

    def stage_qk(group, r, next_tile=False):
        rows = len(group) * tk
        ks = [k_ref[pl.ds(pl.multiple_of(kt * tk, tk), tk), :] for kt, _, _ in group]
        s = _dot_nt(ks[0] if len(ks) == 1 else jnp.concatenate(ks, axis=0), qm_ref[int(next_tile)])
        if use_sel:
            slabs = []
            for t, (kt, _, _) in enumerate(group):
                sb8 = (selbn_ref if next_tile else selb_ref)[0, 0, pl.ds(pl.multiple_of((kt // 2) * 8, 8), 8), :]
                sb4 = jnp.where(kt % 2 == 0, sb8[0:4, :], sb8[4:8, :])
                sbt = jnp.concatenate([sb4] * nq_stack, axis=1)
                for j in range(tk // NSA_SEL_LEN):
                    r0 = t * tk + j * NSA_SEL_LEN
                    slabs.append(s[r0:r0 + NSA_SEL_LEN, :] + sbt[j:j + 1, :])
            s = jnp.concatenate(slabs, axis=0)
        s_scr[r, 0:rows, :] = s
        mt_scr[r] = jnp.max(s, axis=0, keepdims=True)

    def stage_softmax_pv(group, r, zero_first=None):
        rows = len(group) * tk
        if all(bidx is None and fm is None for _, bidx, fm in group):
            s = s_scr[r, 0:rows, :]
            mt = mt_scr[r]
        else:
            parts = []
            for t, (_, bidx, fm) in enumerate(group):
                st = s_scr[r, t * tk:(t + 1) * tk, :]
                if bidx is not None:
                    st = st + jnp.concatenate([bias_ref[q, bidx] for q in range(nq_stack)], axis=1)
                if fm is not None:
                    st = st + jnp.where(fm, MASK_VALUE, 0.0)
                parts.append(st)
            s = parts[0] if len(parts) == 1 else jnp.concatenate(parts, axis=0)
            mt = jnp.max(s, axis=0, keepdims=True)
        m_old = m_ref[...]
        m_new = jnp.maximum(m_old, mt)
        alpha = jnp.exp2(m_old - m_new)
        p = jnp.exp2(s - m_new).astype(BF)
        m_ref[...] = m_new
        vts = [vt_ref[0, :, pl.ds(pl.multiple_of(kt * tk, tk), tk)] for kt, _, _ in group]
        lhs = jnp.concatenate([vts[0] if len(vts) == 1 else jnp.concatenate(vts, axis=1),
                               jnp.ones((ONES_ROWS, rows), BF)], axis=0)
        if zero_first is not None:
            col = lax.broadcasted_iota(jnp.int32, lhs.shape, 1)
            lhs = jnp.where(zero_first & (col < tk), jnp.zeros_like(lhs), lhs)
        acc_ref[...] = alpha * acc_ref[...] + _dot(lhs, p)

    def trip(r, cur, nxt, zero_first=None, next_tile=False):
        for val in (0, 1):
            @pl.when(r == val)
            def _(val=val):
                if nxt:
                    stage_qk(nxt, 1 - val, next_tile)
                stage_softmax_pv(cur, val, zero_first)

    def tile_plan(ii):
        base = ii * (tq // tk)
        near_tiles = [(jnp.maximum(base + a, 0), bidx, (base + a < 0) if a < 0 else None) for a, bidx in near]
        near_groups = [near_tiles[n:n + 2] for n in range(0, len(near_tiles), 2)]
        if mode == "causal":
            n_far = jnp.maximum(base + near[0][0], 0)
            delta = n_far % 2
            trips = (n_far + delta) // 2
            far_group = lambda u: [(jnp.maximum(2 * u + t - delta, 0), None, None) for t in range(2)]
            first = [(jnp.where(trips > 0, far_group(0)[t][0], near_groups[0][t][0]), None, None) for t in range(2)]
        else:
            delta, trips, far_group, first = 0, 0, None, near_groups[0]
        return near_groups, delta, trips, far_group, first

    near_groups, delta, trips, far_group, first = tile_plan(i)
    r0 = jnp.where(i == 0, 0, r0_ref[0])

    @pl.when(i == 0)
    def _():
        stage_qk(first, 0)

    if mode == "causal":
        def body(u, c):
            trip((r0 + u) % 2, far_group(u), far_group(u + 1), zero_first=(u == 0) & (delta == 1))
            return c
        lax.fori_loop(0, trips, body, 0)
    for e, grp in enumerate(near_groups):
        r_e = (r0 + trips + e) % 2
        if e + 1 < len(near_groups):
            trip(r_e, grp, near_groups[e + 1])
        else:
            trip(r_e, grp, tile_plan(i_next)[4], next_tile=True)
            r0_ref[0] = 1 - r_e

    acc = acc_ref[0:vdim, :]
    l = acc_ref[vdim:vdim + 1, :]
    if final == "da":
        lamv = lam_ref[...]
        lam = (jnp.exp(jnp.sum(lamv[0:1] * lamv[1:2], axis=1, keepdims=True))
               - jnp.exp(jnp.sum(lamv[2:3] * lamv[3:4], axis=1, keepdims=True)) + lam_init)
        o = acc[:, :tq] / l[:, :tq] - lam * (acc[:, tq:] / l[:, tq:])
        o = o * lax.rsqrt(jnp.mean(o * o, axis=0, keepdims=True) + NORM_EPS) * subg_ref[...] * (1.0 - lam_init)
        o_ref[0] = o.astype(o_ref.dtype)
    else:
        for r in range(nq_stack):
            o_ref[0, r * vdim:(r + 1) * vdim, :] = (acc[:, r * tq:(r + 1) * tq] / l[:, r * tq:(r + 1) * tq]).astype(o_ref.dtype)


def _attention(q_arr, k_arr, vt, bias, *, B, S, n_prog, q_spec, k_blk, v_blk, bias_blk, nq_stack, vdim, mode,
               out_rows, out_dtype, q_mask, tq, near, selb=None, final="plain", lam=None, subg=None, lam_init=0.0):
    tk = ATT_TK
    nq = S // tq
    lw = nq_stack * tq
    q_w, q_blk = q_spec
    nxt = lambda i: jnp.minimum(i + 1, nq - 1)
    ins = [q_arr, q_arr, k_arr, vt, bias]
    in_specs = [
        pl.BlockSpec((tq, q_w), lambda b, h, i: (b * nq + i, q_blk(h))),
        pl.BlockSpec((tq, q_w), lambda b, h, i: (b * nq + nxt(i), q_blk(h))),
        pl.BlockSpec((S, LANES), lambda b, h, i: (b, k_blk(h))),
        pl.BlockSpec((1, vdim, S), lambda b, h, i: (b, v_blk(h), 0)),
        pl.BlockSpec((nq_stack,) + bias.shape[1:], lambda b, h, i: (bias_blk(h), 0, 0, 0)),
    ]
    if selb is not None:
        ins += [selb, selb]
        in_specs += [pl.BlockSpec((1, 1, selb.shape[2], tq), lambda b, h, i: (b, h, 0, i)),
                     pl.BlockSpec((1, 1, selb.shape[2], tq), lambda b, h, i: (b, h, 0, nxt(i)))]
    if final == "da":
        ins += [lam, subg]
        in_specs += [pl.BlockSpec(lam.shape, lambda b, h, i: (0, 0)), pl.BlockSpec(subg.shape, lambda b, h, i: (0, 0))]
    body = functools.partial(_attn_body, tq=tq, nq_stack=nq_stack, vdim=vdim, mode=mode, use_sel=selb is not None,
                             final=final, lam_init=lam_init, q_mask=q_mask, near=near)
    return pl.pallas_call(
        body, grid=(B, n_prog, nq), in_specs=in_specs,
        out_specs=pl.BlockSpec((1, out_rows, tq), lambda b, h, i: (b, h, i)),
        out_shape=jax.ShapeDtypeStruct((B, out_rows * n_prog, S), out_dtype),
        scratch_shapes=[pltpu.VMEM((2, lw, LANES), BF), pltpu.VMEM((1, lw), F32), pltpu.VMEM((vdim + ONES_ROWS, lw), F32),
                        pltpu.VMEM((2, 2 * tk, lw), F32), pltpu.VMEM((2, 1, lw), F32), pltpu.SMEM((1,), jnp.int32)],
        compiler_params=_cparams("arbitrary", "arbitrary", "arbitrary"), name="attn_" + mode + "_" + final)(*ins)


def _compress_body(x_ref, pe_ref, w1_ref, w2_ref, o_ref):
    x = x_ref[0, 0, 0]
    n, half = x.shape
    pe = pe_ref[0]
    first = _dot((x + pe[:, :half]).astype(BF), w1_ref[0, :half, :])
    second = _dot((x + pe[:, half:]).astype(BF), w1_ref[0, half:, :])
    hcur = first + pltpu.roll(second, n - 1, axis=0)
    hcur = 0.5 * hcur * (1.0 + jnp.tanh(math.sqrt(2.0 / math.pi) * (hcur + 0.044715 * (hcur * hcur * hcur))))
    o_ref[0, 0, 0] = _dot(hcur.astype(BF), w2_ref[0])


def _compress(chunks, pe, w1, w2):
    _, B, G, n, K = chunks.shape
    return pl.pallas_call(
        _compress_body, grid=(2, B, G),
        in_specs=[pl.BlockSpec((1, 1, 1, n, K), lambda a, b, g: (a, b, g, 0, 0)),
                  pl.BlockSpec((1, 1, 2 * K), lambda a, b, g: (a, 0, 0)),
                  pl.BlockSpec((1,) + w1.shape[1:], lambda a, b, g: (a, 0, 0)),
                  pl.BlockSpec((1,) + w2.shape[1:], lambda a, b, g: (a, 0, 0))],
        out_specs=pl.BlockSpec((1, 1, 1, n, HEAD_DIM), lambda a, b, g: (a, b, g, 0, 0)),
        out_shape=jax.ShapeDtypeStruct((2, B, G, n, HEAD_DIM), F32),
        compiler_params=_cparams("arbitrary", "arbitrary", "arbitrary"), name="nsa_compress")(chunks, pe, w1, w2)


def _cmp_body(q_ref, kc_ref, vct_ref, bc_ref, ov_ref, o_ref, selb_ref, s_ref):
    tq = ATT_TQ
    lw = NSA_REP * tq
    i = pl.program_id(2)
    n_cmp = kc_ref.shape[2]
    n_sel = selb_ref.shape[2]
    band = bc_ref.shape[1]
    bc = jnp.concatenate([bc_ref[r] for r in range(NSA_REP)], axis=1)
    qt = q_ref[...]
    lane = lax.broadcasted_iota(jnp.int32, (tq, LANES), 1)
    keep = (lane // HEAD_DIM) == (pl.program_id(1) % 2)
    qm = jnp.concatenate(
        [jnp.where(keep, qt[:, r * LANES:(r + 1) * LANES], jnp.zeros((tq, LANES), BF)) for r in range(NSA_REP)], axis=0)
    s = _dot_nt(kc_ref[0, 0], qm)
    row = lax.broadcasted_iota(jnp.int32, s.shape, 0)
    s_ref[...] = jnp.where(row < (band // 2) * (i + 1), s, MASK_VALUE)

    @pl.when(i == 0)
    def _():
        s_ref[0:band // 2, :] = s_ref[0:band // 2, :] + bc[band // 2:, :]

    @pl.when(i > 0)
    def _():
        r0 = pl.multiple_of((band // 2) * (i - 1), band // 2)
        s_ref[pl.ds(r0, band), :] = s_ref[pl.ds(r0, band), :] + bc

    s = s_ref[...]
    m = jnp.max(s, axis=0, keepdims=True)
    e = jnp.exp2(s - m)
    l = jnp.sum(e, axis=0, keepdims=True)
    inv = jnp.where(m > 0.5 * MASK_VALUE, 1.0 / l, 0.0)
    p = e * inv
    oc = _dot(vct_ref[0, 0], p.astype(BF))
    for r in range(NSA_REP):
        o_ref[0, r * HEAD_DIM:(r + 1) * HEAD_DIM, :] = oc[:, r * tq:(r + 1) * tq]
    ps = p[:, 0:tq]
    for r in range(1, NSA_REP):
        ps = ps + p[:, r * tq:(r + 1) * tq]
    p_hi = ps.astype(BF)
    r1 = ps - p_hi.astype(F32)
    p_mid = r1.astype(BF)
    p_lo = (r1 - p_mid.astype(F32)).astype(BF)
    ov = ov_ref[...]
    imp = _dot(ov, p_hi) + _dot(ov, p_mid) + _dot(ov, p_lo)
    j = lax.broadcasted_iota(jnp.int32, (n_sel, tq), 0)
    t = i * tq + lax.broadcasted_iota(jnp.int32, (n_sel, tq), 1)
    cur = t // NSA_SEL_LEN
    forced = (j == 0) | (j == cur) | (j == cur - 1)
    w = jnp.where(j <= cur, jnp.where(forced, NSA_FORCE_SCORE, imp), -1.0)
    jf = j.astype(F32)
    sel = jnp.zeros((n_sel, tq), F32)
    for _ in range(min(NSA_TOP_N, n_sel)):
        mx = jnp.max(w, axis=0, keepdims=True)
        idx = jnp.min(jnp.where(w == mx, jf, float(n_sel)), axis=0, keepdims=True)
        pick = jf == idx
        sel = jnp.where(pick, 1.0, sel)
        w = jnp.where(pick, -2.0, w)
    selb_ref[0, 0] = jnp.where(sel > 0.5, 0.0, MASK_VALUE)


def _cmp_attention(qn, kc, vct, bc, ov, *, B, S, q_coloff):
    tq = ATT_TQ
    nq = S // tq
    n_cmp = kc.shape[2]
    n_sel = S // NSA_SEL_LEN
    return pl.pallas_call(
        _cmp_body, grid=(B, NSA_GROUPS, nq),
        in_specs=[pl.BlockSpec((tq, NSA_REP * LANES), lambda b, g, i: (b * nq + i, q_coloff + g // 2)),
                  pl.BlockSpec((1, 1, n_cmp, LANES), lambda b, g, i: (b, g, 0, 0)),
                  pl.BlockSpec((1, 1, HEAD_DIM, n_cmp), lambda b, g, i: (b, g, 0, 0)),
                  pl.BlockSpec((NSA_REP,) + bc.shape[1:], lambda b, g, i: (g, 0, 0)),
                  pl.BlockSpec(ov.shape, lambda b, g, i: (0, 0))],
        out_specs=[pl.BlockSpec((1, NSA_REP * HEAD_DIM, tq), lambda b, g, i: (b, g, i)),
                   pl.BlockSpec((1, 1, n_sel, tq), lambda b, g, i: (b, g, 0, i))],
        out_shape=[jax.ShapeDtypeStruct((B, D_MODEL, S), F32), jax.ShapeDtypeStruct((B, NSA_GROUPS, n_sel, S), F32)],
        scratch_shapes=[pltpu.VMEM((n_cmp, NSA_REP * tq), F32)],
        compiler_params=_cparams("arbitrary", "arbitrary", "arbitrary"), name="nsa_cmp_topk")(qn, kc, vct, bc, ov)


def _outproj_body(*refs, n_o):
    it = iter(refs)
    o_refs = [next(it) for _ in range(n_o)]
    gt_ref = next(it) if n_o > 1 else None
    w_ref, x_ref, g_ref, out_ref = (next(it) for _ in range(4))
    if n_o == 1:
        ot = o_refs[0][0]
    else:
        gts = gt_ref[0]
        parts = []
        for hd in range(N_HEADS):
            rows = slice(hd * HEAD_DIM, (hd + 1) * HEAD_DIM)
            acc = gts[hd:hd + 1, :] * o_refs[0][0, rows, :]
            for b in range(1, n_o):
                acc = acc + gts[b * N_HEADS + hd:b * N_HEADS + hd + 1, :] * o_refs[b][0, rows, :]
            parts.append(acc.astype(BF))
        ot = jnp.concatenate(parts, axis=0)
    y = _dot_tn(ot, w_ref[...])
    out_ref[...] = x_ref[...] + _rms(y, g_ref[...])


def _outproj(o_list, gt, w, x2d, g, *, B, S, tm=512):
    M, D = x2d.shape
    ns = S // tm
    ins = list(o_list)
    in_specs = [pl.BlockSpec((1, D, tm), lambda b, s: (b, 0, s)) for _ in o_list]
    if gt is not None:
        ins.append(gt)
        in_specs.append(pl.BlockSpec((1, gt.shape[1], tm), lambda b, s: (b, 0, s)))
    ins += [w, x2d, g]
    in_specs += [pl.BlockSpec(w.shape, lambda b, s: (0, 0)),
                 pl.BlockSpec((tm, D), lambda b, s: (b * ns + s, 0)),
                 pl.BlockSpec(g.shape, lambda b, s: (0, 0))]
    return pl.pallas_call(
        functools.partial(_outproj_body, n_o=len(o_list)), grid=(B, ns), in_specs=in_specs,
        out_specs=pl.BlockSpec((tm, D), lambda b, s: (b * ns + s, 0)),
        out_shape=jax.ShapeDtypeStruct((M, D), F32),
        compiler_params=_cparams("arbitrary", "arbitrary"), name="outproj")(*ins)


def _mlp_body(x_ref, g2_ref, w1_ref, w2_ref, g3_ref, wg_ref, p_ref, wp_ref, out_ref, h_ref, acc_ref):
    f = pl.program_id(1)

    @pl.when(f == 0)
    def _():
        h_ref[...] = _rms(x_ref[...], g2_ref[...]).astype(BF)
        acc_ref[...] = jnp.zeros_like(acc_ref)

    a = jnp.maximum(_dot(h_ref[...], w1_ref[...]), 0.0)
    acc_ref[...] += _dot((a * a).astype(BF), w2_ref[...])

    @pl.when(f == pl.num_programs(1) - 1)
    def _():
        x2 = x_ref[...] + _rms(acc_ref[...], g3_ref[...])
        gate = jax.nn.sigmoid(_dot(x2.astype(BF), wg_ref[...]))
        out_ref[...] = x2 + gate * _dot(p_ref[...].astype(BF), wp_ref[...])


def _mlp_ple(x2d, g2, w1, w2, g3, wg, p2d, wp, *, tm=1024, tf=512):
    M, D = x2d.shape
    FF = w1.shape[1]
    PD = p2d.shape[1]
    const = lambda a: pl.BlockSpec(a.shape, lambda m, f: (0, 0))
    return pl.pallas_call(
        _mlp_body, grid=(M // tm, FF // tf),
        in_specs=[pl.BlockSpec((tm, D), lambda m, f: (m, 0)), const(g2),
                  pl.BlockSpec((D, tf), lambda m, f: (0, f)), pl.BlockSpec((tf, D), lambda m, f: (f, 0)),
                  const(g3), const(wg), pl.BlockSpec((tm, PD), lambda m, f: (m, 0)), const(wp)],
        out_specs=pl.BlockSpec((tm, D), lambda m, f: (m, 0)),
        out_shape=jax.ShapeDtypeStruct((M, D), F32),
        scratch_shapes=[pltpu.VMEM((tm, D), BF), pltpu.VMEM((tm, D), F32)],
        compiler_params=_cparams("arbitrary", "arbitrary"), name="mlp_ple")(x2d, g2, w1, w2, g3, wg, p2d, wp)


def _t5_bucket(dist):
    n = jnp.maximum(dist, 0)
    max_exact = REL_BUCKETS // 2
    nf = jnp.maximum(n, 1).astype(F32)
    large = max_exact + (jnp.log(nf / max_exact) / math.log(REL_MAX_DIST / max_exact)
                         * (REL_BUCKETS - max_exact)).astype(jnp.int32)
    large = jnp.minimum(large, REL_BUCKETS - 1)
    return jnp.where(n < max_exact, n, large)


def _bias_by_distance(rel_bias):
    tb = rel_bias[_t5_bucket(jnp.arange(REL_TABLE_LEN))].astype(F32)
    return ((tb - rel_bias[REL_BUCKETS - 1].astype(F32)[None, :]) * LOG2E).T


def _toeplitz(fn, rows, cols, off, row_stride=1):
    n = row_stride * rows + cols
    d = jnp.arange(n)
    d = jnp.where(d < cols, d, d - n)
    v = fn(d + off)
    flat = jnp.tile(v, (1,) * (v.ndim - 1) + (rows,))[..., :rows * (n - row_stride)]
    return flat.reshape(v.shape[:-1] + (rows, n - row_stride))[..., :cols]


def _bias_fn(tbs, hi_limit=None):
    def fn(d):
        val = jnp.zeros((1,) + d.shape, F32) if tbs is None else tbs[:, jnp.clip(d, 0, REL_TABLE_LEN - 1)]
        bad = d < 0
        if hi_limit is not None:
            bad = bad | (d >= hi_limit)
        return jnp.where(bad[None, :], MASK_VALUE, val)
    return fn


def _near_tables(tbs, tq, offsets, hi_limit=None):
    return jnp.stack([_toeplitz(_bias_fn(tbs, hi_limit), ATT_TK, tq, -a * ATT_TK) for a in offsets], axis=1)


def _cmp_band_table(tbs):
    tq = ATT_TQ
    band = 2 * (tq // NSA_CMP_STRIDE)
    return _toeplitz(_bias_fn(tbs), band, tq, tq - (NSA_CMP_LEN - 1), row_stride=NSA_CMP_STRIDE)


def _nsa_q_perm():
    cols = []
    for p in range(NSA_GROUPS // 2):
        for r in range(NSA_REP):
            for g in (2 * p, 2 * p + 1):
                h = g * NSA_REP + r
                cols.extend(range(h * HEAD_DIM, (h + 1) * HEAD_DIM))
    return jnp.asarray(cols, jnp.int32)


def _row(v):
    return v.reshape(1, -1).astype(F32)


def kernel(x, p, rel_bias, norm_g, mlp_w1, mlp_w2, ple_w, ple_gate_w, da_w_in, da_lambda, da_subln, da_w_out,
           nsa_w_in, nsa_cmp_pe, nsa_cmp_w1, nsa_cmp_w2, nsa_w_out, fox_w_in, fox_b_f, fox_w_out):
    B, S, D = x.shape
    depth = p.shape[0]
    scale = HEAD_DIM ** -0.5 * LOG2E
    tbs = _bias_by_distance(rel_bias)
    da_near = tuple(range(-1, DA_TQ // ATT_TK))
    da_bias = _near_tables(tbs, DA_TQ, da_near)
    x2d = x.reshape(B * S, D)
    ia = ib = ic = 0
    for i in range(depth):
        g = norm_g[i]
        kind = i % N_MIXERS
        if kind == 0:
            lam_init = 0.8 - 0.6 * math.exp(-0.3 * i)
            w_in = da_w_in[ia]
            wn = jnp.concatenate([w_in[:, :D] * scale, w_in[:, D:2 * D]], axis=1).astype(BF)
            wt = w_in[:, 2 * D:].T.astype(BF)
            qk, vt = _inproj(x2d, _row(g[0]), wn, jnp.zeros((1, 2 * D), F32), wt, B=B, S=S)
            ot = _attention(qk, qk, vt, da_bias, B=B, S=S, n_prog=DA_HEADS, q_spec=(LANES, lambda h: h),
                            k_blk=lambda h: DA_HEADS + h, v_blk=lambda h: h, bias_blk=lambda h: h, nq_stack=2,
                            vdim=2 * HEAD_DIM, mode="causal", out_rows=2 * HEAD_DIM, out_dtype=BF, q_mask="da",
                            tq=DA_TQ, near=[(da_near[0] - 1, None)] + [(a, n) for n, a in enumerate(da_near)],
                            final="da", lam=da_lambda[ia].astype(F32), subg=da_subln[ia].reshape(-1, 1).astype(F32),
                            lam_init=lam_init)
            x2d = _outproj([ot], None, da_w_out[ia].astype(BF), x2d, _row(g[1]), B=B, S=S)
            ia += 1
        elif kind == 1:
            w_in = nsa_w_in[ib]
            kvd = NSA_GROUPS * HEAD_DIM
            wq = w_in[:, :D][:, _nsa_q_perm()] * scale
            kv = [w_in[:, D + a * kvd:D + (a + 1) * kvd] for a in range(6)]
            wn = jnp.concatenate([wq, kv[2], kv[4]], axis=1).astype(BF)
            wt = jnp.concatenate([kv[3], kv[5]], axis=1).T.astype(BF)
            wf32 = jnp.concatenate([kv[0], kv[1]], axis=1).astype(BF)
            wgate = w_in[:, D + 6 * kvd:].reshape(D, N_HEADS, 3).transpose(2, 1, 0).reshape(3 * N_HEADS, D).astype(BF)
            qn, vt, cmp_in, gates_t = _inproj(x2d, _row(g[0]), wn, jnp.zeros((1, wn.shape[1]), F32), wt, B=B, S=S,
                                              wf32=wf32, wg=wgate)
            n_chunk = S // NSA_CMP_STRIDE
            cm = cmp_in.reshape(B, n_chunk, NSA_CMP_STRIDE, 2, NSA_GROUPS, HEAD_DIM).transpose(3, 0, 4, 1, 2, 5)
            cm = cm.reshape(2, B, NSA_GROUPS, n_chunk, NSA_CMP_STRIDE * HEAD_DIM)
            pe = nsa_cmp_pe[ib].reshape(2, 1, NSA_CMP_LEN * HEAD_DIM).astype(F32)
            kvc = _compress(cm, pe, nsa_cmp_w1[ib].astype(BF), nsa_cmp_w2[ib].astype(BF))
            kc = jnp.concatenate([kvc[0], kvc[0]], axis=-1).astype(BF)
            vct = kvc[1].transpose(0, 1, 3, 2).astype(BF)
            jj = jnp.arange(S // NSA_SEL_LEN)[:, None]
            nn = jnp.arange(n_chunk)[None, :]
            n_cmp_blocks = (S - NSA_CMP_LEN) // NSA_CMP_STRIDE + 1
            ov = ((nn * NSA_CMP_STRIDE < (jj + 1) * NSA_SEL_LEN) & (nn * NSA_CMP_STRIDE + NSA_CMP_LEN - 1 >= jj * NSA_SEL_LEN)
                  & (nn < n_cmp_blocks)).astype(BF)
            cmp_band = _cmp_band_table(tbs)
            oc_t, selb = _cmp_attention(qn, kc, vct, cmp_band, ov, B=B, S=S, q_coloff=0)
            nsa_kw = dict(B=B, S=S, n_prog=NSA_GROUPS, q_spec=(NSA_REP * LANES, lambda h: h // 2),
                          bias_blk=lambda h: h, nq_stack=NSA_REP, vdim=HEAD_DIM, out_rows=NSA_REP * HEAD_DIM,
                          out_dtype=F32, q_mask="group_parity", tq=ATT_TQ)
            kblk0 = D // LANES
            sel_bias = _near_tables(tbs, ATT_TQ, (-1, 0))
            win_bias = _near_tables(tbs, ATT_TQ, (-2, -1, 0), hi_limit=NSA_WINDOW)
            os_t = _attention(qn, qn, vt, sel_bias, mode="causal", near=[(-1, 0), (0, 1)],
                              k_blk=lambda h: kblk0 + h // 2, v_blk=lambda h: h, selb=selb, **nsa_kw)
            ow_t = _attention(qn, qn, vt, win_bias, mode="window", near=[(-2, 0), (-1, 1), (0, 2)],
                              k_blk=lambda h: kblk0 + 2 + h // 2, v_blk=lambda h: NSA_GROUPS + h, **nsa_kw)
            x2d = _outproj([oc_t, os_t, ow_t], gates_t, nsa_w_out[ib].astype(BF), x2d, _row(g[1]), B=B, S=S)
            ib += 1
        else:
            w_in = fox_w_in[ic]
            wn = jnp.concatenate([w_in[:, :D] * scale, w_in[:, D:2 * D]], axis=1).astype(BF)
            hh = jnp.arange(N_HEADS)
            aug0 = hh * LANES + jnp.where(hh % 2 == 0, HEAD_DIM, 0)
            sel = jnp.arange(2 * D)[None, :] - aug0[:, None]
            aug = jnp.any((sel >= 0) & (sel < 3), axis=0).astype(F32).reshape(1, 2 * D)
            bn = jnp.concatenate([aug, jnp.zeros((1, 2 * D), F32)], axis=1)
            wf = w_in[:, 3 * D:]
            wf3 = jnp.concatenate([wf, wf, wf, jnp.zeros((D, LANES - 3 * N_HEADS), F32)], axis=1).astype(BF)
            bf3 = jnp.concatenate([fox_b_f[ic]] * 3 + [jnp.zeros((LANES - 3 * N_HEADS,), F32)]).reshape(1, LANES).astype(F32)
            rows = jnp.arange(LANES)[:, None]
            tgt = jnp.where(rows < 3 * N_HEADS, aug0[rows % N_HEADS] + rows // N_HEADS, -1)
            pm = jnp.where(jnp.arange(2 * D)[None, :] == tgt, -1.0, 0.0)
            qk, vt = _inproj(x2d, _row(g[0]), wn, bn, w_in[:, 2 * D:3 * D].T.astype(BF), B=B, S=S,
                             fox_extra=[wf3, bf3, pm.astype(BF)])
            fox_near = tuple(range(FOX_TQ // ATT_TK))
            mask_t = _near_tables(None, FOX_TQ, fox_near)
            ot = _attention(qk, qk, vt, mask_t, B=B, S=S, n_prog=N_HEADS, q_spec=(LANES, lambda h: h),
                            k_blk=lambda h: N_HEADS + h, v_blk=lambda h: h, bias_blk=lambda h: 0, nq_stack=1,
                            vdim=HEAD_DIM, mode="causal", out_rows=HEAD_DIM, out_dtype=BF, q_mask="none",
                            tq=FOX_TQ, near=[(a, a) for a in fox_near])
            x2d = _outproj([ot], None, fox_w_out[ic].astype(BF), x2d, _row(g[1]), B=B, S=S)
            ic += 1
        x2d = _mlp_ple(x2d, _row(g[2]), mlp_w1[i].astype(BF), mlp_w2[i].astype(BF), _row(g[3]),
                       ple_gate_w[i].astype(BF), p[i].reshape(B * S, -1), ple_w[i].astype(BF))
    return x2d.reshape(B, S, D)
```

```python
import functools
import math

import jax
import jax.numpy as jnp
from jax import lax
from jax.experimental import pallas as pl
from jax.experimental.pallas import tpu as pltpu

BF = jnp.bfloat16
F32 = jnp.float32

D_MODEL = 1024
HEAD_DIM = 64
LANES = 128
NORM_EPS = 1e-6
MASK_VALUE = -1e30
REL_BUCKETS = 32
REL_MAX_DIST = 128
REL_TABLE_LEN = 512
N_HEADS = D_MODEL // HEAD_DIM
DA_HEADS = N_HEADS // 2
NSA_GROUPS = 4
NSA_REP = N_HEADS // NSA_GROUPS
NSA_CMP_LEN = 32
NSA_CMP_STRIDE = 16
NSA_SEL_LEN = 64
NSA_TOP_N = 16
NSA_WINDOW = 512
NSA_FORCE_SCORE = 1e4
N_MIXERS = 3

ATT_TQ = 256
DA_TQ = 512
FOX_TQ = 1024
ATT_TK = 256
VMEM_LIMIT = 56 * 1024 * 1024
CMP_ROW_STEP = 128
PV_CHUNK = 256
ONES_ROWS = 16
LOG2E = math.log2(math.e)


def _cparams(*sem):
    return pltpu.CompilerParams(dimension_semantics=sem, vmem_limit_bytes=VMEM_LIMIT)


def _rms(x, g):
    return x * lax.rsqrt(jnp.mean(x * x, axis=-1, keepdims=True) + NORM_EPS) * g


def _dot(a, b):
    return jnp.dot(a, b, preferred_element_type=F32)


def _dot_nt(a, b):
    return lax.dot_general(a, b, (((1,), (1,)), ((), ())), preferred_element_type=F32)


def _dot_tn(a, b):
    return lax.dot_general(a, b, (((0,), (0,)), ((), ())), preferred_element_type=F32)


def _inproj_body(*refs, n_f32, n_gate, fox, tm, col_chunk):
    it = iter(refs)
    x_ref, g_ref, wn_ref, bn_ref, wt_ref = (next(it) for _ in range(5))
    wf32_ref = next(it) if n_f32 else None
    wg_ref = next(it) if n_gate else None
    if fox:
        wf_ref, bf_ref, pm_ref = next(it), next(it), next(it)
    on_ref, ot_ref = next(it), next(it)
    of32_ref = next(it) if n_f32 else None
    og_ref = next(it) if n_gate else None
    carry_ref = next(it) if fox else None

    h = _rms(x_ref[...], g_ref[...]).astype(BF)
    n_nat = on_ref.shape[1]
    k_off = n_nat // 2
    if fox:
        @pl.when(pl.program_id(1) == 0)
        def _():
            carry_ref[...] = jnp.zeros_like(carry_ref)

        lf = _dot(h, wf_ref[...]) + bf_ref[...]
        lane = lax.broadcasted_iota(jnp.int32, lf.shape, 1)
        row = lax.broadcasted_iota(jnp.int32, lf.shape, 0)
        ls = jnp.minimum(lf, 0.0) - jnp.log1p(jnp.exp(-jnp.abs(lf)))
        c = jnp.where(lane < 3 * N_HEADS, ls, 0.0)
        k = 1
        while k < tm:
            c = c + jnp.where(row >= k, pltpu.roll(c, k, axis=0), 0.0)
            k *= 2
        c = c + carry_ref[...]
        carry_ref[...] = c[tm - 1:tm, :]
        c = c * LOG2E
        hi = c.astype(BF).astype(F32)
        r1 = c - hi
        mid = r1.astype(BF).astype(F32)
        lo = (r1 - mid).astype(BF).astype(F32)
        c3 = jnp.where(lane < N_HEADS, hi, jnp.where(lane < 2 * N_HEADS, mid, lo)).astype(BF)
    if fox:
        lane_b = lax.broadcasted_iota(jnp.int32, (tm, LANES), 1)
        d_c = wn_ref.shape[1] // 2
        for part in range(2):
            yc = _dot(h, wn_ref[:, part * d_c:(part + 1) * d_c])
            for oc in range(0, k_off, col_chunk):
                pieces = []
                for hh in range(oc // LANES, (oc + col_chunk) // LANES):
                    blk = yc[:, (hh // 2) * LANES:(hh // 2 + 1) * LANES]
                    keep = (lane_b < HEAD_DIM) if hh % 2 == 0 else (lane_b >= HEAD_DIM)
                    pieces.append(jnp.where(keep, blk, 0.0))
                y = jnp.concatenate(pieces, axis=1) + bn_ref[:, part * k_off + oc:part * k_off + oc + col_chunk]
                if part == 1:
                    y = y + _dot(c3, pm_ref[:, oc:oc + col_chunk])
                on_ref[:, part * k_off + oc:part * k_off + oc + col_chunk] = y.astype(on_ref.dtype)
    else:
        for c0 in range(0, n_nat, col_chunk):
            y = _dot(h, wn_ref[:, c0:c0 + col_chunk]) + bn_ref[:, c0:c0 + col_chunk]
            on_ref[:, c0:c0 + col_chunk] = y.astype(on_ref.dtype)
    ot_ref[0] = _dot_nt(wt_ref[...], h).astype(ot_ref.dtype)
    if n_f32:
        of32_ref[...] = _dot(h, wf32_ref[...])
    if n_gate:
        og_ref[0] = jax.nn.sigmoid(_dot_nt(wg_ref[...], h))


def _inproj(x2d, g, wn, bn, wt, *, B, S, tm=512, wf32=None, wg=None, fox_extra=None):
    M, D = x2d.shape
    ns = S // tm
    n_nat, n_t = bn.shape[1], wt.shape[0]
    full = lambda a: pl.BlockSpec(a.shape, lambda b, s: (0,) * a.ndim)
    ins = [x2d, g, wn, bn, wt]
    in_specs = [pl.BlockSpec((tm, D), lambda b, s: (b * ns + s, 0)), full(g), full(wn), full(bn), full(wt)]
    out_shape = [jax.ShapeDtypeStruct((M, n_nat), BF), jax.ShapeDtypeStruct((B, n_t, S), BF)]
    out_specs = [pl.BlockSpec((tm, n_nat), lambda b, s: (b * ns + s, 0)),
                 pl.BlockSpec((1, n_t, tm), lambda b, s: (b, 0, s))]
    scratch = []
    if wf32 is not None:
        ins.append(wf32)
        in_specs.append(full(wf32))
        out_shape.append(jax.ShapeDtypeStruct((M, wf32.shape[1]), F32))
        out_specs.append(pl.BlockSpec((tm, wf32.shape[1]), lambda b, s: (b * ns + s, 0)))
    if wg is not None:
        ins.append(wg)
        in_specs.append(full(wg))
        out_shape.append(jax.ShapeDtypeStruct((B, wg.shape[0], S), F32))
        out_specs.append(pl.BlockSpec((1, wg.shape[0], tm), lambda b, s: (b, 0, s)))
    if fox_extra is not None:
        for a in fox_extra:
            ins.append(a)
            in_specs.append(full(a))
        scratch.append(pltpu.VMEM((1, LANES), F32))
    body = functools.partial(_inproj_body, n_f32=wf32 is not None, n_gate=wg is not None,
                             fox=fox_extra is not None, tm=tm, col_chunk=1024 if n_nat % 1024 == 0 else 512)
    return pl.pallas_call(
        body, grid=(B, ns), in_specs=in_specs, out_specs=out_specs, out_shape=out_shape,
        scratch_shapes=scratch, compiler_params=_cparams("arbitrary", "arbitrary"), name="inproj")(*ins)


def _attn_body(*refs, tq, nq_stack, vdim, mode, use_sel, final, lam_init, q_mask, near):
    it = iter(refs)
    q_ref, qn_ref, k_ref, vt_ref, bias_ref = (next(it) for _ in range(5))
    selb_ref, selbn_ref = (next(it), next(it)) if use_sel else (None, None)
    if final == "da":
        lam_ref, subg_ref = next(it), next(it)
    o_ref = next(it)
    qm_ref, m_ref, acc_ref, s_scr, mt_scr, r0_ref = (next(it) for _ in range(6))
    tk = ATT_TK
    lw = nq_stack * tq
    i = pl.program_id(2)
    i_next = jnp.minimum(i + 1, pl.num_programs(2) - 1)

    lane = lax.broadcasted_iota(jnp.int32, (tq, LANES), 1)
    for slot, src_ref in enumerate((q_ref, qn_ref)):
        qt = src_ref[...]
        for r in range(nq_stack):
            blk = qt[:, 0:LANES] if q_mask == "da" else qt[:, r * LANES:(r + 1) * LANES]
            if q_mask == "da":
                keep = (lane < HEAD_DIM) if r == 0 else (lane >= HEAD_DIM)
            elif q_mask == "group_parity":
                keep = (lane // HEAD_DIM) == (pl.program_id(1) % 2)
            else:
                keep = None
            if keep is not None:
                blk = jnp.where(keep, blk, jnp.zeros_like(blk))
            qm_ref[slot, r * tq:(r + 1) * tq, :] = blk
    m_ref[...] = jnp.full_like(m_ref, MASK_VALUE)
    acc_ref[...] = jnp.zeros_like(acc_ref)

    def stage_qk(group, r, next_tile=False):
        rows = len(group) * tk
        ks = [k_ref[pl.ds(pl.multiple_of(kt * tk, tk), tk), :] for kt, _, _ in group]
        s = _dot_nt(ks[0] if len(ks) == 1 else jnp.concatenate(ks, axis=0), qm_ref[int(next_tile)])
        if use_sel:
            slabs = []
            for t, (kt, _, _) in enumerate(group):
                sb8 = (selbn_ref if next_tile else selb_ref)[0, 0, pl.ds(pl.multiple_of((kt // 2) * 8, 8), 8), :]
                sb4 = jnp.where(kt % 2 == 0, sb8[0:4, :], sb8[4:8, :])
                sbt = jnp.concatenate([sb4] * nq_stack, axis=1)
                for j in range(tk // NSA_SEL_LEN):
                    r0 = t * tk + j * NSA_SEL_LEN
                    slabs.append(s[r0:r0 + NSA_SEL_LEN, :] + sbt[j:j + 1, :])
            s = jnp.concatenate(slabs, axis=0)
        s_scr[r, 0:rows, :] = s
        mt_scr[r] = jnp.max(s, axis=0, keepdims=True)

    def stage_softmax_pv(group, r, zero_first=None):
        rows = len(group) * tk
        plain = all(bidx is None and fm is None for _, bidx, fm in group)
        vts = [vt_ref[0, :, pl.ds(pl.multiple_of(kt * tk, tk), tk)] for kt, _, _ in group]
        lhs = jnp.concatenate([vts[0] if len(vts) == 1 else jnp.concatenate(vts, axis=1),
                               jnp.ones((ONES_ROWS, rows), BF)], axis=0)
        if zero_first is not None:
            col = lax.broadcasted_iota(jnp.int32, lhs.shape, 1)
            lhs = jnp.where(zero_first & (col < tk), jnp.zeros_like(lhs), lhs)
        for c0 in range(0, lw, PV_CHUNK):
            cols = slice(c0, c0 + PV_CHUNK)
            if plain:
                s = s_scr[r, 0:rows, cols]
                mt = mt_scr[r, :, cols]
            else:
                parts = []
                for t, (_, bidx, fm) in enumerate(group):
                    st = s_scr[r, t * tk:(t + 1) * tk, cols]
                    if bidx is not None:
                        q, l0 = divmod(c0, tq)
                        st = st + bias_ref[q, bidx, :, l0:l0 + PV_CHUNK]
                    if fm is not None:
                        st = st + jnp.where(fm, MASK_VALUE, 0.0)
                    parts.append(st)
                s = parts[0] if len(parts) == 1 else jnp.concatenate(parts, axis=0)
                mt = jnp.max(s, axis=0, keepdims=True)
            m_old = m_ref[:, cols]
            m_new = jnp.maximum(m_old, mt)
            alpha = jnp.exp2(m_old - m_new)
            p = jnp.exp2(s - m_new).astype(BF)
            m_ref[:, cols] = m_new
            acc_ref[:, cols] = alpha * acc_ref[:, cols] + _dot(lhs, p)

    def trip(r, cur, nxt, zero_first=None, next_tile=False):
        if not isinstance(r, int):
            for val in (0, 1):
                @pl.when(r == val)
                def _(val=val):
                    trip(val, cur, nxt, zero_first, next_tile)
            return
        if nxt:
            stage_qk(nxt, 1 - r, next_tile)
        stage_softmax_pv(cur, r, zero_first)

    def tile_plan(ii):
        base = ii * (tq // tk)
        near_tiles = [(jnp.maximum(base + a, 0), bidx, (base + a < 0) if a < 0 else None) for a, bidx in near]
        near_groups = [near_tiles[n:n + 2] for n in range(0, len(near_tiles), 2)]
        if mode == "causal":
            n_far = jnp.maximum(base + near[0][0], 0)
            delta = n_far % 2
            trips = (n_far + delta) // 2
            far_group = lambda u: [(jnp.maximum(2 * u + t - delta, 0), None, None) for t in range(2)]
            first = [(jnp.where(trips > 0, far_group(0)[t][0], near_groups[0][t][0]), None, None) for t in range(2)]
        else:
            delta, trips, far_group, first = 0, 0, None, near_groups[0]
        return near_groups, delta, trips, far_group, first

    near_groups, delta, trips, far_group, first = tile_plan(i)
    if mode == "causal":
        r0 = jnp.where(i == 0, 0, r0_ref[0])

        @pl.when(i == 0)
        def _():
            stage_qk(first, 0)

        def body(u, c):
            trip((r0 + u) % 2, far_group(u), far_group(u + 1), zero_first=(u == 0) & (delta == 1))
            return c
        lax.fori_loop(0, trips, body, 0)
    else:
        r0 = 0
        stage_qk(first, 0)
    for e, grp in enumerate(near_groups):
        r_e = (r0 + trips + e) % 2
        if e + 1 < len(near_groups):
            trip(r_e, grp, near_groups[e + 1])
        elif mode == "causal":
            trip(r_e, grp, tile_plan(i_next)[4], next_tile=True)
            r0_ref[0] = 1 - r_e
        else:
            trip(r_e, grp, None)

    acc = acc_ref[0:vdim, :]
    l = acc_ref[vdim:vdim + 1, :]
    if final == "da":
        lamv = lam_ref[...]
        lam = (jnp.exp(jnp.sum(lamv[0:1] * lamv[1:2], axis=1, keepdims=True))
               - jnp.exp(jnp.sum(lamv[2:3] * lamv[3:4], axis=1, keepdims=True)) + lam_init)
        o = acc[:, :tq] / l[:, :tq] - lam * (acc[:, tq:] / l[:, tq:])
        o = o * lax.rsqrt(jnp.mean(o * o, axis=0, keepdims=True) + NORM_EPS) * subg_ref[...] * (1.0 - lam_init)
        o_ref[0] = o.astype(o_ref.dtype)
    else:
        for r in range(nq_stack):
            o_ref[0, r * vdim:(r + 1) * vdim, :] = (acc[:, r * tq:(r + 1) * tq] / l[:, r * tq:(r + 1) * tq]).astype(o_ref.dtype)


def _attention(q_arr, k_arr, vt, bias, *, B, S, n_prog, q_spec, k_blk, v_blk, bias_blk, nq_stack, vdim, mode,
               out_rows, out_dtype, q_mask, tq, near, selb=None, final="plain", lam=None, subg=None, lam_init=0.0):
    tk = ATT_TK
    nq = S // tq
    lw = nq_stack * tq
    q_w, q_blk = q_spec
    nxt = lambda i: jnp.minimum(i + 1, nq - 1)
    ins = [q_arr, q_arr, k_arr, vt, bias]
    in_specs = [
        pl.BlockSpec((tq, q_w), lambda b, h, i: (b * nq + i, q_blk(h))),
        pl.BlockSpec((tq, q_w), lambda b, h, i: (b * nq + nxt(i), q_blk(h))),
        pl.BlockSpec((S, LANES), lambda b, h, i: (b, k_blk(h))),
        pl.BlockSpec((1, vdim, S), lambda b, h, i: (b, v_blk(h), 0)),
        pl.BlockSpec((nq_stack,) + bias.shape[1:], lambda b, h, i: (bias_blk(h), 0, 0, 0)),
    ]
    if selb is not None:
        ins += [selb, selb]
        in_specs += [pl.BlockSpec((1, 1, selb.shape[2], tq), lambda b, h, i: (b, h, 0, i)),
                     pl.BlockSpec((1, 1, selb.shape[2], tq), lambda b, h, i: (b, h, 0, nxt(i)))]
    if final == "da":
        ins += [lam, subg]
        in_specs += [pl.BlockSpec(lam.shape, lambda b, h, i: (0, 0)), pl.BlockSpec(subg.shape, lambda b, h, i: (0, 0))]
    body = functools.partial(_attn_body, tq=tq, nq_stack=nq_stack, vdim=vdim, mode=mode, use_sel=selb is not None,
                             final=final, lam_init=lam_init, q_mask=q_mask, near=near)
    return pl.pallas_call(
        body, grid=(B, n_prog, nq), in_specs=in_specs,
        out_specs=pl.BlockSpec((1, out_rows, tq), lambda b, h, i: (b, h, i)),
        out_shape=jax.ShapeDtypeStruct((B, out_rows * n_prog, S), out_dtype),
        scratch_shapes=[pltpu.VMEM((2, lw, LANES), BF), pltpu.VMEM((1, lw), F32), pltpu.VMEM((vdim + ONES_ROWS, lw), F32),
                        pltpu.VMEM((2, 2 * tk, lw), F32), pltpu.VMEM((2, 1, lw), F32), pltpu.SMEM((1,), jnp.int32)],
        compiler_params=_cparams("arbitrary", "arbitrary", "arbitrary"), name="attn_" + mode + "_" + final)(*ins)


def _compress_body(x_ref, pe_ref, w1_ref, w2_ref, o_ref):
    x = x_ref[0, 0, 0]
    n, half = x.shape
    pe = pe_ref[0]
    first = _dot((x + pe[:, :half]).astype(BF), w1_ref[0, :half, :])
    second = _dot((x + pe[:, half:]).astype(BF), w1_ref[0, half:, :])
    hcur = first + pltpu.roll(second, n - 1, axis=0)
    hcur = 0.5 * hcur * (1.0 + jnp.tanh(math.sqrt(2.0 / math.pi) * (hcur + 0.044715 * (hcur * hcur * hcur))))
    o_ref[0, 0, 0] = _dot(hcur.astype(BF), w2_ref[0])


def _compress(chunks, pe, w1, w2):
    _, B, G, n, K = chunks.shape
    return pl.pallas_call(
        _compress_body, grid=(2, B, G),
        in_specs=[pl.BlockSpec((1, 1, 1, n, K), lambda a, b, g: (a, b, g, 0, 0)),
                  pl.BlockSpec((1, 1, 2 * K), lambda a, b, g: (a, 0, 0)),
                  pl.BlockSpec((1,) + w1.shape[1:], lambda a, b, g: (a, 0, 0)),
                  pl.BlockSpec((1,) + w2.shape[1:], lambda a, b, g: (a, 0, 0))],
        out_specs=pl.BlockSpec((1, 1, 1, n, HEAD_DIM), lambda a, b, g: (a, b, g, 0, 0)),
        out_shape=jax.ShapeDtypeStruct((2, B, G, n, HEAD_DIM), F32),
        compiler_params=_cparams("arbitrary", "arbitrary", "arbitrary"), name="nsa_compress")(chunks, pe, w1, w2)


def _cmp_body(q_ref, kc_ref, vct_ref, bc_ref, ov_ref, o_ref, selb_ref, s_ref, imp_ref):
    tq = ATT_TQ
    i = pl.program_id(2)
    n_cmp = kc_ref.shape[2]
    n_sel = selb_ref.shape[2]
    band = bc_ref.shape[1]
    bc = jnp.concatenate([bc_ref[r] for r in range(NSA_REP)], axis=1)
    qt = q_ref[...]
    lane = lax.broadcasted_iota(jnp.int32, (tq, LANES), 1)
    keep = (lane // HEAD_DIM) == (pl.program_id(1) % 2)
    qm = jnp.concatenate(
        [jnp.where(keep, qt[:, r * LANES:(r + 1) * LANES], jnp.zeros((tq, LANES), BF)) for r in range(NSA_REP)], axis=0)

    def attend(rows):
        s = _dot_nt(kc_ref[0, 0, 0:rows, :], qm)
        row = lax.broadcasted_iota(jnp.int32, s.shape, 0)
        s_ref[0:rows, :] = jnp.where(row < (band // 2) * (i + 1), s, MASK_VALUE)

        @pl.when(i == 0)
        def _():
            s_ref[0:band // 2, :] = s_ref[0:band // 2, :] + bc[band // 2:, :]

        @pl.when(i > 0)
        def _():
            r0 = pl.multiple_of((band // 2) * (i - 1), band // 2)
            s_ref[pl.ds(r0, band), :] = s_ref[pl.ds(r0, band), :] + bc

        ps = None
        for r in range(NSA_REP):
            s = s_ref[0:rows, r * tq:(r + 1) * tq]
            m = jnp.max(s, axis=0, keepdims=True)
            e = jnp.exp2(s - m)
            l = jnp.sum(e, axis=0, keepdims=True)
            inv = jnp.where(m > 0.5 * MASK_VALUE, 1.0 / l, 0.0)
            p = e * inv
            o_ref[0, r * HEAD_DIM:(r + 1) * HEAD_DIM, :] = _dot(vct_ref[0, 0, :, 0:rows], p.astype(BF))
            ps = p if ps is None else ps + p
        p_hi = ps.astype(BF)
        r1 = ps - p_hi.astype(F32)
        p_mid = r1.astype(BF)
        p_lo = (r1 - p_mid.astype(F32)).astype(BF)
        ov = ov_ref[:, 0:rows]
        imp_ref[...] = _dot(ov, p_hi) + _dot(ov, p_mid) + _dot(ov, p_lo)

    step = min(CMP_ROW_STEP, n_cmp)
    tiles_per_step = step // (band // 2)
    for v in range(n_cmp // step):
        @pl.when((i >= v * tiles_per_step) & (i < (v + 1) * tiles_per_step))
        def _(v=v):
            attend((v + 1) * step)

    imp = imp_ref[...]
    j = lax.broadcasted_iota(jnp.int32, (n_sel, tq), 0)
    t = i * tq + lax.broadcasted_iota(jnp.int32, (n_sel, tq), 1)
    cur = t // NSA_SEL_LEN
    forced = (j == 0) | (j == cur) | (j == cur - 1)
    w = jnp.where(j <= cur, jnp.where(forced, NSA_FORCE_SCORE, imp), -1.0)
    jf = j.astype(F32)
    for _ in range(min(NSA_TOP_N, n_sel)):
        mx = jnp.max(w, axis=0, keepdims=True)
        idx = jnp.min(jnp.where(w == mx, jf, float(n_sel)), axis=0, keepdims=True)
        w = jnp.where(jf == idx, -2.0, w)
    selb_ref[0, 0] = jnp.where(w == -2.0, 0.0, MASK_VALUE)


def _cmp_attention(qn, kc, vct, bc, ov, *, B, S, q_coloff):
    tq = ATT_TQ
    nq = S // tq
    n_cmp = kc.shape[2]
    n_sel = S // NSA_SEL_LEN
    return pl.pallas_call(
        _cmp_body, grid=(B, NSA_GROUPS, nq),
        in_specs=[pl.BlockSpec((tq, NSA_REP * LANES), lambda b, g, i: (b * nq + i, q_coloff + g // 2)),
                  pl.BlockSpec((1, 1, n_cmp, LANES), lambda b, g, i: (b, g, 0, 0)),
                  pl.BlockSpec((1, 1, HEAD_DIM, n_cmp), lambda b, g, i: (b, g, 0, 0)),
                  pl.BlockSpec((NSA_REP,) + bc.shape[1:], lambda b, g, i: (g, 0, 0)),
                  pl.BlockSpec(ov.shape, lambda b, g, i: (0, 0))],
        out_specs=[pl.BlockSpec((1, NSA_REP * HEAD_DIM, tq), lambda b, g, i: (b, g, i)),
                   pl.BlockSpec((1, 1, n_sel, tq), lambda b, g, i: (b, g, 0, i))],
        out_shape=[jax.ShapeDtypeStruct((B, D_MODEL, S), F32), jax.ShapeDtypeStruct((B, NSA_GROUPS, n_sel, S), F32)],
        scratch_shapes=[pltpu.VMEM((n_cmp, NSA_REP * tq), F32), pltpu.VMEM((n_sel, tq), F32)],
        compiler_params=_cparams("arbitrary", "arbitrary", "arbitrary"), name="nsa_cmp_topk")(qn, kc, vct, bc, ov)


def _outproj_body(*refs, n_o):
    it = iter(refs)
    o_refs = [next(it) for _ in range(n_o)]
    gt_ref = next(it) if n_o > 1 else None
    w_ref, x_ref, g_ref, out_ref = (next(it) for _ in range(4))
    if n_o == 1:
        ot = o_refs[0][0]
    else:
        gts = gt_ref[0]
        parts = []
        for hd in range(N_HEADS):
            rows = slice(hd * HEAD_DIM, (hd + 1) * HEAD_DIM)
            acc = gts[hd:hd + 1, :] * o_refs[0][0, rows, :]
            for b in range(1, n_o):
                acc = acc + gts[b * N_HEADS + hd:b * N_HEADS + hd + 1, :] * o_refs[b][0, rows, :]
            parts.append(acc.astype(BF))
        ot = jnp.concatenate(parts, axis=0)
    y = _dot_tn(ot, w_ref[...])
    out_ref[...] = x_ref[...] + _rms(y, g_ref[...])


def _outproj(o_list, gt, w, x2d, g, *, B, S, tm=512):
    M, D = x2d.shape
    ns = S // tm
    ins = list(o_list)
    in_specs = [pl.BlockSpec((1, D, tm), lambda b, s: (b, 0, s)) for _ in o_list]
    if gt is not None:
        ins.append(gt)
        in_specs.append(pl.BlockSpec((1, gt.shape[1], tm), lambda b, s: (b, 0, s)))
    ins += [w, x2d, g]
    in_specs += [pl.BlockSpec(w.shape, lambda b, s: (0, 0)),
                 pl.BlockSpec((tm, D), lambda b, s: (b * ns + s, 0)),
                 pl.BlockSpec(g.shape, lambda b, s: (0, 0))]
    return pl.pallas_call(
        functools.partial(_outproj_body, n_o=len(o_list)), grid=(B, ns), in_specs=in_specs,
        out_specs=pl.BlockSpec((tm, D), lambda b, s: (b * ns + s, 0)),
        out_shape=jax.ShapeDtypeStruct((M, D), F32),
        compiler_params=_cparams("arbitrary", "arbitrary"), name="outproj")(*ins)


def _mlp_body(x_ref, g2_ref, w1_ref, w2_ref, g3_ref, wg_ref, p_ref, wp_ref, out_ref, h_ref, acc_ref):
    f = pl.program_id(1)

    @pl.when(f == 0)
    def _():
        h_ref[...] = _rms(x_ref[...], g2_ref[...]).astype(BF)
        acc_ref[...] = jnp.zeros_like(acc_ref)

    a = jnp.maximum(_dot(h_ref[...], w1_ref[...]), 0.0)
    acc_ref[...] += _dot((a * a).astype(BF), w2_ref[...])

    @pl.when(f == pl.num_programs(1) - 1)
    def _():
        x2 = x_ref[...] + _rms(acc_ref[...], g3_ref[...])
        gate = jax.nn.sigmoid(_dot(x2.astype(BF), wg_ref[...]))
        out_ref[...] = x2 + gate * _dot(p_ref[...].astype(BF), wp_ref[...])


def _mlp_ple(x2d, g2, w1, w2, g3, wg, p2d, wp, *, tm=1024, tf=512):
    M, D = x2d.shape
    FF = w1.shape[1]
    PD = p2d.shape[1]
    const = lambda a: pl.BlockSpec(a.shape, lambda m, f: (0, 0))
    return pl.pallas_call(
        _mlp_body, grid=(M // tm, FF // tf),
        in_specs=[pl.BlockSpec((tm, D), lambda m, f: (m, 0)), const(g2),
                  pl.BlockSpec((D, tf), lambda m, f: (0, f)), pl.BlockSpec((tf, D), lambda m, f: (f, 0)),
                  const(g3), const(wg), pl.BlockSpec((tm, PD), lambda m, f: (m, 0)), const(wp)],
        out_specs=pl.BlockSpec((tm, D), lambda m, f: (m, 0)),
        out_shape=jax.ShapeDtypeStruct((M, D), F32),
        scratch_shapes=[pltpu.VMEM((tm, D), BF), pltpu.VMEM((tm, D), F32)],
        compiler_params=_cparams("arbitrary", "arbitrary"), name="mlp_ple")(x2d, g2, w1, w2, g3, wg, p2d, wp)


def _t5_bucket(dist):
    n = jnp.maximum(dist, 0)
    max_exact = REL_BUCKETS // 2
    nf = jnp.maximum(n, 1).astype(F32)
    large = max_exact + (jnp.log(nf / max_exact) / math.log(REL_MAX_DIST / max_exact)
                         * (REL_BUCKETS - max_exact)).astype(jnp.int32)
    large = jnp.minimum(large, REL_BUCKETS - 1)
    return jnp.where(n < max_exact, n, large)


def _bias_by_distance(rel_bias):
    tb = rel_bias[_t5_bucket(jnp.arange(REL_TABLE_LEN))].astype(F32)
    return ((tb - rel_bias[REL_BUCKETS - 1].astype(F32)[None, :]) * LOG2E).T


def _toeplitz(fn, rows, cols, off, row_stride=1):
    n = row_stride * rows + cols
    d = jnp.arange(n)
    d = jnp.where(d < cols, d, d - n)
    v = fn(d + off)
    flat = jnp.tile(v, (1,) * (v.ndim - 1) + (rows,))[..., :rows * (n - row_stride)]
    return flat.reshape(v.shape[:-1] + (rows, n - row_stride))[..., :cols]


def _bias_fn(tbs, hi_limit=None):
    def fn(d):
        val = jnp.zeros((1,) + d.shape, F32) if tbs is None else tbs[:, jnp.clip(d, 0, REL_TABLE_LEN - 1)]
        bad = d < 0
        if hi_limit is not None:
            bad = bad | (d >= hi_limit)
        return jnp.where(bad[None, :], MASK_VALUE, val)
    return fn


def _near_tables(tbs, tq, offsets, hi_limit=None):
    return jnp.stack([_toeplitz(_bias_fn(tbs, hi_limit), ATT_TK, tq, -a * ATT_TK) for a in offsets], axis=1)


def _cmp_band_table(tbs):
    tq = ATT_TQ
    band = 2 * (tq // NSA_CMP_STRIDE)
    return _toeplitz(_bias_fn(tbs), band, tq, tq - (NSA_CMP_LEN - 1), row_stride=NSA_CMP_STRIDE)


def _nsa_q_perm():
    cols = []
    for p in range(NSA_GROUPS // 2):
        for r in range(NSA_REP):
            for g in (2 * p, 2 * p + 1):
                h = g * NSA_REP + r
                cols.extend(range(h * HEAD_DIM, (h + 1) * HEAD_DIM))
    return jnp.asarray(cols, jnp.int32)


def _row(v):
    return v.reshape(1, -1).astype(F32)


def kernel(x, p, rel_bias, norm_g, mlp_w1, mlp_w2, ple_w, ple_gate_w, da_w_in, da_lambda, da_subln, da_w_out,
           nsa_w_in, nsa_cmp_pe, nsa_cmp_w1, nsa_cmp_w2, nsa_w_out, fox_w_in, fox_b_f, fox_w_out):
    B, S, D = x.shape
    depth = p.shape[0]
    scale = HEAD_DIM ** -0.5 * LOG2E
    tbs = _bias_by_distance(rel_bias)
    da_near = tuple(range(-1, DA_TQ // ATT_TK))
    da_bias = _near_tables(tbs, DA_TQ, da_near)
    x2d = x.reshape(B * S, D)
    ia = ib = ic = 0
    for i in range(depth):
        g = norm_g[i]
        kind = i % N_MIXERS
        if kind == 0:
            lam_init = 0.8 - 0.6 * math.exp(-0.3 * i)
            w_in = da_w_in[ia]
            wn = jnp.concatenate([w_in[:, :D] * scale, w_in[:, D:2 * D]], axis=1).astype(BF)
            wt = w_in[:, 2 * D:].T.astype(BF)
            qk, vt = _inproj(x2d, _row(g[0]), wn, jnp.zeros((1, 2 * D), F32), wt, B=B, S=S)
            ot = _attention(qk, qk, vt, da_bias, B=B, S=S, n_prog=DA_HEADS, q_spec=(LANES, lambda h: h),
                            k_blk=lambda h: DA_HEADS + h, v_blk=lambda h: h, bias_blk=lambda h: h, nq_stack=2,
                            vdim=2 * HEAD_DIM, mode="causal", out_rows=2 * HEAD_DIM, out_dtype=BF, q_mask="da",
                            tq=DA_TQ, near=[(da_near[0] - 1, None)] + [(a, n) for n, a in enumerate(da_near)],
                            final="da", lam=da_lambda[ia].astype(F32), subg=da_subln[ia].reshape(-1, 1).astype(F32),
                            lam_init=lam_init)
            x2d = _outproj([ot], None, da_w_out[ia].astype(BF), x2d, _row(g[1]), B=B, S=S)
            ia += 1
        elif kind == 1:
            w_in = nsa_w_in[ib]
            kvd = NSA_GROUPS * HEAD_DIM
            wq = w_in[:, :D][:, _nsa_q_perm()] * scale
            kv = [w_in[:, D + a * kvd:D + (a + 1) * kvd] for a in range(6)]
            wn = jnp.concatenate([wq, kv[2], kv[4]], axis=1).astype(BF)
            wt = jnp.concatenate([kv[3], kv[5]], axis=1).T.astype(BF)
            wf32 = jnp.concatenate([kv[0], kv[1]], axis=1).astype(BF)
            wgate = w_in[:, D + 6 * kvd:].reshape(D, N_HEADS, 3).transpose(2, 1, 0).reshape(3 * N_HEADS, D).astype(BF)
            qn, vt, cmp_in, gates_t = _inproj(x2d, _row(g[0]), wn, jnp.zeros((1, wn.shape[1]), F32), wt, B=B, S=S,
                                              wf32=wf32, wg=wgate)
            n_chunk = S // NSA_CMP_STRIDE
            cm = cmp_in.reshape(B, n_chunk, NSA_CMP_STRIDE, 2, NSA_GROUPS, HEAD_DIM).transpose(3, 0, 4, 1, 2, 5)
            cm = cm.reshape(2, B, NSA_GROUPS, n_chunk, NSA_CMP_STRIDE * HEAD_DIM)
            pe = nsa_cmp_pe[ib].reshape(2, 1, NSA_CMP_LEN * HEAD_DIM).astype(F32)
            kvc = _compress(cm, pe, nsa_cmp_w1[ib].astype(BF), nsa_cmp_w2[ib].astype(BF))
            kc = jnp.concatenate([kvc[0], kvc[0]], axis=-1).astype(BF)
            vct = kvc[1].transpose(0, 1, 3, 2).astype(BF)
            jj = jnp.arange(S // NSA_SEL_LEN)[:, None]
            nn = jnp.arange(n_chunk)[None, :]
            n_cmp_blocks = (S - NSA_CMP_LEN) // NSA_CMP_STRIDE + 1
            ov = ((nn * NSA_CMP_STRIDE < (jj + 1) * NSA_SEL_LEN) & (nn * NSA_CMP_STRIDE + NSA_CMP_LEN - 1 >= jj * NSA_SEL_LEN)
                  & (nn < n_cmp_blocks)).astype(BF)
            cmp_band = _cmp_band_table(tbs)
            oc_t, selb = _cmp_attention(qn, kc, vct, cmp_band, ov, B=B, S=S, q_coloff=0)
            nsa_kw = dict(B=B, S=S, n_prog=NSA_GROUPS, q_spec=(NSA_REP * LANES, lambda h: h // 2),
                          bias_blk=lambda h: h, nq_stack=NSA_REP, vdim=HEAD_DIM, out_rows=NSA_REP * HEAD_DIM,
                          out_dtype=F32, q_mask="group_parity", tq=ATT_TQ)
            kblk0 = D // LANES
            sel_bias = _near_tables(tbs, ATT_TQ, (-1, 0))
            win_bias = _near_tables(tbs, ATT_TQ, (-2, -1, 0), hi_limit=NSA_WINDOW)
            os_t = _attention(qn, qn, vt, sel_bias, mode="causal", near=[(-1, 0), (0, 1)],
                              k_blk=lambda h: kblk0 + h // 2, v_blk=lambda h: h, selb=selb, **nsa_kw)
            ow_t = _attention(qn, qn, vt, win_bias, mode="window", near=[(-2, 0), (-1, 1), (0, 2)],
                              k_blk=lambda h: kblk0 + 2 + h // 2, v_blk=lambda h: NSA_GROUPS + h, **nsa_kw)
            x2d = _outproj([oc_t, os_t, ow_t], gates_t, nsa_w_out[ib].astype(BF), x2d, _row(g[1]), B=B, S=S)
            ib += 1
        else:
            w_in = fox_w_in[ic]
            wn = jnp.concatenate([w_in[:, :D] * scale, w_in[:, D:2 * D]], axis=1).astype(BF)
            hh = jnp.arange(N_HEADS)
            aug0 = hh * LANES + jnp.where(hh % 2 == 0, HEAD_DIM, 0)
            sel = jnp.arange(2 * D)[None, :] - aug0[:, None]
            aug = jnp.any((sel >= 0) & (sel < 3), axis=0).astype(F32).reshape(1, 2 * D)
            bn = jnp.concatenate([aug, jnp.zeros((1, 2 * D), F32)], axis=1)
            wf = w_in[:, 3 * D:]
            wf3 = jnp.concatenate([wf, wf, wf, jnp.zeros((D, LANES - 3 * N_HEADS), F32)], axis=1).astype(BF)
            bf3 = jnp.concatenate([fox_b_f[ic]] * 3 + [jnp.zeros((LANES - 3 * N_HEADS,), F32)]).reshape(1, LANES).astype(F32)
            rows = jnp.arange(LANES)[:, None]
            tgt = jnp.where(rows < 3 * N_HEADS, aug0[rows % N_HEADS] + rows // N_HEADS, -1)
            pm = jnp.where(jnp.arange(2 * D)[None, :] == tgt, -1.0, 0.0)
            qk, vt = _inproj(x2d, _row(g[0]), wn, bn, w_in[:, 2 * D:3 * D].T.astype(BF), B=B, S=S,
                             fox_extra=[wf3, bf3, pm.astype(BF)])
            fox_near = tuple(range(FOX_TQ // ATT_TK))
            mask_t = _near_tables(None, FOX_TQ, fox_near)
            ot = _attention(qk, qk, vt, mask_t, B=B, S=S, n_prog=N_HEADS, q_spec=(LANES, lambda h: h),
                            k_blk=lambda h: N_HEADS + h, v_blk=lambda h: h, bias_blk=lambda h: 0, nq_stack=1,
                            vdim=HEAD_DIM, mode="causal", out_rows=HEAD_DIM, out_dtype=BF, q_mask="none",
                            tq=FOX_TQ, near=[(a, a) for a in fox_near])
            x2d = _outproj([ot], None, fox_w_out[ic].astype(BF), x2d, _row(g[1]), B=B, S=S)
            ic += 1
        x2d = _mlp_ple(x2d, _row(g[2]), mlp_w1[i].astype(BF), mlp_w2[i].astype(BF), _row(g[3]),
                       ple_gate_w[i].astype(BF), p[i].reshape(B * S, -1), ple_w[i].astype(BF))
    return x2d.reshape(B, S, D)
```

```python
import functools
import math

import jax
import jax.numpy as jnp
from jax import lax
from jax.experimental import pallas as pl
from jax.experimental.pallas import tpu as pltpu

BF = jnp.bfloat16
F32 = jnp.float32

D_MODEL = 1024
HEAD_DIM = 64
LANES = 128
NORM_EPS = 1e-6
MASK_VALUE = -1e30
REL_BUCKETS = 32
REL_MAX_DIST = 128
REL_TABLE_LEN = 512
N_HEADS = D_MODEL // HEAD_DIM
DA_HEADS = N_HEADS // 2
NSA_GROUPS = 4
NSA_REP = N_HEADS // NSA_GROUPS
NSA_CMP_LEN = 32
NSA_CMP_STRIDE = 16
NSA_SEL_LEN = 64
NSA_TOP_N = 16
NSA_WINDOW = 512
NSA_FORCE_SCORE = 1e4
N_MIXERS = 3

ATT_TQ = 256
DA_TQ = 512
FOX_TQ = 1024
CAUSAL_GROUP = 4
ATT_TK = 256
VMEM_LIMIT = 56 * 1024 * 1024
CMP_ROW_STEP = 128
PV_CHUNK = 256
ONES_ROWS = 16
LOG2E = math.log2(math.e)


def _cparams(*sem):
    return pltpu.CompilerParams(dimension_semantics=sem, vmem_limit_bytes=VMEM_LIMIT)


def _rms(x, g):
    return x * lax.rsqrt(jnp.mean(x * x, axis=-1, keepdims=True) + NORM_EPS) * g


def _dot(a, b):
    return jnp.dot(a, b, preferred_element_type=F32)


def _dot_nt(a, b):
    return lax.dot_general(a, b, (((1,), (1,)), ((), ())), preferred_element_type=F32)


def _dot_tn(a, b):
    return lax.dot_general(a, b, (((0,), (0,)), ((), ())), preferred_element_type=F32)


def _inproj_body(*refs, n_f32, n_gate, fox, tm, col_chunk):
    it = iter(refs)
    x_ref, g_ref, wn_ref, bn_ref, wt_ref = (next(it) for _ in range(5))
    wf32_ref = next(it) if n_f32 else None
    wg_ref = next(it) if n_gate else None
    if fox:
        wf_ref, bf_ref, pm_ref = next(it), next(it), next(it)
    on_ref, ot_ref = next(it), next(it)
    of32_ref = next(it) if n_f32 else None
    og_ref = next(it) if n_gate else None
    carry_ref = next(it) if fox else None

    h = _rms(x_ref[...], g_ref[...]).astype(BF)
    n_nat = on_ref.shape[1]
    k_off = n_nat // 2
    if fox:
        @pl.when(pl.program_id(1) == 0)
        def _():
            carry_ref[...] = jnp.zeros_like(carry_ref)

        lf = _dot(h, wf_ref[...]) + bf_ref[...]
        lane = lax.broadcasted_iota(jnp.int32, lf.shape, 1)
        row = lax.broadcasted_iota(jnp.int32, lf.shape, 0)
        ls = jnp.minimum(lf, 0.0) - jnp.log1p(jnp.exp(-jnp.abs(lf)))
        c = jnp.where(lane < 3 * N_HEADS, ls, 0.0)
        k = 1
        while k < tm:
            c = c + jnp.where(row >= k, pltpu.roll(c, k, axis=0), 0.0)
            k *= 2
        c = c + carry_ref[...]
        carry_ref[...] = c[tm - 1:tm, :]
        c = c * LOG2E
        hi = c.astype(BF).astype(F32)
        r1 = c - hi
        mid = r1.astype(BF).astype(F32)
        lo = (r1 - mid).astype(BF).astype(F32)
        c3 = jnp.where(lane < N_HEADS, hi, jnp.where(lane < 2 * N_HEADS, mid, lo)).astype(BF)
    if fox:
        lane_b = lax.broadcasted_iota(jnp.int32, (tm, LANES), 1)
        d_c = wn_ref.shape[1] // 2
        for part in range(2):
            yc = _dot(h, wn_ref[:, part * d_c:(part + 1) * d_c])
            for oc in range(0, k_off, col_chunk):
                pieces = []
                for hh in range(oc // LANES, (oc + col_chunk) // LANES):
                    blk = yc[:, (hh // 2) * LANES:(hh // 2 + 1) * LANES]
                    keep = (lane_b < HEAD_DIM) if hh % 2 == 0 else (lane_b >= HEAD_DIM)
                    pieces.append(jnp.where(keep, blk, 0.0))
                y = jnp.concatenate(pieces, axis=1) + bn_ref[:, part * k_off + oc:part * k_off + oc + col_chunk]
                if part == 1:
                    y = y + _dot(c3, pm_ref[:, oc:oc + col_chunk])
                on_ref[:, part * k_off + oc:part * k_off + oc + col_chunk] = y.astype(on_ref.dtype)
    else:
        for c0 in range(0, n_nat, col_chunk):
            y = _dot(h, wn_ref[:, c0:c0 + col_chunk]) + bn_ref[:, c0:c0 + col_chunk]
            on_ref[:, c0:c0 + col_chunk] = y.astype(on_ref.dtype)
    ot_ref[0] = _dot_nt(wt_ref[...], h).astype(ot_ref.dtype)
    if n_f32:
        of32_ref[...] = _dot(h, wf32_ref[...])
    if n_gate:
        og_ref[0] = jax.nn.sigmoid(_dot_nt(wg_ref[...], h))


def _inproj(x2d, g, wn, bn, wt, *, B, S, tm=512, wf32=None, wg=None, fox_extra=None):
    M, D = x2d.shape
    ns = S // tm
    n_nat, n_t = bn.shape[1], wt.shape[0]
    full = lambda a: pl.BlockSpec(a.shape, lambda b, s: (0,) * a.ndim)
    ins = [x2d, g, wn, bn, wt]
    in_specs = [pl.BlockSpec((tm, D), lambda b, s: (b * ns + s, 0)), full(g), full(wn), full(bn), full(wt)]
    out_shape = [jax.ShapeDtypeStruct((M, n_nat), BF), jax.ShapeDtypeStruct((B, n_t, S), BF)]
    out_specs = [pl.BlockSpec((tm, n_nat), lambda b, s: (b * ns + s, 0)),
                 pl.BlockSpec((1, n_t, tm), lambda b, s: (b, 0, s))]
    scratch = []
    if wf32 is not None:
        ins.append(wf32)
        in_specs.append(full(wf32))
        out_shape.append(jax.ShapeDtypeStruct((M, wf32.shape[1]), F32))
        out_specs.append(pl.BlockSpec((tm, wf32.shape[1]), lambda b, s: (b * ns + s, 0)))
    if wg is not None:
        ins.append(wg)
        in_specs.append(full(wg))
        out_shape.append(jax.ShapeDtypeStruct((B, wg.shape[0], S), F32))
        out_specs.append(pl.BlockSpec((1, wg.shape[0], tm), lambda b, s: (b, 0, s)))
    if fox_extra is not None:
        for a in fox_extra:
            ins.append(a)
            in_specs.append(full(a))
        scratch.append(pltpu.VMEM((1, LANES), F32))
    body = functools.partial(_inproj_body, n_f32=wf32 is not None, n_gate=wg is not None,
                             fox=fox_extra is not None, tm=tm, col_chunk=1024 if n_nat % 1024 == 0 else 512)
    return pl.pallas_call(
        body, grid=(B, ns), in_specs=in_specs, out_specs=out_specs, out_shape=out_shape,
        scratch_shapes=scratch, compiler_params=_cparams("arbitrary", "arbitrary"), name="inproj")(*ins)


def _attn_body(*refs, tq, grp, nq_stack, vdim, mode, use_sel, final, lam_init, q_mask, near):
    it = iter(refs)
    q_ref, qn_ref, k_ref, vt_ref, bias_ref = (next(it) for _ in range(5))
    selb_ref, selbn_ref = (next(it), next(it)) if use_sel else (None, None)
    if final == "da":
        lam_ref, subg_ref = next(it), next(it)
    o_ref = next(it)
    qm_ref, m_ref, acc_ref, s_scr, mt_scr, r0_ref = (next(it) for _ in range(6))
    tk = ATT_TK
    lw = nq_stack * tq
    i = pl.program_id(2)
    i_next = jnp.minimum(i + 1, pl.num_programs(2) - 1)

    lane = lax.broadcasted_iota(jnp.int32, (tq, LANES), 1)
    for slot, src_ref in enumerate((q_ref, qn_ref)):
        qt = src_ref[...]
        for r in range(nq_stack):
            blk = qt[:, 0:LANES] if q_mask == "da" else qt[:, r * LANES:(r + 1) * LANES]
            if q_mask == "da":
                keep = (lane < HEAD_DIM) if r == 0 else (lane >= HEAD_DIM)
            elif q_mask == "group_parity":
                keep = (lane // HEAD_DIM) == (pl.program_id(1) % 2)
            else:
                keep = None
            if keep is not None:
                blk = jnp.where(keep, blk, jnp.zeros_like(blk))
            qm_ref[slot, r * tq:(r + 1) * tq, :] = blk
    m_ref[...] = jnp.full_like(m_ref, MASK_VALUE)
    acc_ref[...] = jnp.zeros_like(acc_ref)

    def stage_qk(group, r, next_tile=False):
        rows = len(group) * tk
        ks = [k_ref[pl.ds(pl.multiple_of(kt * tk, tk), tk), :] for kt, _, _ in group]
        s = _dot_nt(ks[0] if len(ks) == 1 else jnp.concatenate(ks, axis=0), qm_ref[int(next_tile)])
        if use_sel:
            slabs = []
            for t, (kt, _, _) in enumerate(group):
                sb8 = (selbn_ref if next_tile else selb_ref)[0, 0, pl.ds(pl.multiple_of((kt // 2) * 8, 8), 8), :]
                sb4 = jnp.where(kt % 2 == 0, sb8[0:4, :], sb8[4:8, :])
                sbt = jnp.concatenate([sb4] * nq_stack, axis=1)
                for j in range(tk // NSA_SEL_LEN):
                    r0 = t * tk + j * NSA_SEL_LEN
                    slabs.append(s[r0:r0 + NSA_SEL_LEN, :] + sbt[j:j + 1, :])
            s = jnp.concatenate(slabs, axis=0)
        s_scr[r, 0:rows, :] = s
        mt_scr[r] = jnp.max(s, axis=0, keepdims=True)

    def stage_softmax_pv(group, r):
        rows = len(group) * tk
        plain = all(bidx is None and fm is None for _, bidx, fm in group)
        vts = [vt_ref[0, :, pl.ds(pl.multiple_of(kt * tk, tk), tk)] for kt, _, _ in group]
        lhs = jnp.concatenate([vts[0] if len(vts) == 1 else jnp.concatenate(vts, axis=1),
                               jnp.ones((ONES_ROWS, rows), BF)], axis=0)
        for c0 in range(0, lw, PV_CHUNK):
            cols = slice(c0, c0 + PV_CHUNK)
            if plain:
                s = s_scr[r, 0:rows, cols]
                mt = mt_scr[r, :, cols]
            else:
                parts = []
                for t, (_, bidx, fm) in enumerate(group):
                    st = s_scr[r, t * tk:(t + 1) * tk, cols]
                    if bidx is not None:
                        q, l0 = divmod(c0, tq)
                        st = st + bias_ref[q, bidx, :, l0:l0 + PV_CHUNK]
                    if fm is not None:
                        st = st + jnp.where(fm, MASK_VALUE, 0.0)
                    parts.append(st)
                s = parts[0] if len(parts) == 1 else jnp.concatenate(parts, axis=0)
                mt = jnp.max(s, axis=0, keepdims=True)
            m_old = m_ref[:, cols]
            m_new = jnp.maximum(m_old, mt)
            alpha = jnp.exp2(m_old - m_new)
            p = jnp.exp2(s - m_new).astype(BF)
            m_ref[:, cols] = m_new
            acc_ref[:, cols] = alpha * acc_ref[:, cols] + _dot(lhs, p)

    def trip(r, cur, nxt, next_tile=False):
        if not isinstance(r, int):
            for val in (0, 1):
                @pl.when(r == val)
                def _(val=val):
                    trip(val, cur, nxt, next_tile)
            return
        if nxt:
            stage_qk(nxt, 1 - r, next_tile)
        stage_softmax_pv(cur, r)

    n_tile = tq // tk
    a0 = near[0][0]
    last_kt = k_ref.shape[0] // tk - 1
    lo_values = sorted({max(ii * n_tile + a0, 0) % grp for ii in range(2 * grp)}) if mode == "causal" else [0]

    def tile_plan(ii):
        base = ii * n_tile
        near_tiles = [(jnp.maximum(base + a, 0), bidx, (base + a < 0) if a < 0 else None) for a, bidx in near]
        n_far = jnp.maximum(base + a0, 0) if mode == "causal" else 0
        trips, lo = n_far // grp, n_far % grp
        tails = {v: [(n_far - v + t, None, None) for t in range(v)] + near_tiles for v in lo_values}
        first = []
        for t in range(grp):
            kt_near = sum(jnp.where(lo == v, tails[v][min(t, len(tails[v]) - 1)][0], 0) for v in lo_values)
            first.append((jnp.where(trips > 0, t, kt_near), None, None))
        return trips, lo, tails, first

    trips, lo, tails, first = tile_plan(i)
    far_group = lambda u: [(jnp.minimum(grp * u + t, last_kt), None, None) for t in range(grp)]
    if mode == "causal":
        r0 = jnp.where(i == 0, 0, r0_ref[0])

        @pl.when(i == 0)
        def _():
            stage_qk(first, 0)

        def body(u, c):
            trip((r0 + u) % 2, far_group(u), far_group(u + 1))
            return c
        lax.fori_loop(0, trips, body, 0)
        first_next = tile_plan(i_next)[3]
    else:
        r0 = 0
        stage_qk(first, 0)

    def run_tail(tail):
        groups = [tail[n:n + grp] for n in range(0, len(tail), grp)]
        for e, group in enumerate(groups):
            r_e = (r0 + trips + e) % 2
            if e + 1 < len(groups):
                trip(r_e, group, groups[e + 1])
            elif mode == "causal":
                trip(r_e, group, first_next, next_tile=True)
                r0_ref[0] = 1 - r_e
            else:
                trip(r_e, group, None)

    if len(lo_values) == 1:
        run_tail(tails[lo_values[0]])
    else:
        for v in lo_values:
            @pl.when(lo == v)
            def _(v=v):
                run_tail(tails[v])

    acc = acc_ref[0:vdim, :]
    l = acc_ref[vdim:vdim + 1, :]
    if final == "da":
        lamv = lam_ref[...]
        lam = (jnp.exp(jnp.sum(lamv[0:1] * lamv[1:2], axis=1, keepdims=True))
               - jnp.exp(jnp.sum(lamv[2:3] * lamv[3:4], axis=1, keepdims=True)) + lam_init)
        o = acc[:, :tq] / l[:, :tq] - lam * (acc[:, tq:] / l[:, tq:])
        o = o * lax.rsqrt(jnp.mean(o * o, axis=0, keepdims=True) + NORM_EPS) * subg_ref[...] * (1.0 - lam_init)
        o_ref[0] = o.astype(o_ref.dtype)
    else:
        for r in range(nq_stack):
            o_ref[0, r * vdim:(r + 1) * vdim, :] = (acc[:, r * tq:(r + 1) * tq] / l[:, r * tq:(r + 1) * tq]).astype(o_ref.dtype)


def _attention(q_arr, k_arr, vt, bias, *, B, S, n_prog, q_spec, k_blk, v_blk, bias_blk, nq_stack, vdim, mode,
               out_rows, out_dtype, q_mask, tq, near, grp=2, selb=None, final="plain", lam=None, subg=None, lam_init=0.0):
    tk = ATT_TK
    nq = S // tq
    lw = nq_stack * tq
    q_w, q_blk = q_spec
    nxt = lambda i: jnp.minimum(i + 1, nq - 1)
    ins = [q_arr, q_arr, k_arr, vt, bias]
    in_specs = [
        pl.BlockSpec((tq, q_w), lambda b, h, i: (b * nq + i, q_blk(h))),
        pl.BlockSpec((tq, q_w), lambda b, h, i: (b * nq + nxt(i), q_blk(h))),
        pl.BlockSpec((S, LANES), lambda b, h, i: (b, k_blk(h))),
        pl.BlockSpec((1, vdim, S), lambda b, h, i: (b, v_blk(h), 0)),
        pl.BlockSpec((nq_stack,) + bias.shape[1:], lambda b, h, i: (bias_blk(h), 0, 0, 0)),
    ]
    if selb is not None:
        ins += [selb, selb]
        in_specs += [pl.BlockSpec((1, 1, selb.shape[2], tq), lambda b, h, i: (b, h, 0, i)),
                     pl.BlockSpec((1, 1, selb.shape[2], tq), lambda b, h, i: (b, h, 0, nxt(i)))]
    if final == "da":
        ins += [lam, subg]
        in_specs += [pl.BlockSpec(lam.shape, lambda b, h, i: (0, 0)), pl.BlockSpec(subg.shape, lambda b, h, i: (0, 0))]
    body = functools.partial(_attn_body, tq=tq, grp=grp, nq_stack=nq_stack, vdim=vdim, mode=mode, use_sel=selb is not None,
                             final=final, lam_init=lam_init, q_mask=q_mask, near=near)
    return pl.pallas_call(
        body, grid=(B, n_prog, nq), in_specs=in_specs,
        out_specs=pl.BlockSpec((1, out_rows, tq), lambda b, h, i: (b, h, i)),
        out_shape=jax.ShapeDtypeStruct((B, out_rows * n_prog, S), out_dtype),
        scratch_shapes=[pltpu.VMEM((2, lw, LANES), BF), pltpu.VMEM((1, lw), F32), pltpu.VMEM((vdim + ONES_ROWS, lw), F32),
                        pltpu.VMEM((2, grp * tk, lw), F32), pltpu.VMEM((2, 1, lw), F32), pltpu.SMEM((1,), jnp.int32)],
        compiler_params=_cparams("arbitrary", "arbitrary", "arbitrary"), name="attn_" + mode + "_" + final)(*ins)


def _compress_body(x_ref, pe_ref, w1_ref, w2_ref, o_ref):
    x = x_ref[0, 0, 0]
    n, half = x.shape
    pe = pe_ref[0]
    first = _dot((x + pe[:, :half]).astype(BF), w1_ref[0, :half, :])
    second = _dot((x + pe[:, half:]).astype(BF), w1_ref[0, half:, :])
    hcur = first + pltpu.roll(second, n - 1, axis=0)
    hcur = 0.5 * hcur * (1.0 + jnp.tanh(math.sqrt(2.0 / math.pi) * (hcur + 0.044715 * (hcur * hcur * hcur))))
    o_ref[0, 0, 0] = _dot(hcur.astype(BF), w2_ref[0])


def _compress(chunks, pe, w1, w2):
    _, B, G, n, K = chunks.shape
    return pl.pallas_call(
        _compress_body, grid=(2, B, G),
        in_specs=[pl.BlockSpec((1, 1, 1, n, K), lambda a, b, g: (a, b, g, 0, 0)),
                  pl.BlockSpec((1, 1, 2 * K), lambda a, b, g: (a, 0, 0)),
                  pl.BlockSpec((1,) + w1.shape[1:], lambda a, b, g: (a, 0, 0)),
                  pl.BlockSpec((1,) + w2.shape[1:], lambda a, b, g: (a, 0, 0))],
        out_specs=pl.BlockSpec((1, 1, 1, n, HEAD_DIM), lambda a, b, g: (a, b, g, 0, 0)),
        out_shape=jax.ShapeDtypeStruct((2, B, G, n, HEAD_DIM), F32),
        compiler_params=_cparams("arbitrary", "arbitrary", "arbitrary"), name="nsa_compress")(chunks, pe, w1, w2)


def _cmp_body(q_ref, kc_ref, vct_ref, bc_ref, ov_ref, o_ref, selb_ref, s_ref, imp_ref):
    tq = ATT_TQ
    i = pl.program_id(2)
    n_cmp = kc_ref.shape[2]
    n_sel = selb_ref.shape[2]
    band = bc_ref.shape[1]
    bc = jnp.concatenate([bc_ref[r] for r in range(NSA_REP)], axis=1)
    qt = q_ref[...]
    lane = lax.broadcasted_iota(jnp.int32, (tq, LANES), 1)
    keep = (lane // HEAD_DIM) == (pl.program_id(1) % 2)
    qm = jnp.concatenate(
        [jnp.where(keep, qt[:, r * LANES:(r + 1) * LANES], jnp.zeros((tq, LANES), BF)) for r in range(NSA_REP)], axis=0)

    def attend(rows):
        s = _dot_nt(kc_ref[0, 0, 0:rows, :], qm)
        row = lax.broadcasted_iota(jnp.int32, s.shape, 0)
        s_ref[0:rows, :] = jnp.where(row < (band // 2) * (i + 1), s, MASK_VALUE)

        @pl.when(i == 0)
        def _():
            s_ref[0:band // 2, :] = s_ref[0:band // 2, :] + bc[band // 2:, :]

        @pl.when(i > 0)
        def _():
            r0 = pl.multiple_of((band // 2) * (i - 1), band // 2)
            s_ref[pl.ds(r0, band), :] = s_ref[pl.ds(r0, band), :] + bc

        ps = None
        for r in range(NSA_REP):
            s = s_ref[0:rows, r * tq:(r + 1) * tq]
            m = jnp.max(s, axis=0, keepdims=True)
            e = jnp.exp2(s - m)
            l = jnp.sum(e, axis=0, keepdims=True)
            inv = jnp.where(m > 0.5 * MASK_VALUE, 1.0 / l, 0.0)
            p = e * inv
            o_ref[0, r * HEAD_DIM:(r + 1) * HEAD_DIM, :] = _dot(vct_ref[0, 0, :, 0:rows], p.astype(BF))
            ps = p if ps is None else ps + p
        p_hi = ps.astype(BF)
        r1 = ps - p_hi.astype(F32)
        p_mid = r1.astype(BF)
        p_lo = (r1 - p_mid.astype(F32)).astype(BF)
        ov = ov_ref[:, 0:rows]
        imp_ref[...] = _dot(ov, p_hi) + _dot(ov, p_mid) + _dot(ov, p_lo)

    step = min(CMP_ROW_STEP, n_cmp)
    tiles_per_step = step // (band // 2)
    for v in range(n_cmp // step):
        @pl.when((i >= v * tiles_per_step) & (i < (v + 1) * tiles_per_step))
        def _(v=v):
            attend((v + 1) * step)

    imp = imp_ref[...]
    j = lax.broadcasted_iota(jnp.int32, (n_sel, tq), 0)
    t = i * tq + lax.broadcasted_iota(jnp.int32, (n_sel, tq), 1)
    cur = t // NSA_SEL_LEN
    forced = (j == 0) | (j == cur) | (j == cur - 1)
    w = jnp.where(j <= cur, jnp.where(forced, NSA_FORCE_SCORE, imp), -1.0)
    jf = j.astype(F32)
    for _ in range(min(NSA_TOP_N, n_sel)):
        mx = jnp.max(w, axis=0, keepdims=True)
        idx = jnp.min(jnp.where(w == mx, jf, float(n_sel)), axis=0, keepdims=True)
        w = jnp.where(jf == idx, -2.0, w)
    selb_ref[0, 0] = jnp.where(w == -2.0, 0.0, MASK_VALUE)


def _cmp_attention(qn, kc, vct, bc, ov, *, B, S, q_coloff):
    tq = ATT_TQ
    nq = S // tq
    n_cmp = kc.shape[2]
    n_sel = S // NSA_SEL_LEN
    return pl.pallas_call(
        _cmp_body, grid=(B, NSA_GROUPS, nq),
        in_specs=[pl.BlockSpec((tq, NSA_REP * LANES), lambda b, g, i: (b * nq + i, q_coloff + g // 2)),
                  pl.BlockSpec((1, 1, n_cmp, LANES), lambda b, g, i: (b, g, 0, 0)),
                  pl.BlockSpec((1, 1, HEAD_DIM, n_cmp), lambda b, g, i: (b, g, 0, 0)),
                  pl.BlockSpec((NSA_REP,) + bc.shape[1:], lambda b, g, i: (g, 0, 0)),
                  pl.BlockSpec(ov.shape, lambda b, g, i: (0, 0))],
        out_specs=[pl.BlockSpec((1, NSA_REP * HEAD_DIM, tq), lambda b, g, i: (b, g, i)),
                   pl.BlockSpec((1, 1, n_sel, tq), lambda b, g, i: (b, g, 0, i))],
        out_shape=[jax.ShapeDtypeStruct((B, D_MODEL, S), F32), jax.ShapeDtypeStruct((B, NSA_GROUPS, n_sel, S), F32)],
        scratch_shapes=[pltpu.VMEM((n_cmp, NSA_REP * tq), F32), pltpu.VMEM((n_sel, tq), F32)],
        compiler_params=_cparams("arbitrary", "arbitrary", "arbitrary"), name="nsa_cmp_topk")(qn, kc, vct, bc, ov)


def _outproj_body(*refs, n_o):
    it = iter(refs)
    o_refs = [next(it) for _ in range(n_o)]
    gt_ref = next(it) if n_o > 1 else None
    w_ref, x_ref, g_ref, out_ref = (next(it) for _ in range(4))
    if n_o == 1:
        ot = o_refs[0][0]
    else:
        gts = gt_ref[0]
        parts = []
        for hd in range(N_HEADS):
            rows = slice(hd * HEAD_DIM, (hd + 1) * HEAD_DIM)
            acc = gts[hd:hd + 1, :] * o_refs[0][0, rows, :]
            for b in range(1, n_o):
                acc = acc + gts[b * N_HEADS + hd:b * N_HEADS + hd + 1, :] * o_refs[b][0, rows, :]
            parts.append(acc.astype(BF))
        ot = jnp.concatenate(parts, axis=0)
    y = _dot_tn(ot, w_ref[...])
    out_ref[...] = x_ref[...] + _rms(y, g_ref[...])


def _outproj(o_list, gt, w, x2d, g, *, B, S, tm=512):
    M, D = x2d.shape
    ns = S // tm
    ins = list(o_list)
    in_specs = [pl.BlockSpec((1, D, tm), lambda b, s: (b, 0, s)) for _ in o_list]
    if gt is not None:
        ins.append(gt)
        in_specs.append(pl.BlockSpec((1, gt.shape[1], tm), lambda b, s: (b, 0, s)))
    ins += [w, x2d, g]
    in_specs += [pl.BlockSpec(w.shape, lambda b, s: (0, 0)),
                 pl.BlockSpec((tm, D), lambda b, s: (b * ns + s, 0)),
                 pl.BlockSpec(g.shape, lambda b, s: (0, 0))]
    return pl.pallas_call(
        functools.partial(_outproj_body, n_o=len(o_list)), grid=(B, ns), in_specs=in_specs,
        out_specs=pl.BlockSpec((tm, D), lambda b, s: (b * ns + s, 0)),
        out_shape=jax.ShapeDtypeStruct((M, D), F32),
        compiler_params=_cparams("arbitrary", "arbitrary"), name="outproj")(*ins)


def _mlp_body(x_ref, g2_ref, w1_ref, w2_ref, g3_ref, wg_ref, p_ref, wp_ref, out_ref, h_ref, acc_ref):
    f = pl.program_id(1)

    @pl.when(f == 0)
    def _():
        h_ref[...] = _rms(x_ref[...], g2_ref[...]).astype(BF)
        acc_ref[...] = jnp.zeros_like(acc_ref)

    a = jnp.maximum(_dot(h_ref[...], w1_ref[...]), 0.0)
    acc_ref[...] += _dot((a * a).astype(BF), w2_ref[...])

    @pl.when(f == pl.num_programs(1) - 1)
    def _():
        x2 = x_ref[...] + _rms(acc_ref[...], g3_ref[...])
        gate = jax.nn.sigmoid(_dot(x2.astype(BF), wg_ref[...]))
        out_ref[...] = x2 + gate * _dot(p_ref[...].astype(BF), wp_ref[...])


def _mlp_ple(x2d, g2, w1, w2, g3, wg, p2d, wp, *, tm=1024, tf=512):
    M, D = x2d.shape
    FF = w1.shape[1]
    PD = p2d.shape[1]
    const = lambda a: pl.BlockSpec(a.shape, lambda m, f: (0, 0))
    return pl.pallas_call(
        _mlp_body, grid=(M // tm, FF // tf),
        in_specs=[pl.BlockSpec((tm, D), lambda m, f: (m, 0)), const(g2),
                  pl.BlockSpec((D, tf), lambda m, f: (0, f)), pl.BlockSpec((tf, D), lambda m, f: (f, 0)),
                  const(g3), const(wg), pl.BlockSpec((tm, PD), lambda m, f: (m, 0)), const(wp)],
        out_specs=pl.BlockSpec((tm, D), lambda m, f: (m, 0)),
        out_shape=jax.ShapeDtypeStruct((M, D), F32),
        scratch_shapes=[pltpu.VMEM((tm, D), BF), pltpu.VMEM((tm, D), F32)],
        compiler_params=_cparams("arbitrary", "arbitrary"), name="mlp_ple")(x2d, g2, w1, w2, g3, wg, p2d, wp)


def _t5_bucket(dist):
    n = jnp.maximum(dist, 0)
    max_exact = REL_BUCKETS // 2
    nf = jnp.maximum(n, 1).astype(F32)
    large = max_exact + (jnp.log(nf / max_exact) / math.log(REL_MAX_DIST / max_exact)
                         * (REL_BUCKETS - max_exact)).astype(jnp.int32)
    large = jnp.minimum(large, REL_BUCKETS - 1)
    return jnp.where(n < max_exact, n, large)


def _bias_by_distance(rel_bias):
    tb = rel_bias[_t5_bucket(jnp.arange(REL_TABLE_LEN))].astype(F32)
    return ((tb - rel_bias[REL_BUCKETS - 1].astype(F32)[None, :]) * LOG2E).T


def _toeplitz(fn, rows, cols, off, row_stride=1):
    n = row_stride * rows + cols
    d = jnp.arange(n)
    d = jnp.where(d < cols, d, d - n)
    v = fn(d + off)
    flat = jnp.tile(v, (1,) * (v.ndim - 1) + (rows,))[..., :rows * (n - row_stride)]
    return flat.reshape(v.shape[:-1] + (rows, n - row_stride))[..., :cols]


def _bias_fn(tbs, hi_limit=None):
    def fn(d):
        val = jnp.zeros((1,) + d.shape, F32) if tbs is None else tbs[:, jnp.clip(d, 0, REL_TABLE_LEN - 1)]
        bad = d < 0
        if hi_limit is not None:
            bad = bad | (d >= hi_limit)
        return jnp.where(bad[None, :], MASK_VALUE, val)
    return fn


def _near_tables(tbs, tq, offsets, hi_limit=None):
    return jnp.stack([_toeplitz(_bias_fn(tbs, hi_limit), ATT_TK, tq, -a * ATT_TK) for a in offsets], axis=1)


def _cmp_band_table(tbs):
    tq = ATT_TQ
    band = 2 * (tq // NSA_CMP_STRIDE)
    return _toeplitz(_bias_fn(tbs), band, tq, tq - (NSA_CMP_LEN - 1), row_stride=NSA_CMP_STRIDE)


def _nsa_q_perm():
    cols = []
    for p in range(NSA_GROUPS // 2):
        for r in range(NSA_REP):
            for g in (2 * p, 2 * p + 1):
                h = g * NSA_REP + r
                cols.extend(range(h * HEAD_DIM, (h + 1) * HEAD_DIM))
    return jnp.asarray(cols, jnp.int32)


def _row(v):
    return v.reshape(1, -1).astype(F32)


def kernel(x, p, rel_bias, norm_g, mlp_w1, mlp_w2, ple_w, ple_gate_w, da_w_in, da_lambda, da_subln, da_w_out,
           nsa_w_in, nsa_cmp_pe, nsa_cmp_w1, nsa_cmp_w2, nsa_w_out, fox_w_in, fox_b_f, fox_w_out):
    B, S, D = x.shape
    depth = p.shape[0]
    scale = HEAD_DIM ** -0.5 * LOG2E
    tbs = _bias_by_distance(rel_bias)
    da_near = tuple(range(-1, DA_TQ // ATT_TK))
    da_bias = _near_tables(tbs, DA_TQ, da_near)
    x2d = x.reshape(B * S, D)
    ia = ib = ic = 0
    for i in range(depth):
        g = norm_g[i]
        kind = i % N_MIXERS
        if kind == 0:
            lam_init = 0.8 - 0.6 * math.exp(-0.3 * i)
            w_in = da_w_in[ia]
            wn = jnp.concatenate([w_in[:, :D] * scale, w_in[:, D:2 * D]], axis=1).astype(BF)
            wt = w_in[:, 2 * D:].T.astype(BF)
            qk, vt = _inproj(x2d, _row(g[0]), wn, jnp.zeros((1, 2 * D), F32), wt, B=B, S=S)
            ot = _attention(qk, qk, vt, da_bias, B=B, S=S, n_prog=DA_HEADS, q_spec=(LANES, lambda h: h),
                            k_blk=lambda h: DA_HEADS + h, v_blk=lambda h: h, bias_blk=lambda h: h, nq_stack=2,
                            vdim=2 * HEAD_DIM, mode="causal", out_rows=2 * HEAD_DIM, out_dtype=BF, q_mask="da",
                            tq=DA_TQ, near=[(da_near[0] - 1, None)] + [(a, n) for n, a in enumerate(da_near)], grp=CAUSAL_GROUP,
                            final="da", lam=da_lambda[ia].astype(F32), subg=da_subln[ia].reshape(-1, 1).astype(F32),
                            lam_init=lam_init)
            x2d = _outproj([ot], None, da_w_out[ia].astype(BF), x2d, _row(g[1]), B=B, S=S)
            ia += 1
        elif kind == 1:
            w_in = nsa_w_in[ib]
            kvd = NSA_GROUPS * HEAD_DIM
            wq = w_in[:, :D][:, _nsa_q_perm()] * scale
            kv = [w_in[:, D + a * kvd:D + (a + 1) * kvd] for a in range(6)]
            wn = jnp.concatenate([wq, kv[2], kv[4]], axis=1).astype(BF)
            wt = jnp.concatenate([kv[3], kv[5]], axis=1).T.astype(BF)
            wf32 = jnp.concatenate([kv[0], kv[1]], axis=1).astype(BF)
            wgate = w_in[:, D + 6 * kvd:].reshape(D, N_HEADS, 3).transpose(2, 1, 0).reshape(3 * N_HEADS, D).astype(BF)
            qn, vt, cmp_in, gates_t = _inproj(x2d, _row(g[0]), wn, jnp.zeros((1, wn.shape[1]), F32), wt, B=B, S=S,
                                              wf32=wf32, wg=wgate)
            n_chunk = S // NSA_CMP_STRIDE
            cm = cmp_in.reshape(B, n_chunk, NSA_CMP_STRIDE, 2, NSA_GROUPS, HEAD_DIM).transpose(3, 0, 4, 1, 2, 5)
            cm = cm.reshape(2, B, NSA_GROUPS, n_chunk, NSA_CMP_STRIDE * HEAD_DIM)
            pe = nsa_cmp_pe[ib].reshape(2, 1, NSA_CMP_LEN * HEAD_DIM).astype(F32)
            kvc = _compress(cm, pe, nsa_cmp_w1[ib].astype(BF), nsa_cmp_w2[ib].astype(BF))
            kc = jnp.concatenate([kvc[0], kvc[0]], axis=-1).astype(BF)
            vct = kvc[1].transpose(0, 1, 3, 2).astype(BF)
            jj = jnp.arange(S // NSA_SEL_LEN)[:, None]
            nn = jnp.arange(n_chunk)[None, :]
            n_cmp_blocks = (S - NSA_CMP_LEN) // NSA_CMP_STRIDE + 1
            ov = ((nn * NSA_CMP_STRIDE < (jj + 1) * NSA_SEL_LEN) & (nn * NSA_CMP_STRIDE + NSA_CMP_LEN - 1 >= jj * NSA_SEL_LEN)
                  & (nn < n_cmp_blocks)).astype(BF)
            cmp_band = _cmp_band_table(tbs)
            oc_t, selb = _cmp_attention(qn, kc, vct, cmp_band, ov, B=B, S=S, q_coloff=0)
            nsa_kw = dict(B=B, S=S, n_prog=NSA_GROUPS, q_spec=(NSA_REP * LANES, lambda h: h // 2),
                          bias_blk=lambda h: h, nq_stack=NSA_REP, vdim=HEAD_DIM, out_rows=NSA_REP * HEAD_DIM,
                          out_dtype=F32, q_mask="group_parity", tq=ATT_TQ)
            kblk0 = D // LANES
            sel_bias = _near_tables(tbs, ATT_TQ, (-1, 0))
            win_bias = _near_tables(tbs, ATT_TQ, (-2, -1, 0), hi_limit=NSA_WINDOW)
            os_t = _attention(qn, qn, vt, sel_bias, mode="causal", near=[(-1, 0), (0, 1)], grp=CAUSAL_GROUP,
                              k_blk=lambda h: kblk0 + h // 2, v_blk=lambda h: h, selb=selb, **nsa_kw)
            ow_t = _attention(qn, qn, vt, win_bias, mode="window", near=[(-2, 0), (-1, 1), (0, 2)],
                              k_blk=lambda h: kblk0 + 2 + h // 2, v_blk=lambda h: NSA_GROUPS + h, **nsa_kw)
            x2d = _outproj([oc_t, os_t, ow_t], gates_t, nsa_w_out[ib].astype(BF), x2d, _row(g[1]), B=B, S=S)
            ib += 1
        else:
            w_in = fox_w_in[ic]
            wn = jnp.concatenate([w_in[:, :D] * scale, w_in[:, D:2 * D]], axis=1).astype(BF)
            hh = jnp.arange(N_HEADS)
            aug0 = hh * LANES + jnp.where(hh % 2 == 0, HEAD_DIM, 0)
            sel = jnp.arange(2 * D)[None, :] - aug0[:, None]
            aug = jnp.any((sel >= 0) & (sel < 3), axis=0).astype(F32).reshape(1, 2 * D)
            bn = jnp.concatenate([aug, jnp.zeros((1, 2 * D), F32)], axis=1)
            wf = w_in[:, 3 * D:]
            wf3 = jnp.concatenate([wf, wf, wf, jnp.zeros((D, LANES - 3 * N_HEADS), F32)], axis=1).astype(BF)
            bf3 = jnp.concatenate([fox_b_f[ic]] * 3 + [jnp.zeros((LANES - 3 * N_HEADS,), F32)]).reshape(1, LANES).astype(F32)
            rows = jnp.arange(LANES)[:, None]
            tgt = jnp.where(rows < 3 * N_HEADS, aug0[rows % N_HEADS] + rows // N_HEADS, -1)
            pm = jnp.where(jnp.arange(2 * D)[None, :] == tgt, -1.0, 0.0)
            qk, vt = _inproj(x2d, _row(g[0]), wn, bn, w_in[:, 2 * D:3 * D].T.astype(BF), B=B, S=S,
                             fox_extra=[wf3, bf3, pm.astype(BF)])
            fox_near = tuple(range(FOX_TQ // ATT_TK))
            mask_t = _near_tables(None, FOX_TQ, fox_near)
            ot = _attention(qk, qk, vt, mask_t, B=B, S=S, n_prog=N_HEADS, q_spec=(LANES, lambda h: h),
                            k_blk=lambda h: N_HEADS + h, v_blk=lambda h: h, bias_blk=lambda h: 0, nq_stack=1,
                            vdim=HEAD_DIM, mode="causal", out_rows=HEAD_DIM, out_dtype=BF, q_mask="none",
                            tq=FOX_TQ, near=[(a, a) for a in fox_near], grp=CAUSAL_GROUP)
            x2d = _outproj([ot], None, fox_w_out[ic].astype(BF), x2d, _row(g[1]), B=B, S=S)
            ic += 1
        x2d = _mlp_ple(x2d, _row(g[2]), mlp_w1[i].astype(BF), mlp_w2[i].astype(BF), _row(g[3]),
                       ple_gate_w[i].astype(BF), p[i].reshape(B * S, -1), ple_w[i].astype(BF))
    return x2d.reshape(B, S, D)
```

```python
import functools
import math

import jax
import jax.numpy as jnp
from jax import lax
from jax.experimental import pallas as pl
from jax.experimental.pallas import tpu as pltpu

BF = jnp.bfloat16
F32 = jnp.float32

D_MODEL = 1024
HEAD_DIM = 64
LANES = 128
NORM_EPS = 1e-6
MASK_VALUE = -1e30
REL_BUCKETS = 32
REL_MAX_DIST = 128
REL_TABLE_LEN = 512
N_HEADS = D_MODEL // HEAD_DIM
DA_HEADS = N_HEADS // 2
NSA_GROUPS = 4
NSA_REP = N_HEADS // NSA_GROUPS
NSA_CMP_LEN = 32
NSA_CMP_STRIDE = 16
NSA_SEL_LEN = 64
NSA_TOP_N = 16
NSA_WINDOW = 512
NSA_FORCE_SCORE = 1e4
N_MIXERS = 3

ATT_TQ = 256
DA_TQ = 512
FOX_TQ = 1024
CAUSAL_GROUP = 8
ATT_TK = 256
VMEM_LIMIT = 56 * 1024 * 1024
CMP_ROW_STEP = 128
PV_CHUNK = 256
ONES_ROWS = 16
LOG2E = math.log2(math.e)


def _cparams(*sem):
    return pltpu.CompilerParams(dimension_semantics=sem, vmem_limit_bytes=VMEM_LIMIT)


def _rms(x, g):
    return x * lax.rsqrt(jnp.mean(x * x, axis=-1, keepdims=True) + NORM_EPS) * g


def _dot(a, b):
    return jnp.dot(a, b, preferred_element_type=F32)


def _dot_nt(a, b):
    return lax.dot_general(a, b, (((1,), (1,)), ((), ())), preferred_element_type=F32)


def _dot_tn(a, b):
    return lax.dot_general(a, b, (((0,), (0,)), ((), ())), preferred_element_type=F32)


def _inproj_body(*refs, n_f32, n_gate, fox, tm, col_chunk):
    it = iter(refs)
    x_ref, g_ref, wn_ref, bn_ref, wt_ref = (next(it) for _ in range(5))
    wf32_ref = next(it) if n_f32 else None
    wg_ref = next(it) if n_gate else None
    if fox:
        wf_ref, bf_ref, pm_ref = next(it), next(it), next(it)
    on_ref, ot_ref = next(it), next(it)
    of32_ref = next(it) if n_f32 else None
    og_ref = next(it) if n_gate else None
    carry_ref = next(it) if fox else None

    h = _rms(x_ref[...], g_ref[...]).astype(BF)
    n_nat = on_ref.shape[1]
    k_off = n_nat // 2
    if fox:
        @pl.when(pl.program_id(1) == 0)
        def _():
            carry_ref[...] = jnp.zeros_like(carry_ref)

        lf = _dot(h, wf_ref[...]) + bf_ref[...]
        lane = lax.broadcasted_iota(jnp.int32, lf.shape, 1)
        row = lax.broadcasted_iota(jnp.int32, lf.shape, 0)
        ls = jnp.minimum(lf, 0.0) - jnp.log1p(jnp.exp(-jnp.abs(lf)))
        c = jnp.where(lane < 3 * N_HEADS, ls, 0.0)
        k = 1
        while k < tm:
            c = c + jnp.where(row >= k, pltpu.roll(c, k, axis=0), 0.0)
            k *= 2
        c = c + carry_ref[...]
        carry_ref[...] = c[tm - 1:tm, :]
        c = c * LOG2E
        hi = c.astype(BF).astype(F32)
        r1 = c - hi
        mid = r1.astype(BF).astype(F32)
        lo = (r1 - mid).astype(BF).astype(F32)
        c3 = jnp.where(lane < N_HEADS, hi, jnp.where(lane < 2 * N_HEADS, mid, lo)).astype(BF)
    if fox:
        lane_b = lax.broadcasted_iota(jnp.int32, (tm, LANES), 1)
        d_c = wn_ref.shape[1] // 2
        for part in range(2):
            yc = _dot(h, wn_ref[:, part * d_c:(part + 1) * d_c])
            for oc in range(0, k_off, col_chunk):
                pieces = []
                for hh in range(oc // LANES, (oc + col_chunk) // LANES):
                    blk = yc[:, (hh // 2) * LANES:(hh // 2 + 1) * LANES]
                    keep = (lane_b < HEAD_DIM) if hh % 2 == 0 else (lane_b >= HEAD_DIM)
                    pieces.append(jnp.where(keep, blk, 0.0))
                y = jnp.concatenate(pieces, axis=1) + bn_ref[:, part * k_off + oc:part * k_off + oc + col_chunk]
                if part == 1:
                    y = y + _dot(c3, pm_ref[:, oc:oc + col_chunk])
                on_ref[:, part * k_off + oc:part * k_off + oc + col_chunk] = y.astype(on_ref.dtype)
    else:
        for c0 in range(0, n_nat, col_chunk):
            y = _dot(h, wn_ref[:, c0:c0 + col_chunk]) + bn_ref[:, c0:c0 + col_chunk]
            on_ref[:, c0:c0 + col_chunk] = y.astype(on_ref.dtype)
    ot_ref[0] = _dot_nt(wt_ref[...], h).astype(ot_ref.dtype)
    if n_f32:
        of32_ref[...] = _dot(h, wf32_ref[...])
    if n_gate:
        og_ref[0] = jax.nn.sigmoid(_dot_nt(wg_ref[...], h))


def _inproj(x2d, g, wn, bn, wt, *, B, S, tm=512, wf32=None, wg=None, fox_extra=None):
    M, D = x2d.shape
    ns = S // tm
    n_nat, n_t = bn.shape[1], wt.shape[0]
    full = lambda a: pl.BlockSpec(a.shape, lambda b, s: (0,) * a.ndim)
    ins = [x2d, g, wn, bn, wt]
    in_specs = [pl.BlockSpec((tm, D), lambda b, s: (b * ns + s, 0)), full(g), full(wn), full(bn), full(wt)]
    out_shape = [jax.ShapeDtypeStruct((M, n_nat), BF), jax.ShapeDtypeStruct((B, n_t, S), BF)]
    out_specs = [pl.BlockSpec((tm, n_nat), lambda b, s: (b * ns + s, 0)),
                 pl.BlockSpec((1, n_t, tm), lambda b, s: (b, 0, s))]
    scratch = []
    if wf32 is not None:
        ins.append(wf32)
        in_specs.append(full(wf32))
        out_shape.append(jax.ShapeDtypeStruct((M, wf32.shape[1]), F32))
        out_specs.append(pl.BlockSpec((tm, wf32.shape[1]), lambda b, s: (b * ns + s, 0)))
    if wg is not None:
        ins.append(wg)
        in_specs.append(full(wg))
        out_shape.append(jax.ShapeDtypeStruct((B, wg.shape[0], S), F32))
        out_specs.append(pl.BlockSpec((1, wg.shape[0], tm), lambda b, s: (b, 0, s)))
    if fox_extra is not None:
        for a in fox_extra:
            ins.append(a)
            in_specs.append(full(a))
        scratch.append(pltpu.VMEM((1, LANES), F32))
    body = functools.partial(_inproj_body, n_f32=wf32 is not None, n_gate=wg is not None,
                             fox=fox_extra is not None, tm=tm, col_chunk=1024 if n_nat % 1024 == 0 else 512)
    return pl.pallas_call(
        body, grid=(B, ns), in_specs=in_specs, out_specs=out_specs, out_shape=out_shape,
        scratch_shapes=scratch, compiler_params=_cparams("arbitrary", "arbitrary"), name="inproj")(*ins)


def _attn_body(*refs, tq, grp, nq_stack, vdim, mode, use_sel, final, lam_init, q_mask, near):
    it = iter(refs)
    q_ref, qn_ref, k_ref, vt_ref, bias_ref = (next(it) for _ in range(5))
    selb_ref, selbn_ref = (next(it), next(it)) if use_sel else (None, None)
    if final == "da":
        lam_ref, subg_ref = next(it), next(it)
    o_ref = next(it)
    qm_ref, m_ref, acc_ref, s_scr, mt_scr, r0_ref = (next(it) for _ in range(6))
    tk = ATT_TK
    lw = nq_stack * tq
    i = pl.program_id(2)
    i_next = jnp.minimum(i + 1, pl.num_programs(2) - 1)

    lane = lax.broadcasted_iota(jnp.int32, (tq, LANES), 1)
    for slot, src_ref in enumerate((q_ref, qn_ref)):
        qt = src_ref[...]
        for r in range(nq_stack):
            blk = qt[:, 0:LANES] if q_mask == "da" else qt[:, r * LANES:(r + 1) * LANES]
            if q_mask == "da":
                keep = (lane < HEAD_DIM) if r == 0 else (lane >= HEAD_DIM)
            elif q_mask == "group_parity":
                keep = (lane // HEAD_DIM) == (pl.program_id(1) % 2)
            else:
                keep = None
            if keep is not None:
                blk = jnp.where(keep, blk, jnp.zeros_like(blk))
            qm_ref[slot, r * tq:(r + 1) * tq, :] = blk
    m_ref[...] = jnp.full_like(m_ref, MASK_VALUE)
    acc_ref[...] = jnp.zeros_like(acc_ref)

    def stage_qk(group, r, next_tile=False):
        rows = len(group) * tk
        ks = [k_ref[pl.ds(pl.multiple_of(kt * tk, tk), tk), :] for kt, _, _ in group]
        s = _dot_nt(ks[0] if len(ks) == 1 else jnp.concatenate(ks, axis=0), qm_ref[int(next_tile)])
        if use_sel:
            slabs = []
            for t, (kt, _, _) in enumerate(group):
                sb8 = (selbn_ref if next_tile else selb_ref)[0, 0, pl.ds(pl.multiple_of((kt // 2) * 8, 8), 8), :]
                sb4 = jnp.where(kt % 2 == 0, sb8[0:4, :], sb8[4:8, :])
                sbt = jnp.concatenate([sb4] * nq_stack, axis=1)
                for j in range(tk // NSA_SEL_LEN):
                    r0 = t * tk + j * NSA_SEL_LEN
                    slabs.append(s[r0:r0 + NSA_SEL_LEN, :] + sbt[j:j + 1, :])
            s = jnp.concatenate(slabs, axis=0)
        s_scr[r, 0:rows, :] = s
        mt_scr[r] = jnp.max(s, axis=0, keepdims=True)

    def stage_softmax_pv(group, r):
        rows = len(group) * tk
        plain = all(bidx is None and fm is None for _, bidx, fm in group)
        vts = [vt_ref[0, :, pl.ds(pl.multiple_of(kt * tk, tk), tk)] for kt, _, _ in group]
        lhs = jnp.concatenate([vts[0] if len(vts) == 1 else jnp.concatenate(vts, axis=1),
                               jnp.ones((ONES_ROWS, rows), BF)], axis=0)
        for c0 in range(0, lw, PV_CHUNK):
            cols = slice(c0, c0 + PV_CHUNK)
            if plain:
                s = s_scr[r, 0:rows, cols]
                mt = mt_scr[r, :, cols]
            else:
                parts = []
                for t, (_, bidx, fm) in enumerate(group):
                    st = s_scr[r, t * tk:(t + 1) * tk, cols]
                    if bidx is not None:
                        q, l0 = divmod(c0, tq)
                        st = st + bias_ref[q, bidx, :, l0:l0 + PV_CHUNK]
                    if fm is not None:
                        st = st + jnp.where(fm, MASK_VALUE, 0.0)
                    parts.append(st)
                s = parts[0] if len(parts) == 1 else jnp.concatenate(parts, axis=0)
                mt = jnp.max(s, axis=0, keepdims=True)
            m_old = m_ref[:, cols]
            m_new = jnp.maximum(m_old, mt)
            alpha = jnp.exp2(m_old - m_new)
            p = jnp.exp2(s - m_new).astype(BF)
            m_ref[:, cols] = m_new
            acc_ref[:, cols] = alpha * acc_ref[:, cols] + _dot(lhs, p)

    def trip(r, cur, nxt, next_tile=False):
        if not isinstance(r, int):
            for val in (0, 1):
                @pl.when(r == val)
                def _(val=val):
                    trip(val, cur, nxt, next_tile)
            return
        if nxt:
            stage_qk(nxt, 1 - r, next_tile)
        stage_softmax_pv(cur, r)

    n_tile = tq // tk
    a0 = near[0][0]
    last_kt = k_ref.shape[0] // tk - 1
    lo_values = sorted({max(ii * n_tile + a0, 0) % grp for ii in range(2 * grp)}) if mode == "causal" else [0]

    def tile_plan(ii):
        base = ii * n_tile
        near_tiles = [(jnp.maximum(base + a, 0), bidx, (base + a < 0) if a < 0 else None) for a, bidx in near]
        n_far = jnp.maximum(base + a0, 0) if mode == "causal" else 0
        trips, lo = n_far // grp, n_far % grp
        tails = {v: [(n_far - v + t, None, None) for t in range(v)] + near_tiles for v in lo_values}
        first = []
        for t in range(grp):
            kt_near = sum(jnp.where(lo == v, tails[v][min(t, len(tails[v]) - 1)][0], 0) for v in lo_values)
            first.append((jnp.where(trips > 0, t, kt_near), None, None))
        return trips, lo, tails, first

    trips, lo, tails, first = tile_plan(i)
    far_group = lambda u: [(jnp.minimum(grp * u + t, last_kt), None, None) for t in range(grp)]
    if mode == "causal":
        r0 = jnp.where(i == 0, 0, r0_ref[0])

        @pl.when(i == 0)
        def _():
            stage_qk(first, 0)

        def body(u, c):
            trip((r0 + u) % 2, far_group(u), far_group(u + 1))
            return c
        lax.fori_loop(0, trips, body, 0)
        first_next = tile_plan(i_next)[3]
    else:
        r0 = 0
        stage_qk(first, 0)

    def run_tail(tail):
        groups = [tail[n:n + grp] for n in range(0, len(tail), grp)]
        for e, group in enumerate(groups):
            r_e = (r0 + trips + e) % 2
            if e + 1 < len(groups):
                trip(r_e, group, groups[e + 1])
            elif mode == "causal":
                trip(r_e, group, first_next, next_tile=True)
                r0_ref[0] = 1 - r_e
            else:
                trip(r_e, group, None)

    if len(lo_values) == 1:
        run_tail(tails[lo_values[0]])
    else:
        for v in lo_values:
            @pl.when(lo == v)
            def _(v=v):
                run_tail(tails[v])

    acc = acc_ref[0:vdim, :]
    l = acc_ref[vdim:vdim + 1, :]
    if final == "da":
        lamv = lam_ref[...]
        lam = (jnp.exp(jnp.sum(lamv[0:1] * lamv[1:2], axis=1, keepdims=True))
               - jnp.exp(jnp.sum(lamv[2:3] * lamv[3:4], axis=1, keepdims=True)) + lam_init)
        o = acc[:, :tq] / l[:, :tq] - lam * (acc[:, tq:] / l[:, tq:])
        o = o * lax.rsqrt(jnp.mean(o * o, axis=0, keepdims=True) + NORM_EPS) * subg_ref[...] * (1.0 - lam_init)
        o_ref[0] = o.astype(o_ref.dtype)
    else:
        for r in range(nq_stack):
            o_ref[0, r * vdim:(r + 1) * vdim, :] = (acc[:, r * tq:(r + 1) * tq] / l[:, r * tq:(r + 1) * tq]).astype(o_ref.dtype)


def _attention(q_arr, k_arr, vt, bias, *, B, S, n_prog, q_spec, k_blk, v_blk, bias_blk, nq_stack, vdim, mode,
               out_rows, out_dtype, q_mask, tq, near, grp=2, selb=None, final="plain", lam=None, subg=None, lam_init=0.0):
    tk = ATT_TK
    nq = S // tq
    lw = nq_stack * tq
    q_w, q_blk = q_spec
    nxt = lambda i: jnp.minimum(i + 1, nq - 1)
    ins = [q_arr, q_arr, k_arr, vt, bias]
    in_specs = [
        pl.BlockSpec((tq, q_w), lambda b, h, i: (b * nq + i, q_blk(h))),
        pl.BlockSpec((tq, q_w), lambda b, h, i: (b * nq + nxt(i), q_blk(h))),
        pl.BlockSpec((S, LANES), lambda b, h, i: (b, k_blk(h))),
        pl.BlockSpec((1, vdim, S), lambda b, h, i: (b, v_blk(h), 0)),
        pl.BlockSpec((nq_stack,) + bias.shape[1:], lambda b, h, i: (bias_blk(h), 0, 0, 0)),
    ]
    if selb is not None:
        ins += [selb, selb]
        in_specs += [pl.BlockSpec((1, 1, selb.shape[2], tq), lambda b, h, i: (b, h, 0, i)),
                     pl.BlockSpec((1, 1, selb.shape[2], tq), lambda b, h, i: (b, h, 0, nxt(i)))]
    if final == "da":
        ins += [lam, subg]
        in_specs += [pl.BlockSpec(lam.shape, lambda b, h, i: (0, 0)), pl.BlockSpec(subg.shape, lambda b, h, i: (0, 0))]
    body = functools.partial(_attn_body, tq=tq, grp=grp, nq_stack=nq_stack, vdim=vdim, mode=mode, use_sel=selb is not None,
                             final=final, lam_init=lam_init, q_mask=q_mask, near=near)
    return pl.pallas_call(
        body, grid=(B, n_prog, nq), in_specs=in_specs,
        out_specs=pl.BlockSpec((1, out_rows, tq), lambda b, h, i: (b, h, i)),
        out_shape=jax.ShapeDtypeStruct((B, out_rows * n_prog, S), out_dtype),
        scratch_shapes=[pltpu.VMEM((2, lw, LANES), BF), pltpu.VMEM((1, lw), F32), pltpu.VMEM((vdim + ONES_ROWS, lw), F32),
                        pltpu.VMEM((2, grp * tk, lw), F32), pltpu.VMEM((2, 1, lw), F32), pltpu.SMEM((1,), jnp.int32)],
        compiler_params=_cparams("arbitrary", "arbitrary", "arbitrary"), name="attn_" + mode + "_" + final)(*ins)


def _compress_body(x_ref, pe_ref, w1_ref, w2_ref, o_ref):
    x = x_ref[0, 0, 0]
    n, half = x.shape
    pe = pe_ref[0]
    first = _dot((x + pe[:, :half]).astype(BF), w1_ref[0, :half, :])
    second = _dot((x + pe[:, half:]).astype(BF), w1_ref[0, half:, :])
    hcur = first + pltpu.roll(second, n - 1, axis=0)
    hcur = 0.5 * hcur * (1.0 + jnp.tanh(math.sqrt(2.0 / math.pi) * (hcur + 0.044715 * (hcur * hcur * hcur))))
    o_ref[0, 0, 0] = _dot(hcur.astype(BF), w2_ref[0])


def _compress(chunks, pe, w1, w2):
    _, B, G, n, K = chunks.shape
    return pl.pallas_call(
        _compress_body, grid=(2, B, G),
        in_specs=[pl.BlockSpec((1, 1, 1, n, K), lambda a, b, g: (a, b, g, 0, 0)),
                  pl.BlockSpec((1, 1, 2 * K), lambda a, b, g: (a, 0, 0)),
                  pl.BlockSpec((1,) + w1.shape[1:], lambda a, b, g: (a, 0, 0)),
                  pl.BlockSpec((1,) + w2.shape[1:], lambda a, b, g: (a, 0, 0))],
        out_specs=pl.BlockSpec((1, 1, 1, n, HEAD_DIM), lambda a, b, g: (a, b, g, 0, 0)),
        out_shape=jax.ShapeDtypeStruct((2, B, G, n, HEAD_DIM), F32),
        compiler_params=_cparams("arbitrary", "arbitrary", "arbitrary"), name="nsa_compress")(chunks, pe, w1, w2)


def _cmp_body(q_ref, kc_ref, vct_ref, bc_ref, ov_ref, o_ref, selb_ref, s_ref, imp_ref):
    tq = ATT_TQ
    i = pl.program_id(2)
    n_cmp = kc_ref.shape[2]
    n_sel = selb_ref.shape[2]
    band = bc_ref.shape[1]
    bc = jnp.concatenate([bc_ref[r] for r in range(NSA_REP)], axis=1)
    qt = q_ref[...]
    lane = lax.broadcasted_iota(jnp.int32, (tq, LANES), 1)
    keep = (lane // HEAD_DIM) == (pl.program_id(1) % 2)
    qm = jnp.concatenate(
        [jnp.where(keep, qt[:, r * LANES:(r + 1) * LANES], jnp.zeros((tq, LANES), BF)) for r in range(NSA_REP)], axis=0)

    def attend(rows):
        s = _dot_nt(kc_ref[0, 0, 0:rows, :], qm)
        row = lax.broadcasted_iota(jnp.int32, s.shape, 0)
        s_ref[0:rows, :] = jnp.where(row < (band // 2) * (i + 1), s, MASK_VALUE)

        @pl.when(i == 0)
        def _():
            s_ref[0:band // 2, :] = s_ref[0:band // 2, :] + bc[band // 2:, :]

        @pl.when(i > 0)
        def _():
            r0 = pl.multiple_of((band // 2) * (i - 1), band // 2)
            s_ref[pl.ds(r0, band), :] = s_ref[pl.ds(r0, band), :] + bc

        ps = None
        for r in range(NSA_REP):
            s = s_ref[0:rows, r * tq:(r + 1) * tq]
            m = jnp.max(s, axis=0, keepdims=True)
            e = jnp.exp2(s - m)
            l = jnp.sum(e, axis=0, keepdims=True)
            inv = jnp.where(m > 0.5 * MASK_VALUE, 1.0 / l, 0.0)
            p = e * inv
            o_ref[0, r * HEAD_DIM:(r + 1) * HEAD_DIM, :] = _dot(vct_ref[0, 0, :, 0:rows], p.astype(BF))
            ps = p if ps is None else ps + p
        p_hi = ps.astype(BF)
        r1 = ps - p_hi.astype(F32)
        p_mid = r1.astype(BF)
        p_lo = (r1 - p_mid.astype(F32)).astype(BF)
        ov = ov_ref[:, 0:rows]
        imp_ref[...] = _dot(ov, p_hi) + _dot(ov, p_mid) + _dot(ov, p_lo)

    step = min(CMP_ROW_STEP, n_cmp)
    tiles_per_step = step // (band // 2)
    for v in range(n_cmp // step):
        @pl.when((i >= v * tiles_per_step) & (i < (v + 1) * tiles_per_step))
        def _(v=v):
            attend((v + 1) * step)

    imp = imp_ref[...]
    j = lax.broadcasted_iota(jnp.int32, (n_sel, tq), 0)
    t = i * tq + lax.broadcasted_iota(jnp.int32, (n_sel, tq), 1)
    cur = t // NSA_SEL_LEN
    forced = (j == 0) | (j == cur) | (j == cur - 1)
    w = jnp.where(j <= cur, jnp.where(forced, NSA_FORCE_SCORE, imp), -1.0)
    jf = j.astype(F32)
    for _ in range(min(NSA_TOP_N, n_sel)):
        mx = jnp.max(w, axis=0, keepdims=True)
        idx = jnp.min(jnp.where(w == mx, jf, float(n_sel)), axis=0, keepdims=True)
        w = jnp.where(jf == idx, -2.0, w)
    selb_ref[0, 0] = jnp.where(w == -2.0, 0.0, MASK_VALUE)


def _cmp_attention(qn, kc, vct, bc, ov, *, B, S, q_coloff):
    tq = ATT_TQ
    nq = S // tq
    n_cmp = kc.shape[2]
    n_sel = S // NSA_SEL_LEN
    return pl.pallas_call(
        _cmp_body, grid=(B, NSA_GROUPS, nq),
        in_specs=[pl.BlockSpec((tq, NSA_REP * LANES), lambda b, g, i: (b * nq + i, q_coloff + g // 2)),
                  pl.BlockSpec((1, 1, n_cmp, LANES), lambda b, g, i: (b, g, 0, 0)),
                  pl.BlockSpec((1, 1, HEAD_DIM, n_cmp), lambda b, g, i: (b, g, 0, 0)),
                  pl.BlockSpec((NSA_REP,) + bc.shape[1:], lambda b, g, i: (g, 0, 0)),
                  pl.BlockSpec(ov.shape, lambda b, g, i: (0, 0))],
        out_specs=[pl.BlockSpec((1, NSA_REP * HEAD_DIM, tq), lambda b, g, i: (b, g, i)),
                   pl.BlockSpec((1, 1, n_sel, tq), lambda b, g, i: (b, g, 0, i))],
        out_shape=[jax.ShapeDtypeStruct((B, D_MODEL, S), F32), jax.ShapeDtypeStruct((B, NSA_GROUPS, n_sel, S), F32)],
        scratch_shapes=[pltpu.VMEM((n_cmp, NSA_REP * tq), F32), pltpu.VMEM((n_sel, tq), F32)],
        compiler_params=_cparams("arbitrary", "arbitrary", "arbitrary"), name="nsa_cmp_topk")(qn, kc, vct, bc, ov)


def _outproj_body(*refs, n_o):
    it = iter(refs)
    o_refs = [next(it) for _ in range(n_o)]
    gt_ref = next(it) if n_o > 1 else None
    w_ref, x_ref, g_ref, out_ref = (next(it) for _ in range(4))
    if n_o == 1:
        ot = o_refs[0][0]
    else:
        gts = gt_ref[0]
        parts = []
        for hd in range(N_HEADS):
            rows = slice(hd * HEAD_DIM, (hd + 1) * HEAD_DIM)
            acc = gts[hd:hd + 1, :] * o_refs[0][0, rows, :]
            for b in range(1, n_o):
                acc = acc + gts[b * N_HEADS + hd:b * N_HEADS + hd + 1, :] * o_refs[b][0, rows, :]
            parts.append(acc.astype(BF))
        ot = jnp.concatenate(parts, axis=0)
    y = _dot_tn(ot, w_ref[...])
    out_ref[...] = x_ref[...] + _rms(y, g_ref[...])


def _outproj(o_list, gt, w, x2d, g, *, B, S, tm=512):
    M, D = x2d.shape
    ns = S // tm
    ins = list(o_list)
    in_specs = [pl.BlockSpec((1, D, tm), lambda b, s: (b, 0, s)) for _ in o_list]
    if gt is not None:
        ins.append(gt)
        in_specs.append(pl.BlockSpec((1, gt.shape[1], tm), lambda b, s: (b, 0, s)))
    ins += [w, x2d, g]
    in_specs += [pl.BlockSpec(w.shape, lambda b, s: (0, 0)),
                 pl.BlockSpec((tm, D), lambda b, s: (b * ns + s, 0)),
                 pl.BlockSpec(g.shape, lambda b, s: (0, 0))]
    return pl.pallas_call(
        functools.partial(_outproj_body, n_o=len(o_list)), grid=(B, ns), in_specs=in_specs,
        out_specs=pl.BlockSpec((tm, D), lambda b, s: (b * ns + s, 0)),
        out_shape=jax.ShapeDtypeStruct((M, D), F32),
        compiler_params=_cparams("arbitrary", "arbitrary"), name="outproj")(*ins)


def _mlp_body(x_ref, g2_ref, w1_ref, w2_ref, g3_ref, wg_ref, p_ref, wp_ref, out_ref, h_ref, acc_ref):
    f = pl.program_id(1)

    @pl.when(f == 0)
    def _():
        h_ref[...] = _rms(x_ref[...], g2_ref[...]).astype(BF)
        acc_ref[...] = jnp.zeros_like(acc_ref)

    a = jnp.maximum(_dot(h_ref[...], w1_ref[...]), 0.0)
    acc_ref[...] += _dot((a * a).astype(BF), w2_ref[...])

    @pl.when(f == pl.num_programs(1) - 1)
    def _():
        x2 = x_ref[...] + _rms(acc_ref[...], g3_ref[...])
        gate = jax.nn.sigmoid(_dot(x2.astype(BF), wg_ref[...]))
        out_ref[...] = x2 + gate * _dot(p_ref[...].astype(BF), wp_ref[...])


def _mlp_ple(x2d, g2, w1, w2, g3, wg, p2d, wp, *, tm=1024, tf=512):
    M, D = x2d.shape
    FF = w1.shape[1]
    PD = p2d.shape[1]
    const = lambda a: pl.BlockSpec(a.shape, lambda m, f: (0, 0))
    return pl.pallas_call(
        _mlp_body, grid=(M // tm, FF // tf),
        in_specs=[pl.BlockSpec((tm, D), lambda m, f: (m, 0)), const(g2),
                  pl.BlockSpec((D, tf), lambda m, f: (0, f)), pl.BlockSpec((tf, D), lambda m, f: (f, 0)),
                  const(g3), const(wg), pl.BlockSpec((tm, PD), lambda m, f: (m, 0)), const(wp)],
        out_specs=pl.BlockSpec((tm, D), lambda m, f: (m, 0)),
        out_shape=jax.ShapeDtypeStruct((M, D), F32),
        scratch_shapes=[pltpu.VMEM((tm, D), BF), pltpu.VMEM((tm, D), F32)],
        compiler_params=_cparams("arbitrary", "arbitrary"), name="mlp_ple")(x2d, g2, w1, w2, g3, wg, p2d, wp)


def _t5_bucket(dist):
    n = jnp.maximum(dist, 0)
    max_exact = REL_BUCKETS // 2
    nf = jnp.maximum(n, 1).astype(F32)
    large = max_exact + (jnp.log(nf / max_exact) / math.log(REL_MAX_DIST / max_exact)
                         * (REL_BUCKETS - max_exact)).astype(jnp.int32)
    large = jnp.minimum(large, REL_BUCKETS - 1)
    return jnp.where(n < max_exact, n, large)


def _bias_by_distance(rel_bias):
    tb = rel_bias[_t5_bucket(jnp.arange(REL_TABLE_LEN))].astype(F32)
    return ((tb - rel_bias[REL_BUCKETS - 1].astype(F32)[None, :]) * LOG2E).T


def _toeplitz(fn, rows, cols, off, row_stride=1):
    n = row_stride * rows + cols
    d = jnp.arange(n)
    d = jnp.where(d < cols, d, d - n)
    v = fn(d + off)
    flat = jnp.tile(v, (1,) * (v.ndim - 1) + (rows,))[..., :rows * (n - row_stride)]
    return flat.reshape(v.shape[:-1] + (rows, n - row_stride))[..., :cols]


def _bias_fn(tbs, hi_limit=None):
    def fn(d):
        val = jnp.zeros((1,) + d.shape, F32) if tbs is None else tbs[:, jnp.clip(d, 0, REL_TABLE_LEN - 1)]
        bad = d < 0
        if hi_limit is not None:
            bad = bad | (d >= hi_limit)
        return jnp.where(bad[None, :], MASK_VALUE, val)
    return fn


def _near_tables(tbs, tq, offsets, hi_limit=None):
    return jnp.stack([_toeplitz(_bias_fn(tbs, hi_limit), ATT_TK, tq, -a * ATT_TK) for a in offsets], axis=1)


def _cmp_band_table(tbs):
    tq = ATT_TQ
    band = 2 * (tq // NSA_CMP_STRIDE)
    return _toeplitz(_bias_fn(tbs), band, tq, tq - (NSA_CMP_LEN - 1), row_stride=NSA_CMP_STRIDE)


def _nsa_q_perm():
    cols = []
    for p in range(NSA_GROUPS // 2):
        for r in range(NSA_REP):
            for g in (2 * p, 2 * p + 1):
                h = g * NSA_REP + r
                cols.extend(range(h * HEAD_DIM, (h + 1) * HEAD_DIM))
    return jnp.asarray(cols, jnp.int32)


def _row(v):
    return v.reshape(1, -1).astype(F32)


def kernel(x, p, rel_bias, norm_g, mlp_w1, mlp_w2, ple_w, ple_gate_w, da_w_in, da_lambda, da_subln, da_w_out,
           nsa_w_in, nsa_cmp_pe, nsa_cmp_w1, nsa_cmp_w2, nsa_w_out, fox_w_in, fox_b_f, fox_w_out):
    B, S, D = x.shape
    depth = p.shape[0]
    scale = HEAD_DIM ** -0.5 * LOG2E
    tbs = _bias_by_distance(rel_bias)
    da_near = tuple(range(-1, DA_TQ // ATT_TK))
    da_bias = _near_tables(tbs, DA_TQ, da_near)
    x2d = x.reshape(B * S, D)
    ia = ib = ic = 0
    for i in range(depth):
        g = norm_g[i]
        kind = i % N_MIXERS
        if kind == 0:
            lam_init = 0.8 - 0.6 * math.exp(-0.3 * i)
            w_in = da_w_in[ia]
            wn = jnp.concatenate([w_in[:, :D] * scale, w_in[:, D:2 * D]], axis=1).astype(BF)
            wt = w_in[:, 2 * D:].T.astype(BF)
            qk, vt = _inproj(x2d, _row(g[0]), wn, jnp.zeros((1, 2 * D), F32), wt, B=B, S=S)
            ot = _attention(qk, qk, vt, da_bias, B=B, S=S, n_prog=DA_HEADS, q_spec=(LANES, lambda h: h),
                            k_blk=lambda h: DA_HEADS + h, v_blk=lambda h: h, bias_blk=lambda h: h, nq_stack=2,
                            vdim=2 * HEAD_DIM, mode="causal", out_rows=2 * HEAD_DIM, out_dtype=BF, q_mask="da",
                            tq=DA_TQ, near=[(da_near[0] - 1, None)] + [(a, n) for n, a in enumerate(da_near)], grp=CAUSAL_GROUP,
                            final="da", lam=da_lambda[ia].astype(F32), subg=da_subln[ia].reshape(-1, 1).astype(F32),
                            lam_init=lam_init)
            x2d = _outproj([ot], None, da_w_out[ia].astype(BF), x2d, _row(g[1]), B=B, S=S)
            ia += 1
        elif kind == 1:
            w_in = nsa_w_in[ib]
            kvd = NSA_GROUPS * HEAD_DIM
            wq = w_in[:, :D][:, _nsa_q_perm()] * scale
            kv = [w_in[:, D + a * kvd:D + (a + 1) * kvd] for a in range(6)]
            wn = jnp.concatenate([wq, kv[2], kv[4]], axis=1).astype(BF)
            wt = jnp.concatenate([kv[3], kv[5]], axis=1).T.astype(BF)
            wf32 = jnp.concatenate([kv[0], kv[1]], axis=1).astype(BF)
            wgate = w_in[:, D + 6 * kvd:].reshape(D, N_HEADS, 3).transpose(2, 1, 0).reshape(3 * N_HEADS, D).astype(BF)
            qn, vt, cmp_in, gates_t = _inproj(x2d, _row(g[0]), wn, jnp.zeros((1, wn.shape[1]), F32), wt, B=B, S=S,
                                              wf32=wf32, wg=wgate)
            n_chunk = S // NSA_CMP_STRIDE
            cm = cmp_in.reshape(B, n_chunk, NSA_CMP_STRIDE, 2, NSA_GROUPS, HEAD_DIM).transpose(3, 0, 4, 1, 2, 5)
            cm = cm.reshape(2, B, NSA_GROUPS, n_chunk, NSA_CMP_STRIDE * HEAD_DIM)
            pe = nsa_cmp_pe[ib].reshape(2, 1, NSA_CMP_LEN * HEAD_DIM).astype(F32)
            kvc = _compress(cm, pe, nsa_cmp_w1[ib].astype(BF), nsa_cmp_w2[ib].astype(BF))
            kc = jnp.concatenate([kvc[0], kvc[0]], axis=-1).astype(BF)
            vct = kvc[1].transpose(0, 1, 3, 2).astype(BF)
            jj = jnp.arange(S // NSA_SEL_LEN)[:, None]
            nn = jnp.arange(n_chunk)[None, :]
            n_cmp_blocks = (S - NSA_CMP_LEN) // NSA_CMP_STRIDE + 1
            ov = ((nn * NSA_CMP_STRIDE < (jj + 1) * NSA_SEL_LEN) & (nn * NSA_CMP_STRIDE + NSA_CMP_LEN - 1 >= jj * NSA_SEL_LEN)
                  & (nn < n_cmp_blocks)).astype(BF)
            cmp_band = _cmp_band_table(tbs)
            oc_t, selb = _cmp_attention(qn, kc, vct, cmp_band, ov, B=B, S=S, q_coloff=0)
            nsa_kw = dict(B=B, S=S, n_prog=NSA_GROUPS, q_spec=(NSA_REP * LANES, lambda h: h // 2),
                          bias_blk=lambda h: h, nq_stack=NSA_REP, vdim=HEAD_DIM, out_rows=NSA_REP * HEAD_DIM,
                          out_dtype=F32, q_mask="group_parity", tq=ATT_TQ)
            kblk0 = D // LANES
            sel_bias = _near_tables(tbs, ATT_TQ, (-1, 0))
            win_bias = _near_tables(tbs, ATT_TQ, (-2, -1, 0), hi_limit=NSA_WINDOW)
            os_t = _attention(qn, qn, vt, sel_bias, mode="causal", near=[(-1, 0), (0, 1)], grp=CAUSAL_GROUP,
                              k_blk=lambda h: kblk0 + h // 2, v_blk=lambda h: h, selb=selb, **nsa_kw)
            ow_t = _attention(qn, qn, vt, win_bias, mode="window", near=[(-2, 0), (-1, 1), (0, 2)],
                              k_blk=lambda h: kblk0 + 2 + h // 2, v_blk=lambda h: NSA_GROUPS + h, **nsa_kw)
            x2d = _outproj([oc_t, os_t, ow_t], gates_t, nsa_w_out[ib].astype(BF), x2d, _row(g[1]), B=B, S=S)
            ib += 1
        else:
            w_in = fox_w_in[ic]
            wn = jnp.concatenate([w_in[:, :D] * scale, w_in[:, D:2 * D]], axis=1).astype(BF)
            hh = jnp.arange(N_HEADS)
            aug0 = hh * LANES + jnp.where(hh % 2 == 0, HEAD_DIM, 0)
            sel = jnp.arange(2 * D)[None, :] - aug0[:, None]
            aug = jnp.any((sel >= 0) & (sel < 3), axis=0).astype(F32).reshape(1, 2 * D)
            bn = jnp.concatenate([aug, jnp.zeros((1, 2 * D), F32)], axis=1)
            wf = w_in[:, 3 * D:]
            wf3 = jnp.concatenate([wf, wf, wf, jnp.zeros((D, LANES - 3 * N_HEADS), F32)], axis=1).astype(BF)
            bf3 = jnp.concatenate([fox_b_f[ic]] * 3 + [jnp.zeros((LANES - 3 * N_HEADS,), F32)]).reshape(1, LANES).astype(F32)
            rows = jnp.arange(LANES)[:, None]
            tgt = jnp.where(rows < 3 * N_HEADS, aug0[rows % N_HEADS] + rows // N_HEADS, -1)
            pm = jnp.where(jnp.arange(2 * D)[None, :] == tgt, -1.0, 0.0)
            qk, vt = _inproj(x2d, _row(g[0]), wn, bn, w_in[:, 2 * D:3 * D].T.astype(BF), B=B, S=S,
                             fox_extra=[wf3, bf3, pm.astype(BF)])
            fox_near = tuple(range(FOX_TQ // ATT_TK))
            mask_t = _near_tables(None, FOX_TQ, fox_near)
            ot = _attention(qk, qk, vt, mask_t, B=B, S=S, n_prog=N_HEADS, q_spec=(LANES, lambda h: h),
                            k_blk=lambda h: N_HEADS + h, v_blk=lambda h: h, bias_blk=lambda h: 0, nq_stack=1,
                            vdim=HEAD_DIM, mode="causal", out_rows=HEAD_DIM, out_dtype=BF, q_mask="none",
                            tq=FOX_TQ, near=[(a, a) for a in fox_near], grp=CAUSAL_GROUP)
            x2d = _outproj([ot], None, fox_w_out[ic].astype(BF), x2d, _row(g[1]), B=B, S=S)
            ic += 1
        x2d = _mlp_ple(x2d, _row(g[2]), mlp_w1[i].astype(BF), mlp_w2[i].astype(BF), _row(g[3]),
                       ple_gate_w[i].astype(BF), p[i].reshape(B * S, -1), ple_w[i].astype(BF))
    return x2d.reshape(B, S, D)
```

```python
import functools
import math

import jax
import jax.numpy as jnp
from jax import lax
from jax.experimental import pallas as pl
from jax.experimental.pallas import tpu as pltpu

BF = jnp.bfloat16
F32 = jnp.float32

D_MODEL = 1024
HEAD_DIM = 64
LANES = 128
NORM_EPS = 1e-6
MASK_VALUE = -1e30
REL_BUCKETS = 32
REL_MAX_DIST = 128
REL_TABLE_LEN = 512
N_HEADS = D_MODEL // HEAD_DIM
DA_HEADS = N_HEADS // 2
NSA_GROUPS = 4
NSA_REP = N_HEADS // NSA_GROUPS
NSA_CMP_LEN = 32
NSA_CMP_STRIDE = 16
NSA_SEL_LEN = 64
NSA_TOP_N = 16
NSA_WINDOW = 512
NSA_FORCE_SCORE = 1e4
N_MIXERS = 3

ATT_TQ = 256
DA_TQ = 512
FOX_TQ = 1024
CAUSAL_GROUP = 4
ATT_TK = 256
VMEM_LIMIT = 56 * 1024 * 1024
CMP_ROW_STEP = 128
PV_CHUNK = 256
ONES_ROWS = 16
LOG2E = math.log2(math.e)


def _cparams(*sem):
    return pltpu.CompilerParams(dimension_semantics=sem, vmem_limit_bytes=VMEM_LIMIT)


def _rms(x, g):
    return x * lax.rsqrt(jnp.mean(x * x, axis=-1, keepdims=True) + NORM_EPS) * g


def _dot(a, b):
    return jnp.dot(a, b, preferred_element_type=F32)


def _dot_nt(a, b):
    return lax.dot_general(a, b, (((1,), (1,)), ((), ())), preferred_element_type=F32)


def _dot_tn(a, b):
    return lax.dot_general(a, b, (((0,), (0,)), ((), ())), preferred_element_type=F32)


def _inproj_body(*refs, n_f32, n_gate, fox, tm, col_chunk):
    it = iter(refs)
    x_ref, g_ref, wn_ref, bn_ref, wt_ref = (next(it) for _ in range(5))
    wf32_ref = next(it) if n_f32 else None
    wg_ref = next(it) if n_gate else None
    if fox:
        wf_ref, bf_ref, pm_ref = next(it), next(it), next(it)
    on_ref, ot_ref = next(it), next(it)
    of32_ref = next(it) if n_f32 else None
    og_ref = next(it) if n_gate else None
    carry_ref = next(it) if fox else None

    h = _rms(x_ref[...], g_ref[...]).astype(BF)
    n_nat = on_ref.shape[1]
    k_off = n_nat // 2
    if fox:
        @pl.when(pl.program_id(1) == 0)
        def _():
            carry_ref[...] = jnp.zeros_like(carry_ref)

        lf = _dot(h, wf_ref[...]) + bf_ref[...]
        lane = lax.broadcasted_iota(jnp.int32, lf.shape, 1)
        row = lax.broadcasted_iota(jnp.int32, lf.shape, 0)
        ls = jnp.minimum(lf, 0.0) - jnp.log1p(jnp.exp(-jnp.abs(lf)))
        c = jnp.where(lane < 3 * N_HEADS, ls, 0.0)
        k = 1
        while k < tm:
            c = c + jnp.where(row >= k, pltpu.roll(c, k, axis=0), 0.0)
            k *= 2
        c = c + carry_ref[...]
        carry_ref[...] = c[tm - 1:tm, :]
        c = c * LOG2E
        hi = c.astype(BF).astype(F32)
        r1 = c - hi
        mid = r1.astype(BF).astype(F32)
        lo = (r1 - mid).astype(BF).astype(F32)
        c3 = jnp.where(lane < N_HEADS, hi, jnp.where(lane < 2 * N_HEADS, mid, lo)).astype(BF)
    if fox:
        lane_b = lax.broadcasted_iota(jnp.int32, (tm, LANES), 1)
        d_c = wn_ref.shape[1] // 2
        for part in range(2):
            yc = _dot(h, wn_ref[:, part * d_c:(part + 1) * d_c])
            for oc in range(0, k_off, col_chunk):
                pieces = []
                for hh in range(oc // LANES, (oc + col_chunk) // LANES):
                    blk = yc[:, (hh // 2) * LANES:(hh // 2 + 1) * LANES]
                    keep = (lane_b < HEAD_DIM) if hh % 2 == 0 else (lane_b >= HEAD_DIM)
                    pieces.append(jnp.where(keep, blk, 0.0))
                y = jnp.concatenate(pieces, axis=1) + bn_ref[:, part * k_off + oc:part * k_off + oc + col_chunk]
                if part == 1:
                    y = y + _dot(c3, pm_ref[:, oc:oc + col_chunk])
                on_ref[:, part * k_off + oc:part * k_off + oc + col_chunk] = y.astype(on_ref.dtype)
    else:
        for c0 in range(0, n_nat, col_chunk):
            y = _dot(h, wn_ref[:, c0:c0 + col_chunk]) + bn_ref[:, c0:c0 + col_chunk]
            on_ref[:, c0:c0 + col_chunk] = y.astype(on_ref.dtype)
    ot_ref[0] = _dot_nt(wt_ref[...], h).astype(ot_ref.dtype)
    if n_f32:
        of32_ref[...] = _dot(h, wf32_ref[...])
    if n_gate:
        og_ref[0] = jax.nn.sigmoid(_dot_nt(wg_ref[...], h))


def _inproj(x2d, g, wn, bn, wt, *, B, S, tm=512, wf32=None, wg=None, fox_extra=None):
    M, D = x2d.shape
    ns = S // tm
    n_nat, n_t = bn.shape[1], wt.shape[0]
    full = lambda a: pl.BlockSpec(a.shape, lambda b, s: (0,) * a.ndim)
    ins = [x2d, g, wn, bn, wt]
    in_specs = [pl.BlockSpec((tm, D), lambda b, s: (b * ns + s, 0)), full(g), full(wn), full(bn), full(wt)]
    out_shape = [jax.ShapeDtypeStruct((M, n_nat), BF), jax.ShapeDtypeStruct((B, n_t, S), BF)]
    out_specs = [pl.BlockSpec((tm, n_nat), lambda b, s: (b * ns + s, 0)),
                 pl.BlockSpec((1, n_t, tm), lambda b, s: (b, 0, s))]
    scratch = []
    if wf32 is not None:
        ins.append(wf32)
        in_specs.append(full(wf32))
        out_shape.append(jax.ShapeDtypeStruct((M, wf32.shape[1]), F32))
        out_specs.append(pl.BlockSpec((tm, wf32.shape[1]), lambda b, s: (b * ns + s, 0)))
    if wg is not None:
        ins.append(wg)
        in_specs.append(full(wg))
        out_shape.append(jax.ShapeDtypeStruct((B, wg.shape[0], S), F32))
        out_specs.append(pl.BlockSpec((1, wg.shape[0], tm), lambda b, s: (b, 0, s)))
    if fox_extra is not None:
        for a in fox_extra:
            ins.append(a)
            in_specs.append(full(a))
        scratch.append(pltpu.VMEM((1, LANES), F32))
    body = functools.partial(_inproj_body, n_f32=wf32 is not None, n_gate=wg is not None,
                             fox=fox_extra is not None, tm=tm, col_chunk=1024 if n_nat % 1024 == 0 else 512)
    return pl.pallas_call(
        body, grid=(B, ns), in_specs=in_specs, out_specs=out_specs, out_shape=out_shape,
        scratch_shapes=scratch, compiler_params=_cparams("arbitrary", "arbitrary"), name="inproj")(*ins)


def _attn_body(*refs, tq, grp, nq_stack, vdim, mode, use_sel, final, lam_init, q_mask, near):
    it = iter(refs)
    q_ref, qn_ref, k_ref, vt_ref, bias_ref = (next(it) for _ in range(5))
    selb_ref, selbn_ref = (next(it), next(it)) if use_sel else (None, None)
    if final == "da":
        lam_ref, subg_ref = next(it), next(it)
    o_ref = next(it)
    qm_ref, m_ref, acc_ref, s_scr, mt_scr, r0_ref = (next(it) for _ in range(6))
    tk = ATT_TK
    lw = nq_stack * tq
    i = pl.program_id(2)
    i_next = jnp.minimum(i + 1, pl.num_programs(2) - 1)

    lane = lax.broadcasted_iota(jnp.int32, (tq, LANES), 1)
    for slot, src_ref in enumerate((q_ref, qn_ref)):
        qt = src_ref[...]
        for r in range(nq_stack):
            blk = qt[:, 0:LANES] if q_mask == "da" else qt[:, r * LANES:(r + 1) * LANES]
            if q_mask == "da":
                keep = (lane < HEAD_DIM) if r == 0 else (lane >= HEAD_DIM)
            elif q_mask == "group_parity":
                keep = (lane // HEAD_DIM) == (pl.program_id(1) % 2)
            else:
                keep = None
            if keep is not None:
                blk = jnp.where(keep, blk, jnp.zeros_like(blk))
            qm_ref[slot, r * tq:(r + 1) * tq, :] = blk
    m_ref[...] = jnp.full_like(m_ref, MASK_VALUE)
    acc_ref[...] = jnp.zeros_like(acc_ref)

    def stage_qk(group, r, next_tile=False):
        rows = len(group) * tk
        ks = [k_ref[pl.ds(pl.multiple_of(kt * tk, tk), tk), :] for kt, _, _ in group]
        s = _dot_nt(ks[0] if len(ks) == 1 else jnp.concatenate(ks, axis=0), qm_ref[int(next_tile)])
        if use_sel:
            slabs = []
            for t, (kt, _, _) in enumerate(group):
                sb8 = (selbn_ref if next_tile else selb_ref)[0, 0, pl.ds(pl.multiple_of((kt // 2) * 8, 8), 8), :]
                sb4 = jnp.where(kt % 2 == 0, sb8[0:4, :], sb8[4:8, :])
                sbt = jnp.concatenate([sb4] * nq_stack, axis=1)
                for j in range(tk // NSA_SEL_LEN):
                    r0 = t * tk + j * NSA_SEL_LEN
                    slabs.append(s[r0:r0 + NSA_SEL_LEN, :] + sbt[j:j + 1, :])
            s = jnp.concatenate(slabs, axis=0)
        s_scr[r, 0:rows, :] = s
        mt_scr[r] = jnp.max(s, axis=0, keepdims=True)

    def stage_softmax_pv(group, r):
        rows = len(group) * tk
        plain = all(bidx is None and fm is None for _, bidx, fm in group)
        vts = [vt_ref[0, :, pl.ds(pl.multiple_of(kt * tk, tk), tk)] for kt, _, _ in group]
        lhs = jnp.concatenate([vts[0] if len(vts) == 1 else jnp.concatenate(vts, axis=1),
                               jnp.ones((ONES_ROWS, rows), BF)], axis=0)
        for c0 in range(0, lw, PV_CHUNK):
            cols = slice(c0, c0 + PV_CHUNK)
            if plain:
                s = s_scr[r, 0:rows, cols]
                mt = mt_scr[r, :, cols]
            else:
                parts = []
                for t, (_, bidx, fm) in enumerate(group):
                    st = s_scr[r, t * tk:(t + 1) * tk, cols]
                    if bidx is not None:
                        q, l0 = divmod(c0, tq)
                        st = st + bias_ref[q, bidx, :, l0:l0 + PV_CHUNK]
                    if fm is not None:
                        st = st + jnp.where(fm, MASK_VALUE, 0.0)
                    parts.append(st)
                s = parts[0] if len(parts) == 1 else jnp.concatenate(parts, axis=0)
                mt = jnp.max(s, axis=0, keepdims=True)
            m_old = m_ref[:, cols]
            m_new = jnp.maximum(m_old, mt)
            alpha = jnp.exp2(m_old - m_new)
            p = jnp.exp2(s - m_new).astype(BF)
            m_ref[:, cols] = m_new
            acc_ref[:, cols] = alpha * acc_ref[:, cols] + _dot(lhs, p)

    def trip(r, cur, nxt, next_tile=False):
        if not isinstance(r, int):
            for val in (0, 1):
                @pl.when(r == val)
                def _(val=val):
                    trip(val, cur, nxt, next_tile)
            return
        if nxt:
            stage_qk(nxt, 1 - r, next_tile)
        stage_softmax_pv(cur, r)

    n_tile = tq // tk
    a0 = near[0][0]
    last_kt = k_ref.shape[0] // tk - 1
    lo_values = sorted({max(ii * n_tile + a0, 0) % grp for ii in range(2 * grp)}) if mode == "causal" else [0]

    def tile_plan(ii):
        base = ii * n_tile
        near_tiles = [(jnp.maximum(base + a, 0), bidx, (base + a < 0) if a < 0 else None) for a, bidx in near]
        n_far = jnp.maximum(base + a0, 0) if mode == "causal" else 0
        trips, lo = n_far // grp, n_far % grp
        tails = {v: [(n_far - v + t, None, None) for t in range(v)] + near_tiles for v in lo_values}
        first = []
        for t in range(grp):
            kt_near = sum(jnp.where(lo == v, tails[v][min(t, len(tails[v]) - 1)][0], 0) for v in lo_values)
            first.append((jnp.where(trips > 0, t, kt_near), None, None))
        return trips, lo, tails, first

    trips, lo, tails, first = tile_plan(i)
    far_group = lambda u: [(jnp.minimum(grp * u + t, last_kt), None, None) for t in range(grp)]
    if mode == "causal":
        r0 = jnp.where(i == 0, 0, r0_ref[0])

        @pl.when(i == 0)
        def _():
            stage_qk(first, 0)

        def body(u, c):
            trip((r0 + u) % 2, far_group(u), far_group(u + 1))
            return c
        lax.fori_loop(0, trips, body, 0)
        first_next = tile_plan(i_next)[3]
    else:
        r0 = 0
        stage_qk(first, 0)

    def run_tail(tail):
        groups = [tail[n:n + grp] for n in range(0, len(tail), grp)]
        for e, group in enumerate(groups):
            r_e = (r0 + trips + e) % 2
            if e + 1 < len(groups):
                trip(r_e, group, groups[e + 1])
            elif mode == "causal":
                trip(r_e, group, first_next, next_tile=True)
                r0_ref[0] = 1 - r_e
            else:
                trip(r_e, group, None)

    if len(lo_values) == 1:
        run_tail(tails[lo_values[0]])
    else:
        for v in lo_values:
            @pl.when(lo == v)
            def _(v=v):
                run_tail(tails[v])

    acc = acc_ref[0:vdim, :]
    l = acc_ref[vdim:vdim + 1, :]
    if final == "da":
        lamv = lam_ref[...]
        lam = (jnp.exp(jnp.sum(lamv[0:1] * lamv[1:2], axis=1, keepdims=True))
               - jnp.exp(jnp.sum(lamv[2:3] * lamv[3:4], axis=1, keepdims=True)) + lam_init)
        o = acc[:, :tq] / l[:, :tq] - lam * (acc[:, tq:] / l[:, tq:])
        o = o * lax.rsqrt(jnp.mean(o * o, axis=0, keepdims=True) + NORM_EPS) * subg_ref[...] * (1.0 - lam_init)
        o_ref[0] = o.astype(o_ref.dtype)
    else:
        for r in range(nq_stack):
            o_ref[0, r * vdim:(r + 1) * vdim, :] = (acc[:, r * tq:(r + 1) * tq] / l[:, r * tq:(r + 1) * tq]).astype(o_ref.dtype)


def _attention(q_arr, k_arr, vt, bias, *, B, S, n_prog, q_spec, k_blk, v_blk, bias_blk, nq_stack, vdim, mode,
               out_rows, out_dtype, q_mask, tq, near, grp=2, selb=None, final="plain", lam=None, subg=None, lam_init=0.0):
    tk = ATT_TK
    nq = S // tq
    lw = nq_stack * tq
    q_w, q_blk = q_spec
    nxt = lambda i: jnp.minimum(i + 1, nq - 1)
    ins = [q_arr, q_arr, k_arr, vt, bias]
    in_specs = [
        pl.BlockSpec((tq, q_w), lambda b, h, i: (b * nq + i, q_blk(h))),
        pl.BlockSpec((tq, q_w), lambda b, h, i: (b * nq + nxt(i), q_blk(h))),
        pl.BlockSpec((S, LANES), lambda b, h, i: (b, k_blk(h))),
        pl.BlockSpec((1, vdim, S), lambda b, h, i: (b, v_blk(h), 0)),
        pl.BlockSpec((nq_stack,) + bias.shape[1:], lambda b, h, i: (bias_blk(h), 0, 0, 0)),
    ]
    if selb is not None:
        ins += [selb, selb]
        in_specs += [pl.BlockSpec((1, 1, selb.shape[2], tq), lambda b, h, i: (b, h, 0, i)),
                     pl.BlockSpec((1, 1, selb.shape[2], tq), lambda b, h, i: (b, h, 0, nxt(i)))]
    if final == "da":
        ins += [lam, subg]
        in_specs += [pl.BlockSpec(lam.shape, lambda b, h, i: (0, 0)), pl.BlockSpec(subg.shape, lambda b, h, i: (0, 0))]
    body = functools.partial(_attn_body, tq=tq, grp=grp, nq_stack=nq_stack, vdim=vdim, mode=mode, use_sel=selb is not None,
                             final=final, lam_init=lam_init, q_mask=q_mask, near=near)
    return pl.pallas_call(
        body, grid=(B, n_prog, nq), in_specs=in_specs,
        out_specs=pl.BlockSpec((1, out_rows, tq), lambda b, h, i: (b, h, i)),
        out_shape=jax.ShapeDtypeStruct((B, out_rows * n_prog, S), out_dtype),
        scratch_shapes=[pltpu.VMEM((2, lw, LANES), BF), pltpu.VMEM((1, lw), F32), pltpu.VMEM((vdim + ONES_ROWS, lw), F32),
                        pltpu.VMEM((2, grp * tk, lw), F32), pltpu.VMEM((2, 1, lw), F32), pltpu.SMEM((1,), jnp.int32)],
        compiler_params=_cparams("arbitrary", "arbitrary", "arbitrary"), name="attn_" + mode + "_" + final)(*ins)


def _compress_body(x_ref, pe_ref, w1_ref, w2_ref, o_ref):
    x = x_ref[0, 0, 0]
    n, half = x.shape
    pe = pe_ref[0]
    first = _dot((x + pe[:, :half]).astype(BF), w1_ref[0, :half, :])
    second = _dot((x + pe[:, half:]).astype(BF), w1_ref[0, half:, :])
    hcur = first + pltpu.roll(second, n - 1, axis=0)
    hcur = 0.5 * hcur * (1.0 + jnp.tanh(math.sqrt(2.0 / math.pi) * (hcur + 0.044715 * (hcur * hcur * hcur))))
    o_ref[0, 0, 0] = _dot(hcur.astype(BF), w2_ref[0])


def _compress(chunks, pe, w1, w2):
    _, B, G, n, K = chunks.shape
    return pl.pallas_call(
        _compress_body, grid=(2, B, G),
        in_specs=[pl.BlockSpec((1, 1, 1, n, K), lambda a, b, g: (a, b, g, 0, 0)),
                  pl.BlockSpec((1, 1, 2 * K), lambda a, b, g: (a, 0, 0)),
                  pl.BlockSpec((1,) + w1.shape[1:], lambda a, b, g: (a, 0, 0)),
                  pl.BlockSpec((1,) + w2.shape[1:], lambda a, b, g: (a, 0, 0))],
        out_specs=pl.BlockSpec((1, 1, 1, n, HEAD_DIM), lambda a, b, g: (a, b, g, 0, 0)),
        out_shape=jax.ShapeDtypeStruct((2, B, G, n, HEAD_DIM), F32),
        compiler_params=_cparams("arbitrary", "arbitrary", "arbitrary"), name="nsa_compress")(chunks, pe, w1, w2)


def _cmp_body(q_ref, kc_ref, vct_ref, bc_ref, ov_ref, o_ref, selb_ref, s_ref, imp_ref):
    tq = ATT_TQ
    i = pl.program_id(2)
    n_cmp = kc_ref.shape[2]
    n_sel = selb_ref.shape[2]
    band = bc_ref.shape[1]
    bc = jnp.concatenate([bc_ref[r] for r in range(NSA_REP)], axis=1)
    qt = q_ref[...]
    lane = lax.broadcasted_iota(jnp.int32, (tq, LANES), 1)
    keep = (lane // HEAD_DIM) == (pl.program_id(1) % 2)
    qm = jnp.concatenate(
        [jnp.where(keep, qt[:, r * LANES:(r + 1) * LANES], jnp.zeros((tq, LANES), BF)) for r in range(NSA_REP)], axis=0)

    def attend(rows):
        s = _dot_nt(kc_ref[0, 0, 0:rows, :], qm)
        row = lax.broadcasted_iota(jnp.int32, s.shape, 0)
        s_ref[0:rows, :] = jnp.where(row < (band // 2) * (i + 1), s, MASK_VALUE)

        @pl.when(i == 0)
        def _():
            s_ref[0:band // 2, :] = s_ref[0:band // 2, :] + bc[band // 2:, :]

        @pl.when(i > 0)
        def _():
            r0 = pl.multiple_of((band // 2) * (i - 1), band // 2)
            s_ref[pl.ds(r0, band), :] = s_ref[pl.ds(r0, band), :] + bc

        ps = None
        for r in range(NSA_REP):
            s = s_ref[0:rows, r * tq:(r + 1) * tq]
            m = jnp.max(s, axis=0, keepdims=True)
            e = jnp.exp2(s - m)
            l = jnp.sum(e, axis=0, keepdims=True)
            inv = jnp.where(m > 0.5 * MASK_VALUE, 1.0 / l, 0.0)
            p = e * inv
            o_ref[0, r * HEAD_DIM:(r + 1) * HEAD_DIM, :] = _dot(vct_ref[0, 0, :, 0:rows], p.astype(BF))
            ps = p if ps is None else ps + p
        p_hi = ps.astype(BF)
        r1 = ps - p_hi.astype(F32)
        p_mid = r1.astype(BF)
        p_lo = (r1 - p_mid.astype(F32)).astype(BF)
        ov = ov_ref[:, 0:rows]
        imp_ref[...] = _dot(ov, p_hi) + _dot(ov, p_mid) + _dot(ov, p_lo)

    def select(rows):
        imp = imp_ref[0:rows, :]
        j = lax.broadcasted_iota(jnp.int32, (rows, tq), 0)
        t = i * tq + lax.broadcasted_iota(jnp.int32, (rows, tq), 1)
        cur = t // NSA_SEL_LEN
        forced = (j == 0) | (j == cur) | (j == cur - 1)
        w = jnp.where(j <= cur, jnp.where(forced, NSA_FORCE_SCORE, imp), -1.0)
        jf = j.astype(F32)
        for _ in range(min(NSA_TOP_N, rows)):
            mx = jnp.max(w, axis=0, keepdims=True)
            idx = jnp.min(jnp.where(w == mx, jf, float(rows)), axis=0, keepdims=True)
            w = jnp.where(jf == idx, -2.0, w)
        selb_ref[0, 0, 0:rows, :] = jnp.where(w == -2.0, 0.0, MASK_VALUE)
        if rows < n_sel:
            selb_ref[0, 0, rows:n_sel, :] = jnp.full((n_sel - rows, tq), MASK_VALUE, F32)

    step = min(CMP_ROW_STEP, n_cmp)
    tiles_per_step = step // (band // 2)
    sel_per_cmp = NSA_SEL_LEN // NSA_CMP_STRIDE
    for v in range(n_cmp // step):
        @pl.when((i >= v * tiles_per_step) & (i < (v + 1) * tiles_per_step))
        def _(v=v):
            attend((v + 1) * step)
            select((v + 1) * step // sel_per_cmp)


def _cmp_attention(qn, kc, vct, bc, ov, *, B, S, q_coloff):
    tq = ATT_TQ
    nq = S // tq
    n_cmp = kc.shape[2]
    n_sel = S // NSA_SEL_LEN
    return pl.pallas_call(
        _cmp_body, grid=(B, NSA_GROUPS, nq),
        in_specs=[pl.BlockSpec((tq, NSA_REP * LANES), lambda b, g, i: (b * nq + i, q_coloff + g // 2)),
                  pl.BlockSpec((1, 1, n_cmp, LANES), lambda b, g, i: (b, g, 0, 0)),
                  pl.BlockSpec((1, 1, HEAD_DIM, n_cmp), lambda b, g, i: (b, g, 0, 0)),
                  pl.BlockSpec((NSA_REP,) + bc.shape[1:], lambda b, g, i: (g, 0, 0)),
                  pl.BlockSpec(ov.shape, lambda b, g, i: (0, 0))],
        out_specs=[pl.BlockSpec((1, NSA_REP * HEAD_DIM, tq), lambda b, g, i: (b, g, i)),
                   pl.BlockSpec((1, 1, n_sel, tq), lambda b, g, i: (b, g, 0, i))],
        out_shape=[jax.ShapeDtypeStruct((B, D_MODEL, S), F32), jax.ShapeDtypeStruct((B, NSA_GROUPS, n_sel, S), F32)],
        scratch_shapes=[pltpu.VMEM((n_cmp, NSA_REP * tq), F32), pltpu.VMEM((n_sel, tq), F32)],
        compiler_params=_cparams("arbitrary", "arbitrary", "arbitrary"), name="nsa_cmp_topk")(qn, kc, vct, bc, ov)


def _outproj_body(*refs, n_o):
    it = iter(refs)
    o_refs = [next(it) for _ in range(n_o)]
    gt_ref = next(it) if n_o > 1 else None
    w_ref, x_ref, g_ref, out_ref = (next(it) for _ in range(4))
    if n_o == 1:
        ot = o_refs[0][0]
    else:
        gts = gt_ref[0]
        parts = []
        for hd in range(N_HEADS):
            rows = slice(hd * HEAD_DIM, (hd + 1) * HEAD_DIM)
            acc = gts[hd:hd + 1, :] * o_refs[0][0, rows, :]
            for b in range(1, n_o):
                acc = acc + gts[b * N_HEADS + hd:b * N_HEADS + hd + 1, :] * o_refs[b][0, rows, :]
            parts.append(acc.astype(BF))
        ot = jnp.concatenate(parts, axis=0)
    y = _dot_tn(ot, w_ref[...])
    out_ref[...] = x_ref[...] + _rms(y, g_ref[...])


def _outproj(o_list, gt, w, x2d, g, *, B, S, tm=512):
    M, D = x2d.shape
    ns = S // tm
    ins = list(o_list)
    in_specs = [pl.BlockSpec((1, D, tm), lambda b, s: (b, 0, s)) for _ in o_list]
    if gt is not None:
        ins.append(gt)
        in_specs.append(pl.BlockSpec((1, gt.shape[1], tm), lambda b, s: (b, 0, s)))
    ins += [w, x2d, g]
    in_specs += [pl.BlockSpec(w.shape, lambda b, s: (0, 0)),
                 pl.BlockSpec((tm, D), lambda b, s: (b * ns + s, 0)),
                 pl.BlockSpec(g.shape, lambda b, s: (0, 0))]
    return pl.pallas_call(
        functools.partial(_outproj_body, n_o=len(o_list)), grid=(B, ns), in_specs=in_specs,
        out_specs=pl.BlockSpec((tm, D), lambda b, s: (b * ns + s, 0)),
        out_shape=jax.ShapeDtypeStruct((M, D), F32),
        compiler_params=_cparams("arbitrary", "arbitrary"), name="outproj")(*ins)


def _mlp_body(x_ref, g2_ref, w1_ref, w2_ref, g3_ref, wg_ref, p_ref, wp_ref, out_ref, h_ref, acc_ref):
    f = pl.program_id(1)

    @pl.when(f == 0)
    def _():
        h_ref[...] = _rms(x_ref[...], g2_ref[...]).astype(BF)
        acc_ref[...] = jnp.zeros_like(acc_ref)

    a = jnp.maximum(_dot(h_ref[...], w1_ref[...]), 0.0)
    acc_ref[...] += _dot((a * a).astype(BF), w2_ref[...])

    @pl.when(f == pl.num_programs(1) - 1)
    def _():
        x2 = x_ref[...] + _rms(acc_ref[...], g3_ref[...])
        gate = jax.nn.sigmoid(_dot(x2.astype(BF), wg_ref[...]))
        out_ref[...] = x2 + gate * _dot(p_ref[...].astype(BF), wp_ref[...])


def _mlp_ple(x2d, g2, w1, w2, g3, wg, p2d, wp, *, tm=1024, tf=512):
    M, D = x2d.shape
    FF = w1.shape[1]
    PD = p2d.shape[1]
    const = lambda a: pl.BlockSpec(a.shape, lambda m, f: (0, 0))
    return pl.pallas_call(
        _mlp_body, grid=(M // tm, FF // tf),
        in_specs=[pl.BlockSpec((tm, D), lambda m, f: (m, 0)), const(g2),
                  pl.BlockSpec((D, tf), lambda m, f: (0, f)), pl.BlockSpec((tf, D), lambda m, f: (f, 0)),
                  const(g3), const(wg), pl.BlockSpec((tm, PD), lambda m, f: (m, 0)), const(wp)],
        out_specs=pl.BlockSpec((tm, D), lambda m, f: (m, 0)),
        out_shape=jax.ShapeDtypeStruct((M, D), F32),
        scratch_shapes=[pltpu.VMEM((tm, D), BF), pltpu.VMEM((tm, D), F32)],
        compiler_params=_cparams("arbitrary", "arbitrary"), name="mlp_ple")(x2d, g2, w1, w2, g3, wg, p2d, wp)


def _t5_bucket(dist):
    n = jnp.maximum(dist, 0)
    max_exact = REL_BUCKETS // 2
    nf = jnp.maximum(n, 1).astype(F32)
    large = max_exact + (jnp.log(nf / max_exact) / math.log(REL_MAX_DIST / max_exact)
                         * (REL_BUCKETS - max_exact)).astype(jnp.int32)
    large = jnp.minimum(large, REL_BUCKETS - 1)
    return jnp.where(n < max_exact, n, large)


def _bias_by_distance(rel_bias):
    tb = rel_bias[_t5_bucket(jnp.arange(REL_TABLE_LEN))].astype(F32)
    return ((tb - rel_bias[REL_BUCKETS - 1].astype(F32)[None, :]) * LOG2E).T


def _toeplitz(fn, rows, cols, off, row_stride=1):
    n = row_stride * rows + cols
    d = jnp.arange(n)
    d = jnp.where(d < cols, d, d - n)
    v = fn(d + off)
    flat = jnp.tile(v, (1,) * (v.ndim - 1) + (rows,))[..., :rows * (n - row_stride)]
    return flat.reshape(v.shape[:-1] + (rows, n - row_stride))[..., :cols]


def _bias_fn(tbs, hi_limit=None):
    def fn(d):
        val = jnp.zeros((1,) + d.shape, F32) if tbs is None else tbs[:, jnp.clip(d, 0, REL_TABLE_LEN - 1)]
        bad = d < 0
        if hi_limit is not None:
            bad = bad | (d >= hi_limit)
        return jnp.where(bad[None, :], MASK_VALUE, val)
    return fn


def _near_tables(tbs, tq, offsets, hi_limit=None):
    return jnp.stack([_toeplitz(_bias_fn(tbs, hi_limit), ATT_TK, tq, -a * ATT_TK) for a in offsets], axis=1)


def _cmp_band_table(tbs):
    tq = ATT_TQ
    band = 2 * (tq // NSA_CMP_STRIDE)
    return _toeplitz(_bias_fn(tbs), band, tq, tq - (NSA_CMP_LEN - 1), row_stride=NSA_CMP_STRIDE)


def _nsa_q_perm():
    cols = []
    for p in range(NSA_GROUPS // 2):
        for r in range(NSA_REP):
            for g in (2 * p, 2 * p + 1):
                h = g * NSA_REP + r
                cols.extend(range(h * HEAD_DIM, (h + 1) * HEAD_DIM))
    return jnp.asarray(cols, jnp.int32)


def _row(v):
    return v.reshape(1, -1).astype(F32)


def kernel(x, p, rel_bias, norm_g, mlp_w1, mlp_w2, ple_w, ple_gate_w, da_w_in, da_lambda, da_subln, da_w_out,
           nsa_w_in, nsa_cmp_pe, nsa_cmp_w1, nsa_cmp_w2, nsa_w_out, fox_w_in, fox_b_f, fox_w_out):
    B, S, D = x.shape
    depth = p.shape[0]
    scale = HEAD_DIM ** -0.5 * LOG2E
    tbs = _bias_by_distance(rel_bias)
    da_near = tuple(range(-1, DA_TQ // ATT_TK))
    da_bias = _near_tables(tbs, DA_TQ, da_near)
    x2d = x.reshape(B * S, D)
    ia = ib = ic = 0
    for i in range(depth):
        g = norm_g[i]
        kind = i % N_MIXERS
        if kind == 0:
            lam_init = 0.8 - 0.6 * math.exp(-0.3 * i)
            w_in = da_w_in[ia]
            wn = jnp.concatenate([w_in[:, :D] * scale, w_in[:, D:2 * D]], axis=1).astype(BF)
            wt = w_in[:, 2 * D:].T.astype(BF)
            qk, vt = _inproj(x2d, _row(g[0]), wn, jnp.zeros((1, 2 * D), F32), wt, B=B, S=S)
            ot = _attention(qk, qk, vt, da_bias, B=B, S=S, n_prog=DA_HEADS, q_spec=(LANES, lambda h: h),
                            k_blk=lambda h: DA_HEADS + h, v_blk=lambda h: h, bias_blk=lambda h: h, nq_stack=2,
                            vdim=2 * HEAD_DIM, mode="causal", out_rows=2 * HEAD_DIM, out_dtype=BF, q_mask="da",
                            tq=DA_TQ, near=[(da_near[0] - 1, None)] + [(a, n) for n, a in enumerate(da_near)], grp=CAUSAL_GROUP,
                            final="da", lam=da_lambda[ia].astype(F32), subg=da_subln[ia].reshape(-1, 1).astype(F32),
                            lam_init=lam_init)
            x2d = _outproj([ot], None, da_w_out[ia].astype(BF), x2d, _row(g[1]), B=B, S=S)
            ia += 1
        elif kind == 1:
            w_in = nsa_w_in[ib]
            kvd = NSA_GROUPS * HEAD_DIM
            wq = w_in[:, :D][:, _nsa_q_perm()] * scale
            kv = [w_in[:, D + a * kvd:D + (a + 1) * kvd] for a in range(6)]
            wn = jnp.concatenate([wq, kv[2], kv[4]], axis=1).astype(BF)
            wt = jnp.concatenate([kv[3], kv[5]], axis=1).T.astype(BF)
            wf32 = jnp.concatenate([kv[0], kv[1]], axis=1).astype(BF)
            wgate = w_in[:, D + 6 * kvd:].reshape(D, N_HEADS, 3).transpose(2, 1, 0).reshape(3 * N_HEADS, D).astype(BF)
            qn, vt, cmp_in, gates_t = _inproj(x2d, _row(g[0]), wn, jnp.zeros((1, wn.shape[1]), F32), wt, B=B, S=S,
                                              wf32=wf32, wg=wgate)
            n_chunk = S // NSA_CMP_STRIDE
            cm = cmp_in.reshape(B, n_chunk, NSA_CMP_STRIDE, 2, NSA_GROUPS, HEAD_DIM).transpose(3, 0, 4, 1, 2, 5)
            cm = cm.reshape(2, B, NSA_GROUPS, n_chunk, NSA_CMP_STRIDE * HEAD_DIM)
            pe = nsa_cmp_pe[ib].reshape(2, 1, NSA_CMP_LEN * HEAD_DIM).astype(F32)
            kvc = _compress(cm, pe, nsa_cmp_w1[ib].astype(BF), nsa_cmp_w2[ib].astype(BF))
            kc = jnp.concatenate([kvc[0], kvc[0]], axis=-1).astype(BF)
            vct = kvc[1].transpose(0, 1, 3, 2).astype(BF)
            jj = jnp.arange(S // NSA_SEL_LEN)[:, None]
            nn = jnp.arange(n_chunk)[None, :]
            n_cmp_blocks = (S - NSA_CMP_LEN) // NSA_CMP_STRIDE + 1
            ov = ((nn * NSA_CMP_STRIDE < (jj + 1) * NSA_SEL_LEN) & (nn * NSA_CMP_STRIDE + NSA_CMP_LEN - 1 >= jj * NSA_SEL_LEN)
                  & (nn < n_cmp_blocks)).astype(BF)
            cmp_band = _cmp_band_table(tbs)
            oc_t, selb = _cmp_attention(qn, kc, vct, cmp_band, ov, B=B, S=S, q_coloff=0)
            nsa_kw = dict(B=B, S=S, n_prog=NSA_GROUPS, q_spec=(NSA_REP * LANES, lambda h: h // 2),
                          bias_blk=lambda h: h, nq_stack=NSA_REP, vdim=HEAD_DIM, out_rows=NSA_REP * HEAD_DIM,
                          out_dtype=F32, q_mask="group_parity", tq=ATT_TQ)
            kblk0 = D // LANES
            sel_bias = _near_tables(tbs, ATT_TQ, (-1, 0))
            win_bias = _near_tables(tbs, ATT_TQ, (-2, -1, 0), hi_limit=NSA_WINDOW)
            os_t = _attention(qn, qn, vt, sel_bias, mode="causal", near=[(-1, 0), (0, 1)], grp=CAUSAL_GROUP,
                              k_blk=lambda h: kblk0 + h // 2, v_blk=lambda h: h, selb=selb, **nsa_kw)
            ow_t = _attention(qn, qn, vt, win_bias, mode="window", near=[(-2, 0), (-1, 1), (0, 2)],
                              k_blk=lambda h: kblk0 + 2 + h // 2, v_blk=lambda h: NSA_GROUPS + h, **nsa_kw)
            x2d = _outproj([oc_t, os_t, ow_t], gates_t, nsa_w_out[ib].astype(BF), x2d, _row(g[1]), B=B, S=S)
            ib += 1
        else:
            w_in = fox_w_in[ic]
            wn = jnp.concatenate([w_in[:, :D] * scale, w_in[:, D:2 * D]], axis=1).astype(BF)
            hh = jnp.arange(N_HEADS)
            aug0 = hh * LANES + jnp.where(hh % 2 == 0, HEAD_DIM, 0)
            sel = jnp.arange(2 * D)[None, :] - aug0[:, None]
            aug = jnp.any((sel >= 0) & (sel < 3), axis=0).astype(F32).reshape(1, 2 * D)
            bn = jnp.concatenate([aug, jnp.zeros((1, 2 * D), F32)], axis=1)
            wf = w_in[:, 3 * D:]
            wf3 = jnp.concatenate([wf, wf, wf, jnp.zeros((D, LANES - 3 * N_HEADS), F32)], axis=1).astype(BF)
            bf3 = jnp.concatenate([fox_b_f[ic]] * 3 + [jnp.zeros((LANES - 3 * N_HEADS,), F32)]).reshape(1, LANES).astype(F32)
            rows = jnp.arange(LANES)[:, None]
            tgt = jnp.where(rows < 3 * N_HEADS, aug0[rows % N_HEADS] + rows // N_HEADS, -1)
            pm = jnp.where(jnp.arange(2 * D)[None, :] == tgt, -1.0, 0.0)
            qk, vt = _inproj(x2d, _row(g[0]), wn, bn, w_in[:, 2 * D:3 * D].T.astype(BF), B=B, S=S,
                             fox_extra=[wf3, bf3, pm.astype(BF)])
            fox_near = tuple(range(FOX_TQ // ATT_TK))
            mask_t = _near_tables(None, FOX_TQ, fox_near)
            ot = _attention(qk, qk, vt, mask_t, B=B, S=S, n_prog=N_HEADS, q_spec=(LANES, lambda h: h),
                            k_blk=lambda h: N_HEADS + h, v_blk=lambda h: h, bias_blk=lambda h: 0, nq_stack=1,
                            vdim=HEAD_DIM, mode="causal", out_rows=HEAD_DIM, out_dtype=BF, q_mask="none",
                            tq=FOX_TQ, near=[(a, a) for a in fox_near], grp=CAUSAL_GROUP)
            x2d = _outproj([ot], None, fox_w_out[ic].astype(BF), x2d, _row(g[1]), B=B, S=S)
            ic += 1
        x2d = _mlp_ple(x2d, _row(g[2]), mlp_w1[i].astype(BF), mlp_w2[i].astype(BF), _row(g[3]),
                       ple_gate_w[i].astype(BF), p[i].reshape(B * S, -1), ple_w[i].astype(BF))
    return x2d.reshape(B, S, D)
```

```python
import functools
import math

import jax
import jax.numpy as jnp
from jax import lax
from jax.experimental import pallas as pl
from jax.experimental.pallas import tpu as pltpu

BF = jnp.bfloat16
F32 = jnp.float32

D_MODEL = 1024
HEAD_DIM = 64
LANES = 128
NORM_EPS = 1e-6
MASK_VALUE = -1e30
REL_BUCKETS = 32
REL_MAX_DIST = 128
REL_TABLE_LEN = 512
N_HEADS = D_MODEL // HEAD_DIM
DA_HEADS = N_HEADS // 2
NSA_GROUPS = 4
NSA_REP = N_HEADS // NSA_GROUPS
NSA_CMP_LEN = 32
NSA_CMP_STRIDE = 16
NSA_SEL_LEN = 64
NSA_TOP_N = 16
NSA_WINDOW = 512
NSA_FORCE_SCORE = 1e4
N_MIXERS = 3

ATT_TQ = 256
DA_TQ = 512
FOX_TQ = 1024
CAUSAL_GROUP = 4
ATT_TK = 256
VMEM_LIMIT = 56 * 1024 * 1024
CMP_ROW_STEP = 128
PV_CHUNK = 256
ONES_ROWS = 16
LOG2E = math.log2(math.e)


def _cparams(*sem):
    return pltpu.CompilerParams(dimension_semantics=sem, vmem_limit_bytes=VMEM_LIMIT)


def _rms(x, g):
    return x * lax.rsqrt(jnp.mean(x * x, axis=-1, keepdims=True) + NORM_EPS) * g


def _dot(a, b):
    return jnp.dot(a, b, preferred_element_type=F32)


def _dot_nt(a, b):
    return lax.dot_general(a, b, (((1,), (1,)), ((), ())), preferred_element_type=F32)


def _dot_tn(a, b):
    return lax.dot_general(a, b, (((0,), (0,)), ((), ())), preferred_element_type=F32)


def _inproj_body(*refs, n_f32, n_gate, fox, tm, col_chunk):
    it = iter(refs)
    x_ref, g_ref, wn_ref, bn_ref, wt_ref = (next(it) for _ in range(5))
    wf32_ref = next(it) if n_f32 else None
    wg_ref = next(it) if n_gate else None
    if fox:
        wf_ref, bf_ref, pm_ref = next(it), next(it), next(it)
    on_ref, ot_ref = next(it), next(it)
    of32_ref = next(it) if n_f32 else None
    og_ref = next(it) if n_gate else None
    carry_ref = next(it) if fox else None

    h = _rms(x_ref[...], g_ref[...]).astype(BF)
    n_nat = on_ref.shape[1]
    k_off = n_nat // 2
    if fox:
        @pl.when(pl.program_id(1) == 0)
        def _():
            carry_ref[...] = jnp.zeros_like(carry_ref)

        lf = _dot(h, wf_ref[...]) + bf_ref[...]
        lane = lax.broadcasted_iota(jnp.int32, lf.shape, 1)
        row = lax.broadcasted_iota(jnp.int32, lf.shape, 0)
        ls = jnp.minimum(lf, 0.0) - jnp.log1p(jnp.exp(-jnp.abs(lf)))
        c = jnp.where(lane < 3 * N_HEADS, ls, 0.0)
        k = 1
        while k < tm:
            c = c + jnp.where(row >= k, pltpu.roll(c, k, axis=0), 0.0)
            k *= 2
        c = c + carry_ref[...]
        carry_ref[...] = c[tm - 1:tm, :]
        c = c * LOG2E
        hi = c.astype(BF).astype(F32)
        r1 = c - hi
        mid = r1.astype(BF).astype(F32)
        lo = (r1 - mid).astype(BF).astype(F32)
        c3 = jnp.where(lane < N_HEADS, hi, jnp.where(lane < 2 * N_HEADS, mid, lo)).astype(BF)
    if fox:
        lane_b = lax.broadcasted_iota(jnp.int32, (tm, LANES), 1)
        d_c = wn_ref.shape[1] // 2
        for part in range(2):
            yc = _dot(h, wn_ref[:, part * d_c:(part + 1) * d_c])
            for oc in range(0, k_off, col_chunk):
                pieces = []
                for hh in range(oc // LANES, (oc + col_chunk) // LANES):
                    blk = yc[:, (hh // 2) * LANES:(hh // 2 + 1) * LANES]
                    keep = (lane_b < HEAD_DIM) if hh % 2 == 0 else (lane_b >= HEAD_DIM)
                    pieces.append(jnp.where(keep, blk, 0.0))
                y = jnp.concatenate(pieces, axis=1) + bn_ref[:, part * k_off + oc:part * k_off + oc + col_chunk]
                if part == 1:
                    y = y + _dot(c3, pm_ref[:, oc:oc + col_chunk])
                on_ref[:, part * k_off + oc:part * k_off + oc + col_chunk] = y.astype(on_ref.dtype)
    else:
        for c0 in range(0, n_nat, col_chunk):
            y = _dot(h, wn_ref[:, c0:c0 + col_chunk]) + bn_ref[:, c0:c0 + col_chunk]
            on_ref[:, c0:c0 + col_chunk] = y.astype(on_ref.dtype)
    ot_ref[0] = _dot_nt(wt_ref[...], h).astype(ot_ref.dtype)
    if n_f32:
        of32_ref[...] = _dot(h, wf32_ref[...])
    if n_gate:
        og_ref[0] = jax.nn.sigmoid(_dot_nt(wg_ref[...], h))


def _inproj(x2d, g, wn, bn, wt, *, B, S, tm=512, wf32=None, wg=None, fox_extra=None):
    M, D = x2d.shape
    ns = S // tm
    n_nat, n_t = bn.shape[1], wt.shape[0]
    full = lambda a: pl.BlockSpec(a.shape, lambda b, s: (0,) * a.ndim)
    ins = [x2d, g, wn, bn, wt]
    in_specs = [pl.BlockSpec((tm, D), lambda b, s: (b * ns + s, 0)), full(g), full(wn), full(bn), full(wt)]
    out_shape = [jax.ShapeDtypeStruct((M, n_nat), BF), jax.ShapeDtypeStruct((B, n_t, S), BF)]
    out_specs = [pl.BlockSpec((tm, n_nat), lambda b, s: (b * ns + s, 0)),
                 pl.BlockSpec((1, n_t, tm), lambda b, s: (b, 0, s))]
    scratch = []
    if wf32 is not None:
        ins.append(wf32)
        in_specs.append(full(wf32))
        out_shape.append(jax.ShapeDtypeStruct((M, wf32.shape[1]), F32))
        out_specs.append(pl.BlockSpec((tm, wf32.shape[1]), lambda b, s: (b * ns + s, 0)))
    if wg is not None:
        ins.append(wg)
        in_specs.append(full(wg))
        out_shape.append(jax.ShapeDtypeStruct((B, wg.shape[0], S), F32))
        out_specs.append(pl.BlockSpec((1, wg.shape[0], tm), lambda b, s: (b, 0, s)))
    if fox_extra is not None:
        for a in fox_extra:
            ins.append(a)
            in_specs.append(full(a))
        scratch.append(pltpu.VMEM((1, LANES), F32))
    body = functools.partial(_inproj_body, n_f32=wf32 is not None, n_gate=wg is not None,
                             fox=fox_extra is not None, tm=tm, col_chunk=1024 if n_nat % 1024 == 0 else 512)
    return pl.pallas_call(
        body, grid=(B, ns), in_specs=in_specs, out_specs=out_specs, out_shape=out_shape,
        scratch_shapes=scratch, compiler_params=_cparams("arbitrary", "arbitrary"), name="inproj")(*ins)


def _attn_body(*refs, tq, grp, nq_stack, vdim, mode, use_sel, final, lam_init, q_mask, near):
    it = iter(refs)
    q_ref, qn_ref, k_ref, vt_ref, bias_ref = (next(it) for _ in range(5))
    selb_ref, selbn_ref = (next(it), next(it)) if use_sel else (None, None)
    if final == "da":
        lam_ref, subg_ref = next(it), next(it)
    o_ref = next(it)
    qm_ref, m_ref, acc_ref, s_scr, mt_scr, r0_ref = (next(it) for _ in range(6))
    tk = ATT_TK
    lw = nq_stack * tq
    i = pl.program_id(2)
    i_next = jnp.minimum(i + 1, pl.num_programs(2) - 1)

    lane = lax.broadcasted_iota(jnp.int32, (tq, LANES), 1)
    for slot, src_ref in enumerate((q_ref, qn_ref)):
        qt = src_ref[...]
        for r in range(nq_stack):
            blk = qt[:, 0:LANES] if q_mask == "da" else qt[:, r * LANES:(r + 1) * LANES]
            if q_mask == "da":
                keep = (lane < HEAD_DIM) if r == 0 else (lane >= HEAD_DIM)
            elif q_mask == "group_parity":
                keep = (lane // HEAD_DIM) == (pl.program_id(1) % 2)
            else:
                keep = None
            if keep is not None:
                blk = jnp.where(keep, blk, jnp.zeros_like(blk))
            qm_ref[slot, r * tq:(r + 1) * tq, :] = blk
    m_ref[...] = jnp.full_like(m_ref, MASK_VALUE)
    acc_ref[...] = jnp.zeros_like(acc_ref)

    def stage_qk(group, r, next_tile=False):
        rows = len(group) * tk
        ks = [k_ref[pl.ds(pl.multiple_of(kt * tk, tk), tk), :] for kt, _, _ in group]
        s = _dot_nt(ks[0] if len(ks) == 1 else jnp.concatenate(ks, axis=0), qm_ref[int(next_tile)])
        if use_sel:
            slabs = []
            for t, (kt, _, _) in enumerate(group):
                sb8 = (selbn_ref if next_tile else selb_ref)[0, 0, pl.ds(pl.multiple_of((kt // 2) * 8, 8), 8), :]
                sb4 = jnp.where(kt % 2 == 0, sb8[0:4, :], sb8[4:8, :])
                sbt = jnp.concatenate([sb4] * nq_stack, axis=1)
                for j in range(tk // NSA_SEL_LEN):
                    r0 = t * tk + j * NSA_SEL_LEN
                    slabs.append(s[r0:r0 + NSA_SEL_LEN, :] + sbt[j:j + 1, :])
            s = jnp.concatenate(slabs, axis=0)
        s_scr[r, 0:rows, :] = s
        mt_scr[r] = jnp.max(s, axis=0, keepdims=True)

    def stage_softmax_pv(group, r):
        rows = len(group) * tk
        plain = all(bidx is None and fm is None for _, bidx, fm in group)
        vts = [vt_ref[0, :, pl.ds(pl.multiple_of(kt * tk, tk), tk)] for kt, _, _ in group]
        lhs = jnp.concatenate([vts[0] if len(vts) == 1 else jnp.concatenate(vts, axis=1),
                               jnp.ones((ONES_ROWS, rows), BF)], axis=0)
        for c0 in range(0, lw, PV_CHUNK):
            cols = slice(c0, c0 + PV_CHUNK)
            if plain:
                s = s_scr[r, 0:rows, cols]
                mt = mt_scr[r, :, cols]
            else:
                parts = []
                for t, (_, bidx, fm) in enumerate(group):
                    st = s_scr[r, t * tk:(t + 1) * tk, cols]
                    if bidx is not None:
                        q, l0 = divmod(c0, tq)
                        st = st + bias_ref[q, bidx, :, l0:l0 + PV_CHUNK]
                    if fm is not None:
                        st = st + jnp.where(fm, MASK_VALUE, 0.0)
                    parts.append(st)
                s = parts[0] if len(parts) == 1 else jnp.concatenate(parts, axis=0)
                mt = jnp.max(s, axis=0, keepdims=True)
            m_old = m_ref[:, cols]
            m_new = jnp.maximum(m_old, mt)
            alpha = jnp.exp2(m_old - m_new)
            p = jnp.exp2(s - m_new).astype(BF)
            m_ref[:, cols] = m_new
            acc_ref[:, cols] = alpha * acc_ref[:, cols] + _dot(lhs, p)

    def trip(r, cur, nxt, next_tile=False):
        if not isinstance(r, int):
            for val in (0, 1):
                @pl.when(r == val)
                def _(val=val):
                    trip(val, cur, nxt, next_tile)
            return
        if nxt:
            stage_qk(nxt, 1 - r, next_tile)
        stage_softmax_pv(cur, r)

    n_tile = tq // tk
    a0 = near[0][0]
    last_kt = k_ref.shape[0] // tk - 1
    lo_values = sorted({max(ii * n_tile + a0, 0) % grp for ii in range(2 * grp)}) if mode == "causal" else [0]

    def tile_plan(ii):
        base = ii * n_tile
        near_tiles = [(jnp.maximum(base + a, 0), bidx, (base + a < 0) if a < 0 else None) for a, bidx in near]
        n_far = jnp.maximum(base + a0, 0) if mode == "causal" else 0
        trips, lo = n_far // grp, n_far % grp
        tails = {v: [(n_far - v + t, None, None) for t in range(v)] + near_tiles for v in lo_values}
        first = []
        for t in range(grp):
            kt_near = sum(jnp.where(lo == v, tails[v][min(t, len(tails[v]) - 1)][0], 0) for v in lo_values)
            first.append((jnp.where(trips > 0, t, kt_near), None, None))
        return trips, lo, tails, first

    trips, lo, tails, first = tile_plan(i)
    far_group = lambda u: [(jnp.minimum(grp * u + t, last_kt), None, None) for t in range(grp)]
    if mode == "causal":
        r0 = jnp.where(i == 0, 0, r0_ref[0])

        @pl.when(i == 0)
        def _():
            stage_qk(first, 0)

        def body(u, c):
            trip((r0 + u) % 2, far_group(u), far_group(u + 1))
            return c
        lax.fori_loop(0, trips, body, 0)
        first_next = tile_plan(i_next)[3]
    else:
        r0 = 0
        stage_qk(first, 0)

    def run_tail(tail):
        groups = [tail[n:n + grp] for n in range(0, len(tail), grp)]
        for e, group in enumerate(groups):
            r_e = (r0 + trips + e) % 2
            if e + 1 < len(groups):
                trip(r_e, group, groups[e + 1])
            elif mode == "causal":
                trip(r_e, group, first_next, next_tile=True)
                r0_ref[0] = 1 - r_e
            else:
                trip(r_e, group, None)

    if len(lo_values) == 1:
        run_tail(tails[lo_values[0]])
    else:
        for v in lo_values:
            @pl.when(lo == v)
            def _(v=v):
                run_tail(tails[v])

    acc = acc_ref[0:vdim, :]
    l = acc_ref[vdim:vdim + 1, :]
    if final == "da":
        lamv = lam_ref[...]
        lam = (jnp.exp(jnp.sum(lamv[0:1] * lamv[1:2], axis=1, keepdims=True))
               - jnp.exp(jnp.sum(lamv[2:3] * lamv[3:4], axis=1, keepdims=True)) + lam_init)
        o = acc[:, :tq] / l[:, :tq] - lam * (acc[:, tq:] / l[:, tq:])
        o = o * lax.rsqrt(jnp.mean(o * o, axis=0, keepdims=True) + NORM_EPS) * subg_ref[...] * (1.0 - lam_init)
        o_ref[0] = o.astype(o_ref.dtype)
    else:
        for r in range(nq_stack):
            o_ref[0, r * vdim:(r + 1) * vdim, :] = (acc[:, r * tq:(r + 1) * tq] / l[:, r * tq:(r + 1) * tq]).astype(o_ref.dtype)


def _attention(q_arr, k_arr, vt, bias, *, B, S, n_prog, q_spec, k_blk, v_blk, bias_blk, nq_stack, vdim, mode,
               out_rows, out_dtype, q_mask, tq, near, grp=2, selb=None, final="plain", lam=None, subg=None, lam_init=0.0):
    tk = ATT_TK
    nq = S // tq
    lw = nq_stack * tq
    q_w, q_blk = q_spec
    nxt = lambda i: jnp.minimum(i + 1, nq - 1)
    ins = [q_arr, q_arr, k_arr, vt, bias]
    in_specs = [
        pl.BlockSpec((tq, q_w), lambda b, h, i: (b * nq + i, q_blk(h))),
        pl.BlockSpec((tq, q_w), lambda b, h, i: (b * nq + nxt(i), q_blk(h))),
        pl.BlockSpec((S, LANES), lambda b, h, i: (b, k_blk(h))),
        pl.BlockSpec((1, vdim, S), lambda b, h, i: (b, v_blk(h), 0)),
        pl.BlockSpec((nq_stack,) + bias.shape[1:], lambda b, h, i: (bias_blk(h), 0, 0, 0)),
    ]
    if selb is not None:
        ins += [selb, selb]
        in_specs += [pl.BlockSpec((1, 1, selb.shape[2], tq), lambda b, h, i: (b, h, 0, i)),
                     pl.BlockSpec((1, 1, selb.shape[2], tq), lambda b, h, i: (b, h, 0, nxt(i)))]
    if final == "da":
        ins += [lam, subg]
        in_specs += [pl.BlockSpec(lam.shape, lambda b, h, i: (0, 0)), pl.BlockSpec(subg.shape, lambda b, h, i: (0, 0))]
    body = functools.partial(_attn_body, tq=tq, grp=grp, nq_stack=nq_stack, vdim=vdim, mode=mode, use_sel=selb is not None,
                             final=final, lam_init=lam_init, q_mask=q_mask, near=near)
    return pl.pallas_call(
        body, grid=(B, n_prog, nq), in_specs=in_specs,
        out_specs=pl.BlockSpec((1, out_rows, tq), lambda b, h, i: (b, h, i)),
        out_shape=jax.ShapeDtypeStruct((B, out_rows * n_prog, S), out_dtype),
        scratch_shapes=[pltpu.VMEM((2, lw, LANES), BF), pltpu.VMEM((1, lw), F32), pltpu.VMEM((vdim + ONES_ROWS, lw), F32),
                        pltpu.VMEM((2, grp * tk, lw), F32), pltpu.VMEM((2, 1, lw), F32), pltpu.SMEM((1,), jnp.int32)],
        compiler_params=_cparams("arbitrary", "arbitrary", "arbitrary"), name="attn_" + mode + "_" + final)(*ins)


def _compress_body(x_ref, pe_ref, w1_ref, w2_ref, o_ref):
    x = x_ref[0, 0, 0]
    n, half = x.shape
    pe = pe_ref[0]
    first = _dot((x + pe[:, :half]).astype(BF), w1_ref[0, :half, :])
    second = _dot((x + pe[:, half:]).astype(BF), w1_ref[0, half:, :])
    hcur = first + pltpu.roll(second, n - 1, axis=0)
    hcur = 0.5 * hcur * (1.0 + jnp.tanh(math.sqrt(2.0 / math.pi) * (hcur + 0.044715 * (hcur * hcur * hcur))))
    o_ref[0, 0, 0] = _dot(hcur.astype(BF), w2_ref[0])


def _compress(chunks, pe, w1, w2):
    _, B, G, n, K = chunks.shape
    return pl.pallas_call(
        _compress_body, grid=(2, B, G),
        in_specs=[pl.BlockSpec((1, 1, 1, n, K), lambda a, b, g: (a, b, g, 0, 0)),
                  pl.BlockSpec((1, 1, 2 * K), lambda a, b, g: (a, 0, 0)),
                  pl.BlockSpec((1,) + w1.shape[1:], lambda a, b, g: (a, 0, 0)),
                  pl.BlockSpec((1,) + w2.shape[1:], lambda a, b, g: (a, 0, 0))],
        out_specs=pl.BlockSpec((1, 1, 1, n, HEAD_DIM), lambda a, b, g: (a, b, g, 0, 0)),
        out_shape=jax.ShapeDtypeStruct((2, B, G, n, HEAD_DIM), F32),
        compiler_params=_cparams("arbitrary", "arbitrary", "arbitrary"), name="nsa_compress")(chunks, pe, w1, w2)


def _cmp_body(q_ref, kc_ref, vct_ref, bc_ref, ov_ref, o_ref, selb_ref, s_ref, imp_ref):
    tq = ATT_TQ
    i = pl.program_id(2)
    n_cmp = kc_ref.shape[2]
    n_sel = selb_ref.shape[2]
    band = bc_ref.shape[1]
    bc = jnp.concatenate([bc_ref[r] for r in range(NSA_REP)], axis=1)
    qt = q_ref[...]
    lane = lax.broadcasted_iota(jnp.int32, (tq, LANES), 1)
    keep = (lane // HEAD_DIM) == (pl.program_id(1) % 2)
    qm = jnp.concatenate(
        [jnp.where(keep, qt[:, r * LANES:(r + 1) * LANES], jnp.zeros((tq, LANES), BF)) for r in range(NSA_REP)], axis=0)

    def attend(rows):
        s = _dot_nt(kc_ref[0, 0, 0:rows, :], qm)
        row = lax.broadcasted_iota(jnp.int32, s.shape, 0)
        s_ref[0:rows, :] = jnp.where(row < (band // 2) * (i + 1), s, MASK_VALUE)

        @pl.when(i == 0)
        def _():
            s_ref[0:band // 2, :] = s_ref[0:band // 2, :] + bc[band // 2:, :]

        @pl.when(i > 0)
        def _():
            r0 = pl.multiple_of((band // 2) * (i - 1), band // 2)
            s_ref[pl.ds(r0, band), :] = s_ref[pl.ds(r0, band), :] + bc

        ps = None
        for r in range(NSA_REP):
            s = s_ref[0:rows, r * tq:(r + 1) * tq]
            m = jnp.max(s, axis=0, keepdims=True)
            e = jnp.exp2(s - m)
            l = jnp.sum(e, axis=0, keepdims=True)
            inv = jnp.where(m > 0.5 * MASK_VALUE, 1.0 / l, 0.0)
            p = e * inv
            o_ref[0, r * HEAD_DIM:(r + 1) * HEAD_DIM, :] = _dot(vct_ref[0, 0, :, 0:rows], p.astype(BF))
            ps = p if ps is None else ps + p
        p_hi = ps.astype(BF)
        r1 = ps - p_hi.astype(F32)
        p_mid = r1.astype(BF)
        p_lo = (r1 - p_mid.astype(F32)).astype(BF)
        ov = ov_ref[:, 0:rows]
        imp_ref[...] = _dot(ov, p_hi) + _dot(ov, p_mid) + _dot(ov, p_lo)

    def select(rows):
        imp = imp_ref[0:rows, :]
        j = lax.broadcasted_iota(jnp.int32, (rows, tq), 0)
        t = i * tq + lax.broadcasted_iota(jnp.int32, (rows, tq), 1)
        cur = t // NSA_SEL_LEN
        forced = (j == 0) | (j == cur) | (j == cur - 1)
        w = jnp.where(j <= cur, jnp.where(forced, NSA_FORCE_SCORE, imp), -1.0)
        jf = j.astype(F32)
        for _ in range(min(NSA_TOP_N, rows)):
            mx = jnp.max(w, axis=0, keepdims=True)
            idx = jnp.min(jnp.where(w == mx, jf, float(rows)), axis=0, keepdims=True)
            w = jnp.where(jf == idx, -2.0, w)
        selb_ref[0, 0, 0:rows, :] = jnp.where(w == -2.0, 0.0, MASK_VALUE)
        if rows < n_sel:
            selb_ref[0, 0, rows:n_sel, :] = jnp.full((n_sel - rows, tq), MASK_VALUE, F32)

    step = min(CMP_ROW_STEP, n_cmp)
    tiles_per_step = step // (band // 2)
    sel_per_cmp = NSA_SEL_LEN // NSA_CMP_STRIDE
    for v in range(n_cmp // step):
        @pl.when((i >= v * tiles_per_step) & (i < (v + 1) * tiles_per_step))
        def _(v=v):
            attend((v + 1) * step)
            select((v + 1) * step // sel_per_cmp)


def _cmp_attention(qn, kc, vct, bc, ov, *, B, S, q_coloff):
    tq = ATT_TQ
    nq = S // tq
    n_cmp = kc.shape[2]
    n_sel = S // NSA_SEL_LEN
    return pl.pallas_call(
        _cmp_body, grid=(B, NSA_GROUPS, nq),
        in_specs=[pl.BlockSpec((tq, NSA_REP * LANES), lambda b, g, i: (b * nq + i, q_coloff + g // 2)),
                  pl.BlockSpec((1, 1, n_cmp, LANES), lambda b, g, i: (b, g, 0, 0)),
                  pl.BlockSpec((1, 1, HEAD_DIM, n_cmp), lambda b, g, i: (b, g, 0, 0)),
                  pl.BlockSpec((NSA_REP,) + bc.shape[1:], lambda b, g, i: (g, 0, 0)),
                  pl.BlockSpec(ov.shape, lambda b, g, i: (0, 0))],
        out_specs=[pl.BlockSpec((1, NSA_REP * HEAD_DIM, tq), lambda b, g, i: (b, g, i)),
                   pl.BlockSpec((1, 1, n_sel, tq), lambda b, g, i: (b, g, 0, i))],
        out_shape=[jax.ShapeDtypeStruct((B, D_MODEL, S), F32), jax.ShapeDtypeStruct((B, NSA_GROUPS, n_sel, S), F32)],
        scratch_shapes=[pltpu.VMEM((n_cmp, NSA_REP * tq), F32), pltpu.VMEM((n_sel, tq), F32)],
        compiler_params=_cparams("arbitrary", "arbitrary", "arbitrary"), name="nsa_cmp_topk")(qn, kc, vct, bc, ov)


def _outproj_body(*refs, n_o):
    it = iter(refs)
    o_refs = [next(it) for _ in range(n_o)]
    gt_ref = next(it) if n_o > 1 else None
    w_ref, x_ref, g_ref, out_ref = (next(it) for _ in range(4))
    if n_o == 1:
        ot = o_refs[0][0]
    else:
        gts = gt_ref[0]
        parts = []
        for hd in range(N_HEADS):
            rows = slice(hd * HEAD_DIM, (hd + 1) * HEAD_DIM)
            acc = gts[hd:hd + 1, :] * o_refs[0][0, rows, :]
            for b in range(1, n_o):
                acc = acc + gts[b * N_HEADS + hd:b * N_HEADS + hd + 1, :] * o_refs[b][0, rows, :]
            parts.append(acc.astype(BF))
        ot = jnp.concatenate(parts, axis=0)
    y = _dot_tn(ot, w_ref[...])
    out_ref[...] = x_ref[...] + _rms(y, g_ref[...])


def _outproj(o_list, gt, w, x2d, g, *, B, S, tm=512):
    M, D = x2d.shape
    ns = S // tm
    ins = list(o_list)
    in_specs = [pl.BlockSpec((1, D, tm), lambda b, s: (b, 0, s)) for _ in o_list]
    if gt is not None:
        ins.append(gt)
        in_specs.append(pl.BlockSpec((1, gt.shape[1], tm), lambda b, s: (b, 0, s)))
    ins += [w, x2d, g]
    in_specs += [pl.BlockSpec(w.shape, lambda b, s: (0, 0)),
                 pl.BlockSpec((tm, D), lambda b, s: (b * ns + s, 0)),
                 pl.BlockSpec(g.shape, lambda b, s: (0, 0))]
    return pl.pallas_call(
        functools.partial(_outproj_body, n_o=len(o_list)), grid=(B, ns), in_specs=in_specs,
        out_specs=pl.BlockSpec((tm, D), lambda b, s: (b * ns + s, 0)),
        out_shape=jax.ShapeDtypeStruct((M, D), F32),
        compiler_params=_cparams("arbitrary", "arbitrary"), name="outproj")(*ins)


def _mlp_body(x_ref, g2_ref, w1_ref, w2_ref, g3_ref, wg_ref, p_ref, wp_ref, out_ref, h_ref, acc_ref):
    f = pl.program_id(1)

    @pl.when(f == 0)
    def _():
        h_ref[...] = _rms(x_ref[...], g2_ref[...]).astype(BF)
        acc_ref[...] = jnp.zeros_like(acc_ref)

    a = jnp.maximum(_dot(h_ref[...], w1_ref[0].astype(BF)), 0.0)
    acc_ref[...] += _dot((a * a).astype(BF), w2_ref[0].astype(BF))

    @pl.when(f == pl.num_programs(1) - 1)
    def _():
        x2 = x_ref[...] + _rms(acc_ref[...], g3_ref[...])
        gate = jax.nn.sigmoid(_dot(x2.astype(BF), wg_ref[...]))
        out_ref[...] = x2 + gate * _dot(p_ref[...].astype(BF), wp_ref[...])


def _mlp_ple(x2d, g2, w1, w2, layer, g3, wg, p2d, wp, *, tm=1024, tf=512):
    M, D = x2d.shape
    FF = w1.shape[2]
    PD = p2d.shape[1]
    const = lambda a: pl.BlockSpec(a.shape, lambda m, f: (0, 0))
    return pl.pallas_call(
        _mlp_body, grid=(M // tm, FF // tf),
        in_specs=[pl.BlockSpec((tm, D), lambda m, f: (m, 0)), const(g2),
                  pl.BlockSpec((1, D, tf), lambda m, f: (layer, 0, f)),
                  pl.BlockSpec((1, tf, D), lambda m, f: (layer, f, 0)),
                  const(g3), const(wg), pl.BlockSpec((tm, PD), lambda m, f: (m, 0)), const(wp)],
        out_specs=pl.BlockSpec((tm, D), lambda m, f: (m, 0)),
        out_shape=jax.ShapeDtypeStruct((M, D), F32),
        scratch_shapes=[pltpu.VMEM((tm, D), BF), pltpu.VMEM((tm, D), F32)],
        compiler_params=_cparams("arbitrary", "arbitrary"), name="mlp_ple")(x2d, g2, w1, w2, g3, wg, p2d, wp)


def _t5_bucket(dist):
    n = jnp.maximum(dist, 0)
    max_exact = REL_BUCKETS // 2
    nf = jnp.maximum(n, 1).astype(F32)
    large = max_exact + (jnp.log(nf / max_exact) / math.log(REL_MAX_DIST / max_exact)
                         * (REL_BUCKETS - max_exact)).astype(jnp.int32)
    large = jnp.minimum(large, REL_BUCKETS - 1)
    return jnp.where(n < max_exact, n, large)


def _bias_by_distance(rel_bias):
    tb = rel_bias[_t5_bucket(jnp.arange(REL_TABLE_LEN))].astype(F32)
    return ((tb - rel_bias[REL_BUCKETS - 1].astype(F32)[None, :]) * LOG2E).T


def _toeplitz(fn, rows, cols, off, row_stride=1):
    n = row_stride * rows + cols
    d = jnp.arange(n)
    d = jnp.where(d < cols, d, d - n)
    v = fn(d + off)
    flat = jnp.tile(v, (1,) * (v.ndim - 1) + (rows,))[..., :rows * (n - row_stride)]
    return flat.reshape(v.shape[:-1] + (rows, n - row_stride))[..., :cols]


def _bias_fn(tbs, hi_limit=None):
    def fn(d):
        val = jnp.zeros((1,) + d.shape, F32) if tbs is None else tbs[:, jnp.clip(d, 0, REL_TABLE_LEN - 1)]
        bad = d < 0
        if hi_limit is not None:
            bad = bad | (d >= hi_limit)
        return jnp.where(bad[None, :], MASK_VALUE, val)
    return fn


def _near_tables(tbs, tq, offsets, hi_limit=None):
    return jnp.stack([_toeplitz(_bias_fn(tbs, hi_limit), ATT_TK, tq, -a * ATT_TK) for a in offsets], axis=1)


def _cmp_band_table(tbs):
    tq = ATT_TQ
    band = 2 * (tq // NSA_CMP_STRIDE)
    return _toeplitz(_bias_fn(tbs), band, tq, tq - (NSA_CMP_LEN - 1), row_stride=NSA_CMP_STRIDE)


def _nsa_q_perm():
    cols = []
    for p in range(NSA_GROUPS // 2):
        for r in range(NSA_REP):
            for g in (2 * p, 2 * p + 1):
                h = g * NSA_REP + r
                cols.extend(range(h * HEAD_DIM, (h + 1) * HEAD_DIM))
    return jnp.asarray(cols, jnp.int32)


def _row(v):
    return v.reshape(1, -1).astype(F32)


def kernel(x, p, rel_bias, norm_g, mlp_w1, mlp_w2, ple_w, ple_gate_w, da_w_in, da_lambda, da_subln, da_w_out,
           nsa_w_in, nsa_cmp_pe, nsa_cmp_w1, nsa_cmp_w2, nsa_w_out, fox_w_in, fox_b_f, fox_w_out):
    B, S, D = x.shape
    depth = p.shape[0]
    scale = HEAD_DIM ** -0.5 * LOG2E
    tbs = _bias_by_distance(rel_bias)
    da_near = tuple(range(-1, DA_TQ // ATT_TK))
    da_bias = _near_tables(tbs, DA_TQ, da_near)
    x2d = x.reshape(B * S, D)
    ia = ib = ic = 0
    for i in range(depth):
        g = norm_g[i]
        kind = i % N_MIXERS
        if kind == 0:
            lam_init = 0.8 - 0.6 * math.exp(-0.3 * i)
            w_in = da_w_in[ia]
            wn = jnp.concatenate([w_in[:, :D] * scale, w_in[:, D:2 * D]], axis=1).astype(BF)
            wt = w_in[:, 2 * D:].T.astype(BF)
            qk, vt = _inproj(x2d, _row(g[0]), wn, jnp.zeros((1, 2 * D), F32), wt, B=B, S=S)
            ot = _attention(qk, qk, vt, da_bias, B=B, S=S, n_prog=DA_HEADS, q_spec=(LANES, lambda h: h),
                            k_blk=lambda h: DA_HEADS + h, v_blk=lambda h: h, bias_blk=lambda h: h, nq_stack=2,
                            vdim=2 * HEAD_DIM, mode="causal", out_rows=2 * HEAD_DIM, out_dtype=BF, q_mask="da",
                            tq=DA_TQ, near=[(da_near[0] - 1, None)] + [(a, n) for n, a in enumerate(da_near)], grp=CAUSAL_GROUP,
                            final="da", lam=da_lambda[ia].astype(F32), subg=da_subln[ia].reshape(-1, 1).astype(F32),
                            lam_init=lam_init)
            x2d = _outproj([ot], None, da_w_out[ia].astype(BF), x2d, _row(g[1]), B=B, S=S)
            ia += 1
        elif kind == 1:
            w_in = nsa_w_in[ib]
            kvd = NSA_GROUPS * HEAD_DIM
            wq = w_in[:, :D][:, _nsa_q_perm()] * scale
            kv = [w_in[:, D + a * kvd:D + (a + 1) * kvd] for a in range(6)]
            wn = jnp.concatenate([wq, kv[2], kv[4]], axis=1).astype(BF)
            wt = jnp.concatenate([kv[3], kv[5]], axis=1).T.astype(BF)
            wf32 = jnp.concatenate([kv[0], kv[1]], axis=1).astype(BF)
            wgate = w_in[:, D + 6 * kvd:].reshape(D, N_HEADS, 3).transpose(2, 1, 0).reshape(3 * N_HEADS, D).astype(BF)
            qn, vt, cmp_in, gates_t = _inproj(x2d, _row(g[0]), wn, jnp.zeros((1, wn.shape[1]), F32), wt, B=B, S=S,
                                              wf32=wf32, wg=wgate)
            n_chunk = S // NSA_CMP_STRIDE
            cm = cmp_in.reshape(B, n_chunk, NSA_CMP_STRIDE, 2, NSA_GROUPS, HEAD_DIM).transpose(3, 0, 4, 1, 2, 5)
            cm = cm.reshape(2, B, NSA_GROUPS, n_chunk, NSA_CMP_STRIDE * HEAD_DIM)
            pe = nsa_cmp_pe[ib].reshape(2, 1, NSA_CMP_LEN * HEAD_DIM).astype(F32)
            kvc = _compress(cm, pe, nsa_cmp_w1[ib].astype(BF), nsa_cmp_w2[ib].astype(BF))
            kc = jnp.concatenate([kvc[0], kvc[0]], axis=-1).astype(BF)
            vct = kvc[1].transpose(0, 1, 3, 2).astype(BF)
            jj = jnp.arange(S // NSA_SEL_LEN)[:, None]
            nn = jnp.arange(n_chunk)[None, :]
            n_cmp_blocks = (S - NSA_CMP_LEN) // NSA_CMP_STRIDE + 1
            ov = ((nn * NSA_CMP_STRIDE < (jj + 1) * NSA_SEL_LEN) & (nn * NSA_CMP_STRIDE + NSA_CMP_LEN - 1 >= jj * NSA_SEL_LEN)
                  & (nn < n_cmp_blocks)).astype(BF)
            cmp_band = _cmp_band_table(tbs)
            oc_t, selb = _cmp_attention(qn, kc, vct, cmp_band, ov, B=B, S=S, q_coloff=0)
            nsa_kw = dict(B=B, S=S, n_prog=NSA_GROUPS, q_spec=(NSA_REP * LANES, lambda h: h // 2),
                          bias_blk=lambda h: h, nq_stack=NSA_REP, vdim=HEAD_DIM, out_rows=NSA_REP * HEAD_DIM,
                          out_dtype=F32, q_mask="group_parity", tq=ATT_TQ)
            kblk0 = D // LANES
            sel_bias = _near_tables(tbs, ATT_TQ, (-1, 0))
            win_bias = _near_tables(tbs, ATT_TQ, (-2, -1, 0), hi_limit=NSA_WINDOW)
            os_t = _attention(qn, qn, vt, sel_bias, mode="causal", near=[(-1, 0), (0, 1)], grp=CAUSAL_GROUP,
                              k_blk=lambda h: kblk0 + h // 2, v_blk=lambda h: h, selb=selb, **nsa_kw)
            ow_t = _attention(qn, qn, vt, win_bias, mode="window", near=[(-2, 0), (-1, 1), (0, 2)],
                              k_blk=lambda h: kblk0 + 2 + h // 2, v_blk=lambda h: NSA_GROUPS + h, **nsa_kw)
            x2d = _outproj([oc_t, os_t, ow_t], gates_t, nsa_w_out[ib].astype(BF), x2d, _row(g[1]), B=B, S=S)
            ib += 1
        else:
            w_in = fox_w_in[ic]
            wn = jnp.concatenate([w_in[:, :D] * scale, w_in[:, D:2 * D]], axis=1).astype(BF)
            hh = jnp.arange(N_HEADS)
            aug0 = hh * LANES + jnp.where(hh % 2 == 0, HEAD_DIM, 0)
            sel = jnp.arange(2 * D)[None, :] - aug0[:, None]
            aug = jnp.any((sel >= 0) & (sel < 3), axis=0).astype(F32).reshape(1, 2 * D)
            bn = jnp.concatenate([aug, jnp.zeros((1, 2 * D), F32)], axis=1)
            wf = w_in[:, 3 * D:]
            wf3 = jnp.concatenate([wf, wf, wf, jnp.zeros((D, LANES - 3 * N_HEADS), F32)], axis=1).astype(BF)
            bf3 = jnp.concatenate([fox_b_f[ic]] * 3 + [jnp.zeros((LANES - 3 * N_HEADS,), F32)]).reshape(1, LANES).astype(F32)
            rows = jnp.arange(LANES)[:, None]
            tgt = jnp.where(rows < 3 * N_HEADS, aug0[rows % N_HEADS] + rows // N_HEADS, -1)
            pm = jnp.where(jnp.arange(2 * D)[None, :] == tgt, -1.0, 0.0)
            qk, vt = _inproj(x2d, _row(g[0]), wn, bn, w_in[:, 2 * D:3 * D].T.astype(BF), B=B, S=S,
                             fox_extra=[wf3, bf3, pm.astype(BF)])
            fox_near = tuple(range(FOX_TQ // ATT_TK))
            mask_t = _near_tables(None, FOX_TQ, fox_near)
            ot = _attention(qk, qk, vt, mask_t, B=B, S=S, n_prog=N_HEADS, q_spec=(LANES, lambda h: h),
                            k_blk=lambda h: N_HEADS + h, v_blk=lambda h: h, bias_blk=lambda h: 0, nq_stack=1,
                            vdim=HEAD_DIM, mode="causal", out_rows=HEAD_DIM, out_dtype=BF, q_mask="none",
                            tq=FOX_TQ, near=[(a, a) for a in fox_near], grp=CAUSAL_GROUP)
            x2d = _outproj([ot], None, fox_w_out[ic].astype(BF), x2d, _row(g[1]), B=B, S=S)
            ic += 1
        x2d = _mlp_ple(x2d, _row(g[2]), mlp_w1, mlp_w2, i, _row(g[3]),
                       ple_gate_w[i].astype(BF), p[i].reshape(B * S, -1), ple_w[i].astype(BF))
    return x2d.reshape(B, S, D)
```

```python
import functools
import math

import jax
import jax.numpy as jnp
from jax import lax
from jax.experimental import pallas as pl
from jax.experimental.pallas import tpu as pltpu

BF = jnp.bfloat16
F32 = jnp.float32

D_MODEL = 1024
HEAD_DIM = 64
LANES = 128
NORM_EPS = 1e-6
MASK_VALUE = -1e30
REL_BUCKETS = 32
REL_MAX_DIST = 128
REL_TABLE_LEN = 512
N_HEADS = D_MODEL // HEAD_DIM
DA_HEADS = N_HEADS // 2
NSA_GROUPS = 4
NSA_REP = N_HEADS // NSA_GROUPS
NSA_CMP_LEN = 32
NSA_CMP_STRIDE = 16
NSA_SEL_LEN = 64
NSA_TOP_N = 16
NSA_WINDOW = 512
NSA_FORCE_SCORE = 1e4
N_MIXERS = 3

ATT_TQ = 256
DA_TQ = 512
FOX_TQ = 1024
CAUSAL_GROUP = 4
ATT_TK = 256
VMEM_LIMIT = 56 * 1024 * 1024
CMP_ROW_STEP = 128
PV_CHUNK = 256
ONES_ROWS = 16
LOG2E = math.log2(math.e)


def _cparams(*sem):
    return pltpu.CompilerParams(dimension_semantics=sem, vmem_limit_bytes=VMEM_LIMIT)


def _rms(x, g):
    return x * lax.rsqrt(jnp.mean(x * x, axis=-1, keepdims=True) + NORM_EPS) * g


def _dot(a, b):
    return jnp.dot(a, b, preferred_element_type=F32)


def _dot_nt(a, b):
    return lax.dot_general(a, b, (((1,), (1,)), ((), ())), preferred_element_type=F32)


def _dot_tn(a, b):
    return lax.dot_general(a, b, (((0,), (0,)), ((), ())), preferred_element_type=F32)


def _inproj_body(*refs, n_f32, n_gate, fox, tm, col_chunk):
    it = iter(refs)
    x_ref, g_ref, wn_ref, bn_ref, wt_ref = (next(it) for _ in range(5))
    wf32_ref = next(it) if n_f32 else None
    wg_ref = next(it) if n_gate else None
    if fox:
        wf_ref, bf_ref, pm_ref = next(it), next(it), next(it)
    on_ref, ot_ref = next(it), next(it)
    of32_ref = next(it) if n_f32 else None
    og_ref = next(it) if n_gate else None
    carry_ref = next(it) if fox else None

    h = _rms(x_ref[...], g_ref[...]).astype(BF)
    n_nat = on_ref.shape[1]
    k_off = n_nat // 2
    if fox:
        @pl.when(pl.program_id(1) == 0)
        def _():
            carry_ref[...] = jnp.zeros_like(carry_ref)

        lf = _dot(h, wf_ref[...]) + bf_ref[...]
        lane = lax.broadcasted_iota(jnp.int32, lf.shape, 1)
        row = lax.broadcasted_iota(jnp.int32, lf.shape, 0)
        ls = jnp.minimum(lf, 0.0) - jnp.log1p(jnp.exp(-jnp.abs(lf)))
        c = jnp.where(lane < 3 * N_HEADS, ls, 0.0)
        k = 1
        while k < tm:
            c = c + jnp.where(row >= k, pltpu.roll(c, k, axis=0), 0.0)
            k *= 2
        c = c + carry_ref[...]
        carry_ref[...] = c[tm - 1:tm, :]
        c = c * LOG2E
        hi = c.astype(BF).astype(F32)
        r1 = c - hi
        mid = r1.astype(BF).astype(F32)
        lo = (r1 - mid).astype(BF).astype(F32)
        c3 = jnp.where(lane < N_HEADS, hi, jnp.where(lane < 2 * N_HEADS, mid, lo)).astype(BF)
    if fox:
        lane_b = lax.broadcasted_iota(jnp.int32, (tm, LANES), 1)
        d_c = wn_ref.shape[1] // 2
        for part in range(2):
            yc = _dot(h, wn_ref[:, part * d_c:(part + 1) * d_c])
            for oc in range(0, k_off, col_chunk):
                pieces = []
                for hh in range(oc // LANES, (oc + col_chunk) // LANES):
                    blk = yc[:, (hh // 2) * LANES:(hh // 2 + 1) * LANES]
                    keep = (lane_b < HEAD_DIM) if hh % 2 == 0 else (lane_b >= HEAD_DIM)
                    pieces.append(jnp.where(keep, blk, 0.0))
                y = jnp.concatenate(pieces, axis=1) + bn_ref[:, part * k_off + oc:part * k_off + oc + col_chunk]
                if part == 1:
                    y = y + _dot(c3, pm_ref[:, oc:oc + col_chunk])
                on_ref[:, part * k_off + oc:part * k_off + oc + col_chunk] = y.astype(on_ref.dtype)
    else:
        for c0 in range(0, n_nat, col_chunk):
            y = _dot(h, wn_ref[:, c0:c0 + col_chunk]) + bn_ref[:, c0:c0 + col_chunk]
            on_ref[:, c0:c0 + col_chunk] = y.astype(on_ref.dtype)
    ot_ref[0] = _dot_nt(wt_ref[...], h).astype(ot_ref.dtype)
    if n_f32:
        of32_ref[...] = _dot(h, wf32_ref[...])
    if n_gate:
        og_ref[0] = jax.nn.sigmoid(_dot_nt(wg_ref[...], h))


def _inproj(x2d, g, wn, bn, wt, *, B, S, tm=512, wf32=None, wg=None, fox_extra=None):
    M, D = x2d.shape
    ns = S // tm
    n_nat, n_t = bn.shape[1], wt.shape[0]
    full = lambda a: pl.BlockSpec(a.shape, lambda b, s: (0,) * a.ndim)
    ins = [x2d, g, wn, bn, wt]
    in_specs = [pl.BlockSpec((tm, D), lambda b, s: (b * ns + s, 0)), full(g), full(wn), full(bn), full(wt)]
    out_shape = [jax.ShapeDtypeStruct((M, n_nat), BF), jax.ShapeDtypeStruct((B, n_t, S), BF)]
    out_specs = [pl.BlockSpec((tm, n_nat), lambda b, s: (b * ns + s, 0)),
                 pl.BlockSpec((1, n_t, tm), lambda b, s: (b, 0, s))]
    scratch = []
    if wf32 is not None:
        ins.append(wf32)
        in_specs.append(full(wf32))
        out_shape.append(jax.ShapeDtypeStruct((M, wf32.shape[1]), F32))
        out_specs.append(pl.BlockSpec((tm, wf32.shape[1]), lambda b, s: (b * ns + s, 0)))
    if wg is not None:
        ins.append(wg)
        in_specs.append(full(wg))
        out_shape.append(jax.ShapeDtypeStruct((B, wg.shape[0], S), F32))
        out_specs.append(pl.BlockSpec((1, wg.shape[0], tm), lambda b, s: (b, 0, s)))
    if fox_extra is not None:
        for a in fox_extra:
            ins.append(a)
            in_specs.append(full(a))
        scratch.append(pltpu.VMEM((1, LANES), F32))
    body = functools.partial(_inproj_body, n_f32=wf32 is not None, n_gate=wg is not None,
                             fox=fox_extra is not None, tm=tm, col_chunk=1024 if n_nat % 1024 == 0 else 512)
    return pl.pallas_call(
        body, grid=(B, ns), in_specs=in_specs, out_specs=out_specs, out_shape=out_shape,
        scratch_shapes=scratch, compiler_params=_cparams("arbitrary", "arbitrary"), name="inproj")(*ins)


def _attn_body(*refs, tq, grp, nq_stack, vdim, mode, use_sel, final, lam_init, q_mask, near):
    it = iter(refs)
    q_ref, qn_ref, k_ref, vt_ref, bias_ref = (next(it) for _ in range(5))
    selb_ref, selbn_ref = (next(it), next(it)) if use_sel else (None, None)
    if final == "da":
        lam_ref, subg_ref = next(it), next(it)
    o_ref = next(it)
    qm_ref, m_ref, acc_ref, s_scr, mt_scr, r0_ref = (next(it) for _ in range(6))
    tk = ATT_TK
    lw = nq_stack * tq
    i = pl.program_id(2)
    i_next = jnp.minimum(i + 1, pl.num_programs(2) - 1)

    lane = lax.broadcasted_iota(jnp.int32, (tq, LANES), 1)
    for slot, src_ref in enumerate((q_ref, qn_ref)):
        qt = src_ref[...]
        for r in range(nq_stack):
            blk = qt[:, 0:LANES] if q_mask == "da" else qt[:, r * LANES:(r + 1) * LANES]
            if q_mask == "da":
                keep = (lane < HEAD_DIM) if r == 0 else (lane >= HEAD_DIM)
            elif q_mask == "group_parity":
                keep = (lane // HEAD_DIM) == (pl.program_id(1) % 2)
            else:
                keep = None
            if keep is not None:
                blk = jnp.where(keep, blk, jnp.zeros_like(blk))
            qm_ref[slot, r * tq:(r + 1) * tq, :] = blk
    m_ref[...] = jnp.full_like(m_ref, MASK_VALUE)
    acc_ref[...] = jnp.zeros_like(acc_ref)

    def stage_qk(group, r, next_tile=False):
        rows = len(group) * tk
        ks = [k_ref[pl.ds(pl.multiple_of(kt * tk, tk), tk), :] for kt, _, _ in group]
        s = _dot_nt(ks[0] if len(ks) == 1 else jnp.concatenate(ks, axis=0), qm_ref[int(next_tile)])
        if use_sel:
            slabs = []
            for t, (kt, _, _) in enumerate(group):
                sb8 = (selbn_ref if next_tile else selb_ref)[0, 0, pl.ds(pl.multiple_of((kt // 2) * 8, 8), 8), :]
                sb4 = jnp.where(kt % 2 == 0, sb8[0:4, :], sb8[4:8, :])
                sbt = jnp.concatenate([sb4] * nq_stack, axis=1)
                for j in range(tk // NSA_SEL_LEN):
                    r0 = t * tk + j * NSA_SEL_LEN
                    slabs.append(s[r0:r0 + NSA_SEL_LEN, :] + sbt[j:j + 1, :])
            s = jnp.concatenate(slabs, axis=0)
        s_scr[r, 0:rows, :] = s
        mt_scr[r] = jnp.max(s, axis=0, keepdims=True)

    def stage_softmax_pv(group, r):
        rows = len(group) * tk
        plain = all(bidx is None and fm is None for _, bidx, fm in group)
        vts = [vt_ref[0, :, pl.ds(pl.multiple_of(kt * tk, tk), tk)] for kt, _, _ in group]
        lhs = jnp.concatenate([vts[0] if len(vts) == 1 else jnp.concatenate(vts, axis=1),
                               jnp.ones((ONES_ROWS, rows), BF)], axis=0)
        for c0 in range(0, lw, PV_CHUNK):
            cols = slice(c0, c0 + PV_CHUNK)
            if plain:
                s = s_scr[r, 0:rows, cols]
                mt = mt_scr[r, :, cols]
            else:
                parts = []
                for t, (_, bidx, fm) in enumerate(group):
                    st = s_scr[r, t * tk:(t + 1) * tk, cols]
                    if bidx is not None:
                        q, l0 = divmod(c0, tq)
                        st = st + bias_ref[q, bidx, :, l0:l0 + PV_CHUNK]
                    if fm is not None:
                        st = st + jnp.where(fm, MASK_VALUE, 0.0)
                    parts.append(st)
                s = parts[0] if len(parts) == 1 else jnp.concatenate(parts, axis=0)
                mt = jnp.max(s, axis=0, keepdims=True)
            m_old = m_ref[:, cols]
            m_new = jnp.maximum(m_old, mt)
            alpha = jnp.exp2(m_old - m_new)
            p = jnp.exp2(s - m_new).astype(BF)
            m_ref[:, cols] = m_new
            acc_ref[:, cols] = alpha * acc_ref[:, cols] + _dot(lhs, p)

    def trip(r, cur, nxt, next_tile=False):
        if not isinstance(r, int):
            for val in (0, 1):
                @pl.when(r == val)
                def _(val=val):
                    trip(val, cur, nxt, next_tile)
            return
        if nxt:
            stage_qk(nxt, 1 - r, next_tile)
        stage_softmax_pv(cur, r)

    n_tile = tq // tk
    a0 = near[0][0]
    last_kt = k_ref.shape[0] // tk - 1
    lo_values = sorted({max(ii * n_tile + a0, 0) % grp for ii in range(2 * grp)}) if mode == "causal" else [0]

    def tile_plan(ii):
        base = ii * n_tile
        near_tiles = [(jnp.maximum(base + a, 0), bidx, (base + a < 0) if a < 0 else None) for a, bidx in near]
        n_far = jnp.maximum(base + a0, 0) if mode == "causal" else 0
        trips, lo = n_far // grp, n_far % grp
        tails = {v: [(n_far - v + t, None, None) for t in range(v)] + near_tiles for v in lo_values}
        first = []
        for t in range(grp):
            kt_near = sum(jnp.where(lo == v, tails[v][min(t, len(tails[v]) - 1)][0], 0) for v in lo_values)
            first.append((jnp.where(trips > 0, t, kt_near), None, None))
        return trips, lo, tails, first

    trips, lo, tails, first = tile_plan(i)
    far_group = lambda u: [(jnp.minimum(grp * u + t, last_kt), None, None) for t in range(grp)]
    if mode == "causal":
        r0 = jnp.where(i == 0, 0, r0_ref[0])

        @pl.when(i == 0)
        def _():
            stage_qk(first, 0)

        def body(u, c):
            trip((r0 + u) % 2, far_group(u), far_group(u + 1))
            return c
        lax.fori_loop(0, trips, body, 0)
        first_next = tile_plan(i_next)[3]
    else:
        r0 = 0
        stage_qk(first, 0)

    def run_tail(tail):
        groups = [tail[n:n + grp] for n in range(0, len(tail), grp)]
        for e, group in enumerate(groups):
            r_e = (r0 + trips + e) % 2
            if e + 1 < len(groups):
                trip(r_e, group, groups[e + 1])
            elif mode == "causal":
                trip(r_e, group, first_next, next_tile=True)
                r0_ref[0] = 1 - r_e
            else:
                trip(r_e, group, None)

    if len(lo_values) == 1:
        run_tail(tails[lo_values[0]])
    else:
        for v in lo_values:
            @pl.when(lo == v)
            def _(v=v):
                run_tail(tails[v])

    acc = acc_ref[0:vdim, :]
    l = acc_ref[vdim:vdim + 1, :]
    if final == "da":
        lamv = lam_ref[...]
        lam = (jnp.exp(jnp.sum(lamv[0:1] * lamv[1:2], axis=1, keepdims=True))
               - jnp.exp(jnp.sum(lamv[2:3] * lamv[3:4], axis=1, keepdims=True)) + lam_init)
        o = acc[:, :tq] / l[:, :tq] - lam * (acc[:, tq:] / l[:, tq:])
        o = o * lax.rsqrt(jnp.mean(o * o, axis=0, keepdims=True) + NORM_EPS) * subg_ref[...] * (1.0 - lam_init)
        o_ref[0] = o.astype(o_ref.dtype)
    else:
        for r in range(nq_stack):
            o_ref[0, r * vdim:(r + 1) * vdim, :] = (acc[:, r * tq:(r + 1) * tq] / l[:, r * tq:(r + 1) * tq]).astype(o_ref.dtype)


def _attention(q_arr, k_arr, vt, bias, *, B, S, n_prog, q_spec, k_blk, v_blk, bias_blk, nq_stack, vdim, mode,
               out_rows, out_dtype, q_mask, tq, near, grp=2, selb=None, final="plain", lam=None, subg=None, lam_init=0.0):
    tk = ATT_TK
    nq = S // tq
    lw = nq_stack * tq
    q_w, q_blk = q_spec
    nxt = lambda i: jnp.minimum(i + 1, nq - 1)
    ins = [q_arr, q_arr, k_arr, vt, bias]
    in_specs = [
        pl.BlockSpec((tq, q_w), lambda b, h, i: (b * nq + i, q_blk(h))),
        pl.BlockSpec((tq, q_w), lambda b, h, i: (b * nq + nxt(i), q_blk(h))),
        pl.BlockSpec((S, LANES), lambda b, h, i: (b, k_blk(h))),
        pl.BlockSpec((1, vdim, S), lambda b, h, i: (b, v_blk(h), 0)),
        pl.BlockSpec((nq_stack,) + bias.shape[1:], lambda b, h, i: (bias_blk(h), 0, 0, 0)),
    ]
    if selb is not None:
        ins += [selb, selb]
        in_specs += [pl.BlockSpec((1, 1, selb.shape[2], tq), lambda b, h, i: (b, h, 0, i)),
                     pl.BlockSpec((1, 1, selb.shape[2], tq), lambda b, h, i: (b, h, 0, nxt(i)))]
    if final == "da":
        ins += [lam, subg]
        in_specs += [pl.BlockSpec(lam.shape, lambda b, h, i: (0, 0)), pl.BlockSpec(subg.shape, lambda b, h, i: (0, 0))]
    body = functools.partial(_attn_body, tq=tq, grp=grp, nq_stack=nq_stack, vdim=vdim, mode=mode, use_sel=selb is not None,
                             final=final, lam_init=lam_init, q_mask=q_mask, near=near)
    return pl.pallas_call(
        body, grid=(B, n_prog, nq), in_specs=in_specs,
        out_specs=pl.BlockSpec((1, out_rows, tq), lambda b, h, i: (b, h, i)),
        out_shape=jax.ShapeDtypeStruct((B, out_rows * n_prog, S), out_dtype),
        scratch_shapes=[pltpu.VMEM((2, lw, LANES), BF), pltpu.VMEM((1, lw), F32), pltpu.VMEM((vdim + ONES_ROWS, lw), F32),
                        pltpu.VMEM((2, grp * tk, lw), F32), pltpu.VMEM((2, 1, lw), F32), pltpu.SMEM((1,), jnp.int32)],
        compiler_params=_cparams("arbitrary", "arbitrary", "arbitrary"), name="attn_" + mode + "_" + final)(*ins)


def _compress_body(x_ref, pe_ref, w1_ref, w2_ref, o_ref):
    x = x_ref[0, 0, 0]
    n, half = x.shape
    pe = pe_ref[0]
    first = _dot((x + pe[:, :half]).astype(BF), w1_ref[0, :half, :])
    second = _dot((x + pe[:, half:]).astype(BF), w1_ref[0, half:, :])
    hcur = first + pltpu.roll(second, n - 1, axis=0)
    hcur = 0.5 * hcur * (1.0 + jnp.tanh(math.sqrt(2.0 / math.pi) * (hcur + 0.044715 * (hcur * hcur * hcur))))
    o_ref[0, 0, 0] = _dot(hcur.astype(BF), w2_ref[0])


def _compress(chunks, pe, w1, w2):
    _, B, G, n, K = chunks.shape
    return pl.pallas_call(
        _compress_body, grid=(2, B, G),
        in_specs=[pl.BlockSpec((1, 1, 1, n, K), lambda a, b, g: (a, b, g, 0, 0)),
                  pl.BlockSpec((1, 1, 2 * K), lambda a, b, g: (a, 0, 0)),
                  pl.BlockSpec((1,) + w1.shape[1:], lambda a, b, g: (a, 0, 0)),
                  pl.BlockSpec((1,) + w2.shape[1:], lambda a, b, g: (a, 0, 0))],
        out_specs=pl.BlockSpec((1, 1, 1, n, HEAD_DIM), lambda a, b, g: (a, b, g, 0, 0)),
        out_shape=jax.ShapeDtypeStruct((2, B, G, n, HEAD_DIM), F32),
        compiler_params=_cparams("arbitrary", "arbitrary", "arbitrary"), name="nsa_compress")(chunks, pe, w1, w2)


def _cmp_body(q_ref, kc_ref, vct_ref, bc_ref, ov_ref, o_ref, selb_ref, s_ref, imp_ref):
    tq = ATT_TQ
    i = pl.program_id(2)
    n_cmp = kc_ref.shape[2]
    n_sel = selb_ref.shape[2]
    band = bc_ref.shape[1]
    bc = jnp.concatenate([bc_ref[r] for r in range(NSA_REP)], axis=1)
    qt = q_ref[...]
    lane = lax.broadcasted_iota(jnp.int32, (tq, LANES), 1)
    keep = (lane // HEAD_DIM) == (pl.program_id(1) % 2)
    qm = jnp.concatenate(
        [jnp.where(keep, qt[:, r * LANES:(r + 1) * LANES], jnp.zeros((tq, LANES), BF)) for r in range(NSA_REP)], axis=0)

    def attend(rows):
        s = _dot_nt(kc_ref[0, 0, 0:rows, :], qm)
        row = lax.broadcasted_iota(jnp.int32, s.shape, 0)
        s_ref[0:rows, :] = jnp.where(row < (band // 2) * (i + 1), s, MASK_VALUE)

        @pl.when(i == 0)
        def _():
            s_ref[0:band // 2, :] = s_ref[0:band // 2, :] + bc[band // 2:, :]

        @pl.when(i > 0)
        def _():
            r0 = pl.multiple_of((band // 2) * (i - 1), band // 2)
            s_ref[pl.ds(r0, band), :] = s_ref[pl.ds(r0, band), :] + bc

        ps = None
        for r in range(NSA_REP):
            s = s_ref[0:rows, r * tq:(r + 1) * tq]
            m = jnp.max(s, axis=0, keepdims=True)
            e = jnp.exp2(s - m)
            l = jnp.sum(e, axis=0, keepdims=True)
            inv = jnp.where(m > 0.5 * MASK_VALUE, 1.0 / l, 0.0)
            p = e * inv
            o_ref[0, r * HEAD_DIM:(r + 1) * HEAD_DIM, :] = _dot(vct_ref[0, 0, :, 0:rows], p.astype(BF))
            ps = p if ps is None else ps + p
        p_hi = ps.astype(BF)
        r1 = ps - p_hi.astype(F32)
        p_mid = r1.astype(BF)
        p_lo = (r1 - p_mid.astype(F32)).astype(BF)
        ov = ov_ref[:, 0:rows]
        imp_ref[...] = _dot(ov, p_hi) + _dot(ov, p_mid) + _dot(ov, p_lo)

    def select(rows):
        imp = imp_ref[0:rows, :]
        j = lax.broadcasted_iota(jnp.int32, (rows, tq), 0)
        t = i * tq + lax.broadcasted_iota(jnp.int32, (rows, tq), 1)
        cur = t // NSA_SEL_LEN
        forced = (j == 0) | (j == cur) | (j == cur - 1)
        w = jnp.where(j <= cur, jnp.where(forced, NSA_FORCE_SCORE, imp), -1.0)
        jf = j.astype(F32)
        for _ in range(min(NSA_TOP_N, rows)):
            mx = jnp.max(w, axis=0, keepdims=True)
            idx = jnp.min(jnp.where(w == mx, jf, float(rows)), axis=0, keepdims=True)
            w = jnp.where(jf == idx, -2.0, w)
        selb_ref[0, 0, 0:rows, :] = jnp.where(w == -2.0, 0.0, MASK_VALUE)
        if rows < n_sel:
            selb_ref[0, 0, rows:n_sel, :] = jnp.full((n_sel - rows, tq), MASK_VALUE, F32)

    step = min(CMP_ROW_STEP, n_cmp)
    tiles_per_step = step // (band // 2)
    sel_per_cmp = NSA_SEL_LEN // NSA_CMP_STRIDE
    for v in range(n_cmp // step):
        @pl.when((i >= v * tiles_per_step) & (i < (v + 1) * tiles_per_step))
        def _(v=v):
            attend((v + 1) * step)
            select((v + 1) * step // sel_per_cmp)


def _cmp_attention(qn, kc, vct, bc, ov, *, B, S, q_coloff):
    tq = ATT_TQ
    nq = S // tq
    n_cmp = kc.shape[2]
    n_sel = S // NSA_SEL_LEN
    return pl.pallas_call(
        _cmp_body, grid=(B, NSA_GROUPS, nq),
        in_specs=[pl.BlockSpec((tq, NSA_REP * LANES), lambda b, g, i: (b * nq + i, q_coloff + g // 2)),
                  pl.BlockSpec((1, 1, n_cmp, LANES), lambda b, g, i: (b, g, 0, 0)),
                  pl.BlockSpec((1, 1, HEAD_DIM, n_cmp), lambda b, g, i: (b, g, 0, 0)),
                  pl.BlockSpec((NSA_REP,) + bc.shape[1:], lambda b, g, i: (g, 0, 0)),
                  pl.BlockSpec(ov.shape, lambda b, g, i: (0, 0))],
        out_specs=[pl.BlockSpec((1, NSA_REP * HEAD_DIM, tq), lambda b, g, i: (b, g, i)),
                   pl.BlockSpec((1, 1, n_sel, tq), lambda b, g, i: (b, g, 0, i))],
        out_shape=[jax.ShapeDtypeStruct((B, D_MODEL, S), F32), jax.ShapeDtypeStruct((B, NSA_GROUPS, n_sel, S), F32)],
        scratch_shapes=[pltpu.VMEM((n_cmp, NSA_REP * tq), F32), pltpu.VMEM((n_sel, tq), F32)],
        compiler_params=_cparams("arbitrary", "arbitrary", "arbitrary"), name="nsa_cmp_topk")(qn, kc, vct, bc, ov)


def _outproj_body(*refs, n_o):
    it = iter(refs)
    o_refs = [next(it) for _ in range(n_o)]
    gt_ref = next(it) if n_o > 1 else None
    w_ref, x_ref, g_ref, out_ref = (next(it) for _ in range(4))
    if n_o == 1:
        ot = o_refs[0][0]
    else:
        gts = gt_ref[0]
        parts = []
        for hd in range(N_HEADS):
            rows = slice(hd * HEAD_DIM, (hd + 1) * HEAD_DIM)
            acc = gts[hd:hd + 1, :] * o_refs[0][0, rows, :]
            for b in range(1, n_o):
                acc = acc + gts[b * N_HEADS + hd:b * N_HEADS + hd + 1, :] * o_refs[b][0, rows, :]
            parts.append(acc.astype(BF))
        ot = jnp.concatenate(parts, axis=0)
    y = _dot_tn(ot, w_ref[...])
    out_ref[...] = x_ref[...] + _rms(y, g_ref[...])


def _outproj(o_list, gt, w, x2d, g, *, B, S, tm=512):
    M, D = x2d.shape
    ns = S // tm
    ins = list(o_list)
    in_specs = [pl.BlockSpec((1, D, tm), lambda b, s: (b, 0, s)) for _ in o_list]
    if gt is not None:
        ins.append(gt)
        in_specs.append(pl.BlockSpec((1, gt.shape[1], tm), lambda b, s: (b, 0, s)))
    ins += [w, x2d, g]
    in_specs += [pl.BlockSpec(w.shape, lambda b, s: (0, 0)),
                 pl.BlockSpec((tm, D), lambda b, s: (b * ns + s, 0)),
                 pl.BlockSpec(g.shape, lambda b, s: (0, 0))]
    return pl.pallas_call(
        functools.partial(_outproj_body, n_o=len(o_list)), grid=(B, ns), in_specs=in_specs,
        out_specs=pl.BlockSpec((tm, D), lambda b, s: (b * ns + s, 0)),
        out_shape=jax.ShapeDtypeStruct((M, D), F32),
        compiler_params=_cparams("arbitrary", "arbitrary"), name="outproj")(*ins)


def _mlp_body(*refs, mixer):
    it = iter(refs)
    x_ref = next(it)
    if mixer:
        ot_ref, wo_ref, g1_ref = next(it), next(it), next(it)
    g2_ref, w1_ref, w2_ref, g3_ref, wg_ref, p_ref, wp_ref, out_ref, h_ref, acc_ref = (next(it) for _ in range(10))
    f = pl.program_id(1)

    @pl.when(f == 0)
    def _():
        x1 = x_ref[...]
        if mixer:
            x1 = x1 + _rms(_dot_tn(ot_ref[0], wo_ref[...]), g1_ref[...])
        out_ref[...] = x1
        h_ref[...] = _rms(x1, g2_ref[...]).astype(BF)
        acc_ref[...] = jnp.zeros_like(acc_ref)

    a = jnp.maximum(_dot(h_ref[...], w1_ref[0].astype(BF)), 0.0)
    acc_ref[...] += _dot((a * a).astype(BF), w2_ref[0].astype(BF))

    @pl.when(f == pl.num_programs(1) - 1)
    def _():
        x2 = out_ref[...] + _rms(acc_ref[...], g3_ref[...])
        gate = jax.nn.sigmoid(_dot(x2.astype(BF), wg_ref[...]))
        out_ref[...] = x2 + gate * _dot(p_ref[...].astype(BF), wp_ref[...])


def _mlp_ple(x2d, g2, w1, w2, layer, g3, wg, p2d, wp, *, mixer=None, S=None, tm=1024, tf=512):
    M, D = x2d.shape
    FF = w1.shape[2]
    PD = p2d.shape[1]
    const = lambda a: pl.BlockSpec(a.shape, lambda m, f: (0, 0))
    ins = [x2d]
    in_specs = [pl.BlockSpec((tm, D), lambda m, f: (m, 0))]
    if mixer is not None:
        ns = S // tm
        ins += list(mixer)
        in_specs += [pl.BlockSpec((1, D, tm), lambda m, f: (m // ns, 0, m % ns)), const(mixer[1]), const(mixer[2])]
    ins += [g2, w1, w2, g3, wg, p2d, wp]
    in_specs += [const(g2), pl.BlockSpec((1, D, tf), lambda m, f: (layer, 0, f)),
                 pl.BlockSpec((1, tf, D), lambda m, f: (layer, f, 0)),
                 const(g3), const(wg), pl.BlockSpec((tm, PD), lambda m, f: (m, 0)), const(wp)]
    return pl.pallas_call(
        functools.partial(_mlp_body, mixer=mixer is not None), grid=(M // tm, FF // tf), in_specs=in_specs,
        out_specs=pl.BlockSpec((tm, D), lambda m, f: (m, 0)),
        out_shape=jax.ShapeDtypeStruct((M, D), F32),
        scratch_shapes=[pltpu.VMEM((tm, D), BF), pltpu.VMEM((tm, D), F32)],
        compiler_params=_cparams("arbitrary", "arbitrary"), name="mlp_ple")(*ins)


def _t5_bucket(dist):
    n = jnp.maximum(dist, 0)
    max_exact = REL_BUCKETS // 2
    nf = jnp.maximum(n, 1).astype(F32)
    large = max_exact + (jnp.log(nf / max_exact) / math.log(REL_MAX_DIST / max_exact)
                         * (REL_BUCKETS - max_exact)).astype(jnp.int32)
    large = jnp.minimum(large, REL_BUCKETS - 1)
    return jnp.where(n < max_exact, n, large)


def _bias_by_distance(rel_bias):
    tb = rel_bias[_t5_bucket(jnp.arange(REL_TABLE_LEN))].astype(F32)
    return ((tb - rel_bias[REL_BUCKETS - 1].astype(F32)[None, :]) * LOG2E).T


def _toeplitz(fn, rows, cols, off, row_stride=1):
    n = row_stride * rows + cols
    d = jnp.arange(n)
    d = jnp.where(d < cols, d, d - n)
    v = fn(d + off)
    flat = jnp.tile(v, (1,) * (v.ndim - 1) + (rows,))[..., :rows * (n - row_stride)]
    return flat.reshape(v.shape[:-1] + (rows, n - row_stride))[..., :cols]


def _bias_fn(tbs, hi_limit=None):
    def fn(d):
        val = jnp.zeros((1,) + d.shape, F32) if tbs is None else tbs[:, jnp.clip(d, 0, REL_TABLE_LEN - 1)]
        bad = d < 0
        if hi_limit is not None:
            bad = bad | (d >= hi_limit)
        return jnp.where(bad[None, :], MASK_VALUE, val)
    return fn


def _near_tables(tbs, tq, offsets, hi_limit=None):
    return jnp.stack([_toeplitz(_bias_fn(tbs, hi_limit), ATT_TK, tq, -a * ATT_TK) for a in offsets], axis=1)


def _cmp_band_table(tbs):
    tq = ATT_TQ
    band = 2 * (tq // NSA_CMP_STRIDE)
    return _toeplitz(_bias_fn(tbs), band, tq, tq - (NSA_CMP_LEN - 1), row_stride=NSA_CMP_STRIDE)


def _nsa_q_perm():
    cols = []
    for p in range(NSA_GROUPS // 2):
        for r in range(NSA_REP):
            for g in (2 * p, 2 * p + 1):
                h = g * NSA_REP + r
                cols.extend(range(h * HEAD_DIM, (h + 1) * HEAD_DIM))
    return jnp.asarray(cols, jnp.int32)


def _row(v):
    return v.reshape(1, -1).astype(F32)


def kernel(x, p, rel_bias, norm_g, mlp_w1, mlp_w2, ple_w, ple_gate_w, da_w_in, da_lambda, da_subln, da_w_out,
           nsa_w_in, nsa_cmp_pe, nsa_cmp_w1, nsa_cmp_w2, nsa_w_out, fox_w_in, fox_b_f, fox_w_out):
    B, S, D = x.shape
    depth = p.shape[0]
    scale = HEAD_DIM ** -0.5 * LOG2E
    tbs = _bias_by_distance(rel_bias)
    da_near = tuple(range(-1, DA_TQ // ATT_TK))
    da_bias = _near_tables(tbs, DA_TQ, da_near)
    x2d = x.reshape(B * S, D)
    ia = ib = ic = 0
    for i in range(depth):
        g = norm_g[i]
        kind = i % N_MIXERS
        if kind == 0:
            lam_init = 0.8 - 0.6 * math.exp(-0.3 * i)
            w_in = da_w_in[ia]
            wn = jnp.concatenate([w_in[:, :D] * scale, w_in[:, D:2 * D]], axis=1).astype(BF)
            wt = w_in[:, 2 * D:].T.astype(BF)
            qk, vt = _inproj(x2d, _row(g[0]), wn, jnp.zeros((1, 2 * D), F32), wt, B=B, S=S)
            ot = _attention(qk, qk, vt, da_bias, B=B, S=S, n_prog=DA_HEADS, q_spec=(LANES, lambda h: h),
                            k_blk=lambda h: DA_HEADS + h, v_blk=lambda h: h, bias_blk=lambda h: h, nq_stack=2,
                            vdim=2 * HEAD_DIM, mode="causal", out_rows=2 * HEAD_DIM, out_dtype=BF, q_mask="da",
                            tq=DA_TQ, near=[(da_near[0] - 1, None)] + [(a, n) for n, a in enumerate(da_near)], grp=CAUSAL_GROUP,
                            final="da", lam=da_lambda[ia].astype(F32), subg=da_subln[ia].reshape(-1, 1).astype(F32),
                            lam_init=lam_init)
            mixer = (ot, da_w_out[ia].astype(BF), _row(g[1]))
            ia += 1
        elif kind == 1:
            w_in = nsa_w_in[ib]
            kvd = NSA_GROUPS * HEAD_DIM
            wq = w_in[:, :D][:, _nsa_q_perm()] * scale
            kv = [w_in[:, D + a * kvd:D + (a + 1) * kvd] for a in range(6)]
            wn = jnp.concatenate([wq, kv[2], kv[4]], axis=1).astype(BF)
            wt = jnp.concatenate([kv[3], kv[5]], axis=1).T.astype(BF)
            wf32 = jnp.concatenate([kv[0], kv[1]], axis=1).astype(BF)
            wgate = w_in[:, D + 6 * kvd:].reshape(D, N_HEADS, 3).transpose(2, 1, 0).reshape(3 * N_HEADS, D).astype(BF)
            qn, vt, cmp_in, gates_t = _inproj(x2d, _row(g[0]), wn, jnp.zeros((1, wn.shape[1]), F32), wt, B=B, S=S,
                                              wf32=wf32, wg=wgate)
            n_chunk = S // NSA_CMP_STRIDE
            cm = cmp_in.reshape(B, n_chunk, NSA_CMP_STRIDE, 2, NSA_GROUPS, HEAD_DIM).transpose(3, 0, 4, 1, 2, 5)
            cm = cm.reshape(2, B, NSA_GROUPS, n_chunk, NSA_CMP_STRIDE * HEAD_DIM)
            pe = nsa_cmp_pe[ib].reshape(2, 1, NSA_CMP_LEN * HEAD_DIM).astype(F32)
            kvc = _compress(cm, pe, nsa_cmp_w1[ib].astype(BF), nsa_cmp_w2[ib].astype(BF))
            kc = jnp.concatenate([kvc[0], kvc[0]], axis=-1).astype(BF)
            vct = kvc[1].transpose(0, 1, 3, 2).astype(BF)
            jj = jnp.arange(S // NSA_SEL_LEN)[:, None]
            nn = jnp.arange(n_chunk)[None, :]
            n_cmp_blocks = (S - NSA_CMP_LEN) // NSA_CMP_STRIDE + 1
            ov = ((nn * NSA_CMP_STRIDE < (jj + 1) * NSA_SEL_LEN) & (nn * NSA_CMP_STRIDE + NSA_CMP_LEN - 1 >= jj * NSA_SEL_LEN)
                  & (nn < n_cmp_blocks)).astype(BF)
            cmp_band = _cmp_band_table(tbs)
            oc_t, selb = _cmp_attention(qn, kc, vct, cmp_band, ov, B=B, S=S, q_coloff=0)
            nsa_kw = dict(B=B, S=S, n_prog=NSA_GROUPS, q_spec=(NSA_REP * LANES, lambda h: h // 2),
                          bias_blk=lambda h: h, nq_stack=NSA_REP, vdim=HEAD_DIM, out_rows=NSA_REP * HEAD_DIM,
                          out_dtype=F32, q_mask="group_parity", tq=ATT_TQ)
            kblk0 = D // LANES
            sel_bias = _near_tables(tbs, ATT_TQ, (-1, 0))
            win_bias = _near_tables(tbs, ATT_TQ, (-2, -1, 0), hi_limit=NSA_WINDOW)
            os_t = _attention(qn, qn, vt, sel_bias, mode="causal", near=[(-1, 0), (0, 1)], grp=CAUSAL_GROUP,
                              k_blk=lambda h: kblk0 + h // 2, v_blk=lambda h: h, selb=selb, **nsa_kw)
            ow_t = _attention(qn, qn, vt, win_bias, mode="window", near=[(-2, 0), (-1, 1), (0, 2)],
                              k_blk=lambda h: kblk0 + 2 + h // 2, v_blk=lambda h: NSA_GROUPS + h, **nsa_kw)
            x2d = _outproj([oc_t, os_t, ow_t], gates_t, nsa_w_out[ib].astype(BF), x2d, _row(g[1]), B=B, S=S)
            mixer = None
            ib += 1
        else:
            w_in = fox_w_in[ic]
            wn = jnp.concatenate([w_in[:, :D] * scale, w_in[:, D:2 * D]], axis=1).astype(BF)
            hh = jnp.arange(N_HEADS)
            aug0 = hh * LANES + jnp.where(hh % 2 == 0, HEAD_DIM, 0)
            sel = jnp.arange(2 * D)[None, :] - aug0[:, None]
            aug = jnp.any((sel >= 0) & (sel < 3), axis=0).astype(F32).reshape(1, 2 * D)
            bn = jnp.concatenate([aug, jnp.zeros((1, 2 * D), F32)], axis=1)
            wf = w_in[:, 3 * D:]
            wf3 = jnp.concatenate([wf, wf, wf, jnp.zeros((D, LANES - 3 * N_HEADS), F32)], axis=1).astype(BF)
            bf3 = jnp.concatenate([fox_b_f[ic]] * 3 + [jnp.zeros((LANES - 3 * N_HEADS,), F32)]).reshape(1, LANES).astype(F32)
            rows = jnp.arange(LANES)[:, None]
            tgt = jnp.where(rows < 3 * N_HEADS, aug0[rows % N_HEADS] + rows // N_HEADS, -1)
            pm = jnp.where(jnp.arange(2 * D)[None, :] == tgt, -1.0, 0.0)
            qk, vt = _inproj(x2d, _row(g[0]), wn, bn, w_in[:, 2 * D:3 * D].T.astype(BF), B=B, S=S,
                             fox_extra=[wf3, bf3, pm.astype(BF)])
            fox_near = tuple(range(FOX_TQ // ATT_TK))
            mask_t = _near_tables(None, FOX_TQ, fox_near)
            ot = _attention(qk, qk, vt, mask_t, B=B, S=S, n_prog=N_HEADS, q_spec=(LANES, lambda h: h),
                            k_blk=lambda h: N_HEADS + h, v_blk=lambda h: h, bias_blk=lambda h: 0, nq_stack=1,
                            vdim=HEAD_DIM, mode="causal", out_rows=HEAD_DIM, out_dtype=BF, q_mask="none",
                            tq=FOX_TQ, near=[(a, a) for a in fox_near], grp=CAUSAL_GROUP)
            mixer = (ot, fox_w_out[ic].astype(BF), _row(g[1]))
            ic += 1
        x2d = _mlp_ple(x2d, _row(g[2]), mlp_w1, mlp_w2, i, _row(g[3]),
                       ple_gate_w[i].astype(BF), p[i].reshape(B * S, -1), ple_w[i].astype(BF), mixer=mixer, S=S)
    return x2d.reshape(B, S, D)
```
